```python
import jax, jax.numpy as jnp
from jax import lax
import numpy as np

D_MODEL = 2048
BATCH = 2
SEQ = 4096
DEPTH = 4

CHUNK = 64
N_MIXERS = 3
EPS = 1e-6
N_A = (DEPTH + 2) // 3
N_B = (DEPTH + 1) // 3
N_C = DEPTH // 3
POOL_WINDOWS = (2, 4, 8, 16)
N_POOL_GROUPS = len(POOL_WINDOWS)
POOL_GROUP = D_MODEL // N_POOL_GROUPS
HEAD_DIM = 128
N_HEADS = D_MODEL // HEAD_DIM
N_KV_HEADS = 4
KV_REP = N_HEADS // N_KV_HEADS
IDX_HEADS = 16
IDX_DIM = 64
INDEX_TOPK = 256
Q_BLOCK = 128
ROPE_THETA = 500000.0
ROT_FRACTION = 4
B_SIZES = (N_HEADS * HEAD_DIM, N_KV_HEADS * HEAD_DIM, N_KV_HEADS * HEAD_DIM,
           IDX_HEADS * IDX_DIM, IDX_DIM, IDX_HEADS)
B_COLS = sum(B_SIZES)
B_SPLITS = [int(c) for c in np.cumsum(B_SIZES)[:-1]]
SGU_BLOCK = 128
SGU_GROUPS = 8
SGU_WIDTH = D_MODEL
SGU_GROUP_DIM = SGU_WIDTH // SGU_GROUPS
D_FF = 4 * D_MODEL

kernel_name = "hybrid_pool_dsa_sgu_trunk"


def rmsnorm(x, g):
    xf = x.astype(jnp.float32)
    y = xf * lax.rsqrt(jnp.mean(xf * xf, axis=-1, keepdims=True) + EPS)
    return (y * g.astype(jnp.float32)).astype(x.dtype)


def rope(x, pos):
    rot = x.shape[-1] // ROT_FRACTION
    half = rot // 2
    inv = ROPE_THETA ** (-jnp.arange(half, dtype=jnp.float32) / half)
    ang = pos.astype(jnp.float32)[:, None] * inv[None, :]
    cos = jnp.cos(ang)[None, :, None, :]
    sin = jnp.sin(ang)[None, :, None, :]
    xf = x[..., :rot].astype(jnp.float32)
    x1, x2 = xf[..., :half], xf[..., half:]
    rotated = jnp.concatenate([x1 * cos - x2 * sin, x2 * cos + x1 * sin], axis=-1).astype(x.dtype)
    return jnp.concatenate([rotated, x[..., rot:]], axis=-1)


def pool_mixer(h, w_groups, scale):
    b, s, d = h.shape
    hf = h.astype(jnp.float32)
    cs = jnp.cumsum(hf, axis=1)
    t1 = jnp.arange(1, s + 1, dtype=jnp.float32)[None, :, None]
    outs = []
    for g, w in enumerate(POOL_WINDOWS):
        sl = slice(g * POOL_GROUP, (g + 1) * POOL_GROUP)
        c = cs[:, :, sl]
        lower = jnp.pad(c[:, :s - w], ((0, 0), (w, 0), (0, 0)))
        mean = (c - lower) / jnp.minimum(t1, float(w))
        outs.append(mean - hf[:, :, sl])
    p = jnp.stack(outs, axis=2).astype(h.dtype)
    y = jnp.einsum('bsgc,gce->bsge', p, w_groups).reshape(b, s, d)
    return y * scale


def sparse_attention(h, w_in, q_gain, k_gain, w_o, pos):
    b, s, d = h.shape
    n_sel = min(INDEX_TOPK, s // 4)
    nb = s // Q_BLOCK
    q, k, v, qi, ki, wi = jnp.split(h @ w_in, B_SPLITS, axis=-1)
    q = rope(rmsnorm(q.reshape(b, s, N_HEADS, HEAD_DIM), q_gain), pos)
    k = rope(rmsnorm(k.reshape(b, s, N_KV_HEADS, HEAD_DIM), k_gain), pos)
    v = v.reshape(b, s, N_KV_HEADS, HEAD_DIM)
    qi = rope(qi.reshape(b, s, IDX_HEADS, IDX_DIM), pos)
    ki = rope(ki[:, :, None, :], pos)[:, :, 0, :]
    wi = wi * (IDX_HEADS ** -0.5)
    key_chunk = jnp.arange(s) // CHUNK

    def to_blocks(a):
        return a.reshape(b, nb, Q_BLOCK, *a.shape[2:]).swapaxes(0, 1)

    def block(args):
        qb, qib, wib, t0 = args
        q_chunk = (t0 + jnp.arange(Q_BLOCK)) // CHUNK
        adm = key_chunk[None, :] <= q_chunk[:, None]
        rel = jax.nn.relu(jnp.einsum('bqhd,bsd->bqhs', qib, ki).astype(jnp.float32) * (IDX_DIM ** -0.5))
        score = jnp.einsum('bqhs,bqh->bqs', rel, wib.astype(jnp.float32))
        score = jnp.where(adm[None], score, -jnp.inf)
        _, idx = lax.top_k(score, n_sel)
        valid = key_chunk[idx] <= q_chunk[None, :, None]
        kg = jax.vmap(lambda a, i: a[i])(k, idx)
        vg = jax.vmap(lambda a, i: a[i])(v, idx)
        qg = qb.reshape(b, Q_BLOCK, N_KV_HEADS, KV_REP, HEAD_DIM)
        lg = jnp.einsum('bqgrd,bqkgd->bqgrk', qg, kg).astype(jnp.float32) * (HEAD_DIM ** -0.5)
        lg = jnp.where(valid[:, :, None, None, :], lg, -jnp.inf)
        p = jax.nn.softmax(lg, axis=-1).astype(vg.dtype)
        o = jnp.einsum('bqgrk,bqkgd->bqgrd', p, vg)
        return o.reshape(b, Q_BLOCK, N_HEADS * HEAD_DIM)

    starts = jnp.arange(nb) * Q_BLOCK
    o = lax.map(block, (to_blocks(q), to_blocks(qi), to_blocks(wi), starts))
    o = o.swapaxes(0, 1).reshape(b, s, N_HEADS * HEAD_DIM)
    return o @ w_o


def spatial_gating(h, w_in, b_in, v_gain, w_s, b_s, w_o):
    b, s, d = h.shape
    nblk = s // SGU_BLOCK
    z = jax.nn.gelu(h @ w_in + b_in, approximate=False)
    u, v = jnp.split(z, 2, axis=-1)
    v = rmsnorm(v, v_gain)
    vb = v.reshape(b, nblk, SGU_BLOCK, SGU_GROUPS, SGU_GROUP_DIM)
    i = jnp.arange(SGU_BLOCK)
    mask = (i[None, :] // CHUNK) <= (i[:, None] // CHUNK)
    ws = jnp.where(mask[None], w_s, 0.0).astype(v.dtype)
    mixed = jnp.einsum('gij,bnjgc->bnigc', ws, vb) + b_s.T[None, None, :, :, None]
    return (u * mixed.reshape(b, s, SGU_WIDTH)) @ w_o


def setup_inputs(seed: int = 0) -> dict:
    key = jax.random.key(seed)
    ks = jax.random.split(key, 20)
    f32 = jnp.float32
    nrm = lambda k, shape, fan: jax.random.normal(k, shape, f32) * (fan ** -0.5)
    gain = lambda k, shape: 1.0 + 0.02 * jax.random.normal(k, shape, f32)
    return {
        "x": jax.random.normal(ks[0], (BATCH, SEQ, D_MODEL), f32),
        "norm_mix": gain(ks[1], (DEPTH, D_MODEL)),
        "norm_ffn": gain(ks[2], (DEPTH, D_MODEL)),
        "pool_w": nrm(ks[3], (N_A, N_POOL_GROUPS, POOL_GROUP, POOL_GROUP), POOL_GROUP),
        "pool_scale": 1.0 + 0.1 * jax.random.normal(ks[4], (N_A, D_MODEL), f32),
        "attn_w_in": nrm(ks[5], (N_B, D_MODEL, B_COLS), D_MODEL),
        "attn_q_gain": gain(ks[6], (N_B, HEAD_DIM)),
        "attn_k_gain": gain(ks[7], (N_B, HEAD_DIM)),
        "attn_w_o": nrm(ks[8], (N_B, N_HEADS * HEAD_DIM, D_MODEL), N_HEADS * HEAD_DIM),
        "sgu_w_in": nrm(ks[9], (N_C, D_MODEL, 2 * SGU_WIDTH), D_MODEL),
        "sgu_b_in": 0.02 * jax.random.normal(ks[10], (N_C, 2 * SGU_WIDTH), f32),
        "sgu_v_gain": gain(ks[11], (N_C, SGU_WIDTH)),
        "sgu_w_s": nrm(ks[12], (N_C, SGU_GROUPS, SGU_BLOCK, SGU_BLOCK), SGU_BLOCK),
        "sgu_b_s": 1.0 + 0.1 * jax.random.normal(ks[13], (N_C, SGU_GROUPS, SGU_BLOCK), f32),
        "sgu_w_o": nrm(ks[14], (N_C, SGU_WIDTH, D_MODEL), SGU_WIDTH),
        "ffn_w_up": nrm(ks[15], (DEPTH, D_MODEL, D_FF), D_MODEL),
        "ffn_w_down": nrm(ks[16], (DEPTH, D_FF, D_MODEL), D_FF),
    }


def reference(x, norm_mix, norm_ffn, pool_w, pool_scale, attn_w_in, attn_q_gain, attn_k_gain,
              attn_w_o, sgu_w_in, sgu_b_in, sgu_v_gain, sgu_w_s, sgu_b_s, sgu_w_o,
              ffn_w_up, ffn_w_down):
    pos = jnp.arange(x.shape[1])
    for i in range(DEPTH):
        kind, j = i % N_MIXERS, i // N_MIXERS
        h = rmsnorm(x, norm_mix[i])
        if kind == 0:
            y = pool_mixer(h, pool_w[j], pool_scale[j])
        elif kind == 1:
            y = sparse_attention(h, attn_w_in[j], attn_q_gain[j], attn_k_gain[j], attn_w_o[j], pos)
        else:
            y = spatial_gating(h, sgu_w_in[j], sgu_b_in[j], sgu_v_gain[j], sgu_w_s[j], sgu_b_s[j], sgu_w_o[j])
        x = x + y
        h = rmsnorm(x, norm_ffn[i])
        x = x + jnp.square(jax.nn.relu(h @ ffn_w_up[i])) @ ffn_w_down[i]
    return x
```

```python
import functools
import math

import jax
import jax.numpy as jnp
from jax import lax
from jax.experimental import pallas as pl
from jax.experimental.pallas import tpu as pltpu

EPS = 1e-6
CHUNK = 64
POOL_WINDOWS = (2, 4, 8, 16)
POOL_HALO = 16
HEAD_DIM = 128
N_KV_HEADS = 4
IDX_HEADS = 16
IDX_DIM = 64
INDEX_TOPK = 256
Q_BLOCK = 128
ROPE_THETA = 500000.0
ROT_FRACTION = 4
SGU_BLOCK = 128
SGU_GROUPS = 8
LANES = 128
INT_MIN = -(2 ** 31)
MASK_BIAS = -1e30
VMEM_LIMIT_BYTES = 60 * 1024 * 1024

F32 = jnp.float32
BF16 = jnp.bfloat16


def _params(*sem):
    return pltpu.CompilerParams(dimension_semantics=sem, vmem_limit_bytes=VMEM_LIMIT_BYTES)


def _resident(shape):
    nd = len(shape)
    return pl.BlockSpec(shape, lambda *_: (0,) * nd, pipeline_mode=pl.Buffered(1))


def _rms(xf, g):
    ms = jnp.mean(xf * xf, axis=-1, keepdims=True)
    return xf * lax.rsqrt(ms + EPS) * g


def _dot(a, b):
    return jnp.dot(a, b, preferred_element_type=F32)


def _dot_nt(a, b):
    return lax.dot_general(a, b, (((1,), (1,)), ((), ())), preferred_element_type=F32)


def _pool_kernel(x_ref, halo_ref, g_ref, w_ref, scale_ref, o_ref, *, ts):
    i = pl.program_id(1)
    x = x_ref[0]
    g = g_ref[...]
    h = _rms(x, g)
    hh = _rms(halo_ref[0], g)
    hh = jnp.where(i > 0, hh, 0.0)
    hf = jnp.concatenate([hh, h], axis=0)
    t1 = (i * ts + lax.broadcasted_iota(jnp.int32, (ts, 1), 0) + 1).astype(F32)
    cg = x.shape[1] // len(POOL_WINDOWS)
    for gi, w in enumerate(POOL_WINDOWS):
        sl = slice(gi * cg, (gi + 1) * cg)
        s = hf[:, sl]
        k = 1
        while k < w:
            s = s + pltpu.roll(s, k, 0)
            k *= 2
        mean = s[POOL_HALO:] / jnp.minimum(t1, float(w))
        p = (mean - h[:, sl]).astype(BF16)
        y = _dot(p, w_ref[gi]) * scale_ref[:, sl]
        o_ref[0, :, sl] = x[:, sl] + y


def _pool_mixer(x, g, w_bf, scale):
    b, s, d = x.shape
    ts = min(512, s)
    hb = ts // POOL_HALO
    ng = len(POOL_WINDOWS)
    return pl.pallas_call(
        functools.partial(_pool_kernel, ts=ts),
        grid=(b, s // ts),
        in_specs=[
            pl.BlockSpec((1, ts, d), lambda bi, i: (bi, i, 0)),
            pl.BlockSpec((1, POOL_HALO, d), lambda bi, i: (bi, jnp.maximum(i * hb - 1, 0), 0)),
            _resident((1, d)),
            _resident((ng, d // ng, d // ng)),
            _resident((1, d)),
        ],
        out_specs=pl.BlockSpec((1, ts, d), lambda bi, i: (bi, i, 0)),
        out_shape=jax.ShapeDtypeStruct((b, s, d), F32),
        compiler_params=_params("parallel", "parallel"),
        name="pool_mixer",
    )(x, x, g.reshape(1, d), w_bf, scale.reshape(1, d))


def _ffn_kernel(x_ref, g_ref, wu_ref, wd_ref, o_ref, h_ref):
    j = pl.program_id(1)

    @pl.when(j == 0)
    def _():
        x = x_ref[...]
        h_ref[...] = _rms(x, g_ref[...]).astype(BF16)
        o_ref[...] = x

    u = _dot(h_ref[...], wu_ref[...])
    a = jnp.square(jnp.maximum(u, 0.0)).astype(BF16)
    o_ref[...] += _dot(a, wd_ref[...])


def _ffn(x2, g, wu_bf, wd_bf):
    n, d = x2.shape
    f = wu_bf.shape[1]
    tm = min(1024, n)
    tf = 512
    return pl.pallas_call(
        _ffn_kernel,
        grid=(n // tm, f // tf),
        in_specs=[
            pl.BlockSpec((tm, d), lambda i, j: (i, 0)),
            _resident((1, d)),
            pl.BlockSpec((d, tf), lambda i, j: (0, j)),
            pl.BlockSpec((tf, d), lambda i, j: (j, 0)),
        ],
        out_specs=pl.BlockSpec((tm, d), lambda i, j: (i, 0)),
        out_shape=jax.ShapeDtypeStruct((n, d), F32),
        scratch_shapes=[pltpu.VMEM((tm, d), BF16)],
        compiler_params=_params("parallel", "arbitrary"),
        name="ffn",
    )(x2, g.reshape(1, d), wu_bf, wd_bf)


def _proj_res_kernel(x_ref, a_ref, w_ref, o_ref):
    o_ref[...] = x_ref[...] + _dot(a_ref[...], w_ref[...])


def _proj_res(x2, a_bf, w_bf):
    n, d = x2.shape
    kdim = a_bf.shape[1]
    tm = min(512, n)
    return pl.pallas_call(
        _proj_res_kernel,
        grid=(n // tm,),
        in_specs=[
            pl.BlockSpec((tm, d), lambda i: (i, 0)),
            pl.BlockSpec((tm, kdim), lambda i: (i, 0)),
            _resident((kdim, d)),
        ],
        out_specs=pl.BlockSpec((tm, d), lambda i: (i, 0)),
        out_shape=jax.ShapeDtypeStruct((n, d), F32),
        compiler_params=_params("parallel"),
        name="proj_res",
    )(x2, a_bf, w_bf)


def _rope_tables(s, width, rot):
    half = rot // 2
    inv = ROPE_THETA ** (-jnp.arange(half, dtype=F32) / half)
    ang = jnp.arange(s, dtype=F32)[:, None] * inv[None, :]
    cos, sin = jnp.cos(ang), jnp.sin(ang)
    pad = jnp.zeros((s, width - rot), F32)
    zero = jnp.zeros((s, half), F32)
    c = jnp.concatenate([cos, cos, pad + 1.0], axis=1)
    a = jnp.concatenate([-sin, zero, pad], axis=1)
    b = jnp.concatenate([zero, sin, pad], axis=1)
    rep = LANES // width
    return tuple(jnp.tile(t, (1, rep)) for t in (c, a, b))


def _rope(x, c, a, b, half):
    return x * c + pltpu.roll(x, LANES - half, 1) * a + pltpu.roll(x, half, 1) * b


def _q_proj_kernel(x_ref, g_ref, w_ref, gain_ref, c_ref, a_ref, b_ref, q_ref):
    h = _rms(x_ref[...], g_ref[...]).astype(BF16)
    q = _dot(h, w_ref[...])
    gain = gain_ref[...]
    c, a, b = c_ref[...], a_ref[...], b_ref[...]
    half = HEAD_DIM // ROT_FRACTION // 2
    for hd in range(q.shape[1] // HEAD_DIM):
        sl = slice(hd * HEAD_DIM, (hd + 1) * HEAD_DIM)
        qh = _rope(_rms(q[:, sl], gain), c, a, b, half)
        q_ref[:, sl] = (qh * (HEAD_DIM ** -0.5)).astype(BF16)


def _kv_proj_kernel(x_ref, g_ref, w_ref, gain_ref, c_ref, a_ref, b_ref, ci_ref, ai_ref, bi_ref,
                    k_ref, v_ref, qi_ref, ki_ref, wi_ref):
    h = _rms(x_ref[...], g_ref[...]).astype(BF16)
    y = _dot(h, w_ref[...])
    gain = gain_ref[...]
    c, a, b = c_ref[...], a_ref[...], b_ref[...]
    ci, ai, bi = ci_ref[...], ai_ref[...], bi_ref[...]
    half = HEAD_DIM // ROT_FRACTION // 2
    halfi = IDX_DIM // ROT_FRACTION // 2
    nkv = N_KV_HEADS * HEAD_DIM
    for hd in range(N_KV_HEADS):
        sl = slice(hd * HEAD_DIM, (hd + 1) * HEAD_DIM)
        k_ref[:, sl] = _rope(_rms(y[:, sl], gain), c, a, b, half).astype(BF16)
    v_ref[...] = y[:, nkv:2 * nkv].astype(BF16)
    nqi = IDX_HEADS * IDX_DIM
    for blk in range(nqi // LANES):
        sl = slice(2 * nkv + blk * LANES, 2 * nkv + (blk + 1) * LANES)
        qi_ref[:, blk * LANES:(blk + 1) * LANES] = _rope(y[:, sl], ci, ai, bi, halfi).astype(BF16)
    kw = y[:, 2 * nkv + nqi:]
    ki_ref[...] = _rope(kw, ci, ai, bi, halfi)[:, :IDX_DIM].astype(BF16)
    wi_ref[...] = pltpu.roll(kw, LANES - IDX_DIM, 1) * (IDX_HEADS ** -0.5 * IDX_DIM ** -0.5)


def _attn_kernel(q_ref, qi_ref, wi_ref, k_ref, v_ref, ki_ref, o_ref, key_ref, bias_ref, wb_ref,
                 *, tk, n_sel):
    t0 = pl.program_id(1) * Q_BLOCK
    nk = (t0 + Q_BLOCK + tk - 1) // tk
    nsub = tk // LANES
    q_chunk = (t0 + lax.broadcasted_iota(jnp.int32, (Q_BLOCK, 1), 0)) // CHUNK

    qi = qi_ref[0]
    wi = wi_ref[0]
    for hd in range(IDX_HEADS):
        wb_ref[hd] = jnp.broadcast_to(wi[:, hd:hd + 1], (Q_BLOCK, LANES))
    qis = [qi[:, hd * IDX_DIM:(hd + 1) * IDX_DIM] for hd in range(IDX_HEADS)]

    def score_tile(kt, carry):
        off = pl.multiple_of(kt * tk, tk)
        ki_t = ki_ref[0, pl.ds(off, tk), :]
        acc = jnp.zeros((Q_BLOCK, tk), F32)
        for hd in range(IDX_HEADS):
            d = jnp.maximum(_dot_nt(qis[hd], ki_t), 0.0)
            wb = wb_ref[hd]
            acc = acc + d * jnp.concatenate([wb] * nsub, axis=1)
        key_chunk = (off + lax.broadcasted_iota(jnp.int32, (1, tk), 1)) // CHUNK
        bits = pltpu.bitcast(acc, jnp.int32)
        skey = jnp.where(bits < 0, bits ^ jnp.int32(0x7FFFFFFF), bits)
        key_ref[:, pl.ds(off, tk)] = jnp.where(key_chunk <= q_chunk, skey, jnp.int32(INT_MIN))
        return carry

    lax.fori_loop(0, nk, score_tile, 0)

    def count_ge(thr):
        def body(kt, c):
            off = pl.multiple_of(kt * tk, tk)
            m = (key_ref[:, pl.ds(off, tk)] >= thr).astype(jnp.int32)
            for sb in range(nsub):
                c = c + m[:, sb * LANES:(sb + 1) * LANES]
            return c

        c = lax.fori_loop(0, nk, body, jnp.zeros((Q_BLOCK, LANES), jnp.int32))
        return jnp.sum(c, axis=1, keepdims=True)

    def bit_body(bi, thr):
        cand = thr + lax.shift_left(jnp.int32(1), 31 - bi)
        return jnp.where(count_ge(cand) >= n_sel, cand, thr)

    thr = lax.fori_loop(0, 32, bit_body, jnp.full((Q_BLOCK, 1), INT_MIN, jnp.int32))
    thr = jnp.maximum(thr, jnp.int32(INT_MIN + 1))

    def bias_tile(kt, carry):
        off = pl.multiple_of(kt * tk, tk)
        sel = key_ref[:, pl.ds(off, tk)] >= thr
        bias_ref[:, pl.ds(off, tk)] = jnp.where(sel, 0.0, MASK_BIAS).astype(F32)
        return carry

    lax.fori_loop(0, nk, bias_tile, 0)

    q = q_ref[0]
    rep = q.shape[1] // HEAD_DIM // N_KV_HEADS
    rows = rep * Q_BLOCK
    for g in range(N_KV_HEADS):
        gsl = slice(g * HEAD_DIM, (g + 1) * HEAD_DIM)
        qs = jnp.concatenate(
            [q[:, (g * rep + r) * HEAD_DIM:(g * rep + r + 1) * HEAD_DIM] for r in range(rep)], axis=0)

        def body(kt, carry, gsl=gsl, qs=qs):
            m, l, acc = carry
            off = pl.multiple_of(kt * tk, tk)
            k_t = k_ref[0, pl.ds(off, tk), gsl]
            v_t = v_ref[0, pl.ds(off, tk), gsl]
            bias = bias_ref[:, pl.ds(off, tk)]
            s = _dot_nt(qs, k_t) + jnp.concatenate([bias] * rep, axis=0)
            m_new = jnp.maximum(m, jnp.max(s, axis=1, keepdims=True))
            alpha = jnp.exp(m - m_new)
            p = jnp.exp(s - m_new)
            l = alpha * l + jnp.sum(p, axis=1, keepdims=True)
            acc = alpha * acc + _dot(p.astype(BF16), v_t)
            return m_new, l, acc

        init = (jnp.full((rows, 1), MASK_BIAS, F32), jnp.zeros((rows, 1), F32),
                jnp.zeros((rows, HEAD_DIM), F32))
        _, l, acc = lax.fori_loop(0, nk, body, init)
        o = acc / l
        for r in range(rep):
            hsl = slice((g * rep + r) * HEAD_DIM, (g * rep + r + 1) * HEAD_DIM)
            o_ref[0, :, hsl] = o[r * Q_BLOCK:(r + 1) * Q_BLOCK].astype(BF16)


def _sparse_attention(x, g, w_in, q_gain, k_gain, w_o):
    b, s, d = x.shape
    n = b * s
    x2 = x.reshape(n, d)
    nq = d
    nkv = N_KV_HEADS * HEAD_DIM
    nqi = IDX_HEADS * IDX_DIM
    w_q = w_in[:, :nq].astype(BF16)
    pad = LANES - IDX_DIM - IDX_HEADS
    w_rest = jnp.pad(w_in[:, nq:], ((0, 0), (0, pad))).astype(BF16)
    nrest = w_rest.shape[1]
    c, a, bt = _rope_tables(s, HEAD_DIM, HEAD_DIM // ROT_FRACTION)
    ci, ai, bi = _rope_tables(s, IDX_DIM, IDX_DIM // ROT_FRACTION)

    tm = min(512, s)
    nt = s // tm
    row = lambda i: (i, 0)
    pos = lambda i: (i % nt, 0)
    tab = pl.BlockSpec((tm, LANES), pos)
    q = pl.pallas_call(
        _q_proj_kernel,
        grid=(n // tm,),
        in_specs=[pl.BlockSpec((tm, d), row), _resident((1, d)), _resident((d, nq)),
                  _resident((1, HEAD_DIM)), tab, tab, tab],
        out_specs=pl.BlockSpec((tm, nq), row),
        out_shape=jax.ShapeDtypeStruct((n, nq), BF16),
        compiler_params=_params("parallel"),
        name="attn_q_proj",
    )(x2, g.reshape(1, d), w_q, q_gain.reshape(1, HEAD_DIM), c, a, bt)

    k, v, qi, ki, wi = pl.pallas_call(
        _kv_proj_kernel,
        grid=(n // tm,),
        in_specs=[pl.BlockSpec((tm, d), row), _resident((1, d)), _resident((d, nrest)),
                  _resident((1, HEAD_DIM)), tab, tab, tab, tab, tab, tab],
        out_specs=[pl.BlockSpec((tm, nkv), row), pl.BlockSpec((tm, nkv), row),
                   pl.BlockSpec((tm, nqi), row), pl.BlockSpec((tm, IDX_DIM), row),
                   pl.BlockSpec((tm, LANES), row)],
        out_shape=[jax.ShapeDtypeStruct((n, nkv), BF16), jax.ShapeDtypeStruct((n, nkv), BF16),
                   jax.ShapeDtypeStruct((n, nqi), BF16), jax.ShapeDtypeStruct((n, IDX_DIM), BF16),
                   jax.ShapeDtypeStruct((n, LANES), F32)],
        compiler_params=_params("parallel"),
        name="attn_kv_proj",
    )(x2, g.reshape(1, d), w_rest, k_gain.reshape(1, HEAD_DIM), c, a, bt, ci, ai, bi)

    n_sel = min(INDEX_TOPK, s // 4)
    tk = min(512, s)
    blk = lambda bi_, qb: (bi_, qb, 0)
    full = lambda bi_, qb: (bi_, 0, 0)
    o = pl.pallas_call(
        functools.partial(_attn_kernel, tk=tk, n_sel=n_sel),
        grid=(b, s // Q_BLOCK),
        in_specs=[pl.BlockSpec((1, Q_BLOCK, nq), blk), pl.BlockSpec((1, Q_BLOCK, nqi), blk),
                  pl.BlockSpec((1, Q_BLOCK, LANES), blk), pl.BlockSpec((1, s, nkv), full),
                  pl.BlockSpec((1, s, nkv), full), pl.BlockSpec((1, s, IDX_DIM), full)],
        out_specs=pl.BlockSpec((1, Q_BLOCK, nq), blk),
        out_shape=jax.ShapeDtypeStruct((b, s, nq), BF16),
        scratch_shapes=[pltpu.VMEM((Q_BLOCK, s), jnp.int32), pltpu.VMEM((Q_BLOCK, s), F32),
                        pltpu.VMEM((IDX_HEADS, Q_BLOCK, LANES), F32)],
        compiler_params=_params("parallel", "arbitrary"),
        name="sparse_attn",
    )(q.reshape(b, s, nq), qi.reshape(b, s, nqi), wi.reshape(b, s, LANES),
      k.reshape(b, s, nkv), v.reshape(b, s, nkv), ki.reshape(b, s, IDX_DIM))
    return _proj_res(x2, o.reshape(n, nq), w_o.astype(BF16)).reshape(b, s, d)


def _sgu_kernel(x_ref, g_ref, w_ref, b_ref, vg_ref, ws_ref, bs_ref, o_ref, *, tm):
    h = _rms(x_ref[...], g_ref[...]).astype(BF16)
    z = _dot(h, w_ref[...]) + b_ref[...]
    z = 0.5 * z * (1.0 + lax.erf(z * (2.0 ** -0.5)))
    width = z.shape[1] // 2
    u = z[:, :width]
    v = _rms(z[:, width:], vg_ref[...]).astype(BF16)
    gd = width // SGU_GROUPS
    ii = lax.broadcasted_iota(jnp.int32, (SGU_BLOCK, SGU_BLOCK), 0) // CHUNK
    jj = lax.broadcasted_iota(jnp.int32, (SGU_BLOCK, SGU_BLOCK), 1) // CHUNK
    causal = jj <= ii
    bs = bs_ref[...]
    for gi in range(SGU_GROUPS):
        ws = jnp.where(causal, ws_ref[gi], 0.0).astype(BF16)
        bias = bs[:, gi:gi + 1]
        for nb in range(tm // SGU_BLOCK):
            rs = slice(nb * SGU_BLOCK, (nb + 1) * SGU_BLOCK)
            cs = slice(gi * gd, (gi + 1) * gd)
            mixed = _dot(ws, v[rs, cs]) + bias
            o_ref[rs, cs] = (u[rs, cs] * mixed).astype(BF16)


def _spatial_gating(x, g, w_in, b_in, v_gain, w_s, b_s, w_o):
    b, s, d = x.shape
    n = b * s
    x2 = x.reshape(n, d)
    width = w_in.shape[1] // 2
    tm = min(256, s)
    row = lambda i: (i, 0)
    gated = pl.pallas_call(
        functools.partial(_sgu_kernel, tm=tm),
        grid=(n // tm,),
        in_specs=[pl.BlockSpec((tm, d), row), _resident((1, d)), _resident((d, 2 * width)),
                  _resident((1, 2 * width)), _resident((1, width)),
                  _resident((SGU_GROUPS, SGU_BLOCK, SGU_BLOCK)), _resident((SGU_BLOCK, SGU_GROUPS))],
        out_specs=pl.BlockSpec((tm, width), row),
        out_shape=jax.ShapeDtypeStruct((n, width), BF16),
        compiler_params=_params("parallel"),
        name="sgu_gate",
    )(x2, g.reshape(1, d), w_in.astype(BF16), b_in.reshape(1, 2 * width), v_gain.reshape(1, width),
      w_s, b_s.T)
    return _proj_res(x2, gated, w_o.astype(BF16)).reshape(b, s, d)


def kernel(x, norm_mix, norm_ffn, pool_w, pool_scale, attn_w_in, attn_q_gain, attn_k_gain, attn_w_o,
           sgu_w_in, sgu_b_in, sgu_v_gain, sgu_w_s, sgu_b_s, sgu_w_o, ffn_w_up, ffn_w_down):
    b, s, d = x.shape
    depth = norm_mix.shape[0]
    for i in range(depth):
        kind, j = i % 3, i // 3
        if kind == 0:
            x = _pool_mixer(x, norm_mix[i], pool_w[j].astype(BF16), pool_scale[j])
        elif kind == 1:
            x = _sparse_attention(x, norm_mix[i], attn_w_in[j], attn_q_gain[j], attn_k_gain[j],
                                  attn_w_o[j])
        else:
            x = _spatial_gating(x, norm_mix[i], sgu_w_in[j], sgu_b_in[j], sgu_v_gain[j], sgu_w_s[j],
                                sgu_b_s[j], sgu_w_o[j])
        x = _ffn(x.reshape(b * s, d), norm_ffn[i], ffn_w_up[i].astype(BF16),
                 ffn_w_down[i].astype(BF16)).reshape(b, s, d)
    return x
```

```python
import functools

import jax
import jax.numpy as jnp
from jax import lax
from jax.experimental import pallas as pl
from jax.experimental.pallas import tpu as pltpu

EPS = 1e-6
CHUNK = 64
POOL_WINDOWS = (2, 4, 8, 16)
POOL_HALO = 16
HEAD_DIM = 128
N_KV_HEADS = 4
IDX_HEADS = 16
IDX_DIM = 64
INDEX_TOPK = 256
Q_BLOCK = 128
ROPE_THETA = 500000.0
ROT_FRACTION = 4
SGU_BLOCK = 128
SGU_GROUPS = 8
LANES = 128
INT_MIN = -(2 ** 31)
COUNT_ROWS = 64
ONES_ROWS = 16
LOG2E = 1.4426950408889634
MASK_BIAS = -1e30
VMEM_LIMIT_BYTES = 60 * 1024 * 1024

F32 = jnp.float32
BF16 = jnp.bfloat16


def _params(*sem):
    return pltpu.CompilerParams(dimension_semantics=sem, vmem_limit_bytes=VMEM_LIMIT_BYTES)


def _resident(shape):
    nd = len(shape)
    return pl.BlockSpec(shape, lambda *_: (0,) * nd, pipeline_mode=pl.Buffered(1))


def _rms(xf, g):
    ms = jnp.mean(xf * xf, axis=-1, keepdims=True)
    return xf * lax.rsqrt(ms + EPS) * g


def _dot(a, b):
    return jnp.dot(a, b, preferred_element_type=F32)


def _pool_kernel(x_ref, halo_ref, g_ref, w_ref, scale_ref, o_ref, *, ts):
    i = pl.program_id(1)
    x = x_ref[0]
    g = g_ref[...]
    h = _rms(x, g)
    hh = _rms(halo_ref[0], g)
    hh = jnp.where(i > 0, hh, 0.0)
    hf = jnp.concatenate([hh, h], axis=0)
    t1 = (i * ts + lax.broadcasted_iota(jnp.int32, (ts, 1), 0) + 1).astype(F32)
    cg = x.shape[1] // len(POOL_WINDOWS)
    for gi, w in enumerate(POOL_WINDOWS):
        sl = slice(gi * cg, (gi + 1) * cg)
        s = hf[:, sl]
        k = 1
        while k < w:
            s = s + pltpu.roll(s, k, 0)
            k *= 2
        mean = s[POOL_HALO:] / jnp.minimum(t1, float(w))
        p = (mean - h[:, sl]).astype(BF16)
        y = _dot(p, w_ref[gi]) * scale_ref[:, sl]
        o_ref[0, :, sl] = x[:, sl] + y


def _pool_mixer(x, g, w_bf, scale):
    b, s, d = x.shape
    ts = min(512, s)
    hb = ts // POOL_HALO
    ng = len(POOL_WINDOWS)
    return pl.pallas_call(
        functools.partial(_pool_kernel, ts=ts),
        grid=(b, s // ts),
        in_specs=[
            pl.BlockSpec((1, ts, d), lambda bi, i: (bi, i, 0)),
            pl.BlockSpec((1, POOL_HALO, d), lambda bi, i: (bi, jnp.maximum(i * hb - 1, 0), 0)),
            _resident((1, d)),
            _resident((ng, d // ng, d // ng)),
            _resident((1, d)),
        ],
        out_specs=pl.BlockSpec((1, ts, d), lambda bi, i: (bi, i, 0)),
        out_shape=jax.ShapeDtypeStruct((b, s, d), F32),
        compiler_params=_params("parallel", "parallel"),
        name="pool_mixer",
    )(x, x, g.reshape(1, d), w_bf, scale.reshape(1, d))


def _ffn_kernel(x_ref, g_ref, wu_ref, wd_ref, o_ref, h_ref):
    j = pl.program_id(1)

    @pl.when(j == 0)
    def _():
        x = x_ref[...]
        h_ref[...] = _rms(x, g_ref[...]).astype(BF16)
        o_ref[...] = x

    u = _dot(h_ref[...], wu_ref[...])
    a = jnp.square(jnp.maximum(u, 0.0)).astype(BF16)
    o_ref[...] += _dot(a, wd_ref[...])


def _ffn(x2, g, wu_bf, wd_bf):
    n, d = x2.shape
    f = wu_bf.shape[1]
    tm = min(1024, n)
    tf = 512
    return pl.pallas_call(
        _ffn_kernel,
        grid=(n // tm, f // tf),
        in_specs=[
            pl.BlockSpec((tm, d), lambda i, j: (i, 0)),
            _resident((1, d)),
            pl.BlockSpec((d, tf), lambda i, j: (0, j)),
            pl.BlockSpec((tf, d), lambda i, j: (j, 0)),
        ],
        out_specs=pl.BlockSpec((tm, d), lambda i, j: (i, 0)),
        out_shape=jax.ShapeDtypeStruct((n, d), F32),
        scratch_shapes=[pltpu.VMEM((tm, d), BF16)],
        compiler_params=_params("parallel", "arbitrary"),
        name="ffn",
    )(x2, g.reshape(1, d), wu_bf, wd_bf)


def _proj_res_kernel(x_ref, a_ref, w_ref, o_ref):
    o_ref[...] = x_ref[...] + _dot(a_ref[...], w_ref[...])


def _proj_res(x2, a_bf, w_bf):
    n, d = x2.shape
    kdim = a_bf.shape[1]
    tm = min(512, n)
    return pl.pallas_call(
        _proj_res_kernel,
        grid=(n // tm,),
        in_specs=[
            pl.BlockSpec((tm, d), lambda i: (i, 0)),
            pl.BlockSpec((tm, kdim), lambda i: (i, 0)),
            _resident((kdim, d)),
        ],
        out_specs=pl.BlockSpec((tm, d), lambda i: (i, 0)),
        out_shape=jax.ShapeDtypeStruct((n, d), F32),
        compiler_params=_params("parallel"),
        name="proj_res",
    )(x2, a_bf, w_bf)


def _rope_tables(s, width, rot):
    half = rot // 2
    inv = ROPE_THETA ** (-jnp.arange(half, dtype=F32) / half)
    ang = jnp.arange(s, dtype=F32)[:, None] * inv[None, :]
    cos, sin = jnp.cos(ang), jnp.sin(ang)
    pad = jnp.zeros((s, width - rot), F32)
    zero = jnp.zeros((s, half), F32)
    c = jnp.concatenate([cos, cos, pad + 1.0], axis=1)
    a = jnp.concatenate([-sin, zero, pad], axis=1)
    b = jnp.concatenate([zero, sin, pad], axis=1)
    rep = LANES // width
    return tuple(jnp.tile(t, (1, rep)) for t in (c, a, b))


def _rope(x, c, a, b, half):
    return x * c + pltpu.roll(x, LANES - half, 1) * a + pltpu.roll(x, half, 1) * b


def _q_proj_kernel(x_ref, g_ref, w_ref, gain_ref, c_ref, a_ref, b_ref, qt_ref):
    h = _rms(x_ref[...], g_ref[...]).astype(BF16)
    q = _dot(h, w_ref[...])
    gain = gain_ref[...]
    c, a, b = c_ref[...], a_ref[...], b_ref[...]
    half = HEAD_DIM // ROT_FRACTION // 2
    for hd in range(q.shape[1] // HEAD_DIM):
        sl = slice(hd * HEAD_DIM, (hd + 1) * HEAD_DIM)
        qh = _rope(_rms(q[:, sl], gain), c, a, b, half) * (HEAD_DIM ** -0.5 * LOG2E)
        qt_ref[0, sl, :] = qh.T.astype(BF16)


def _kv_proj_kernel(x_ref, g_ref, w_ref, gain_ref, c_ref, a_ref, b_ref, ci_ref, ai_ref, bi_ref,
                    k_ref, vt_ref, qit_ref, ki_ref, wit_ref):
    h = _rms(x_ref[...], g_ref[...]).astype(BF16)
    y = _dot(h, w_ref[...])
    gain = gain_ref[...]
    c, a, b = c_ref[...], a_ref[...], b_ref[...]
    ci, ai, bi = ci_ref[...], ai_ref[...], bi_ref[...]
    half = HEAD_DIM // ROT_FRACTION // 2
    halfi = IDX_DIM // ROT_FRACTION // 2
    nkv = N_KV_HEADS * HEAD_DIM
    for hd in range(N_KV_HEADS):
        sl = slice(hd * HEAD_DIM, (hd + 1) * HEAD_DIM)
        k_ref[:, sl] = _rope(_rms(y[:, sl], gain), c, a, b, half).astype(BF16)
        vt_ref[0, sl, :] = y[:, nkv + hd * HEAD_DIM:nkv + (hd + 1) * HEAD_DIM].T.astype(BF16)
    nqi = IDX_HEADS * IDX_DIM
    for blk in range(nqi // LANES):
        sl = slice(2 * nkv + blk * LANES, 2 * nkv + (blk + 1) * LANES)
        qit_ref[0, blk * LANES:(blk + 1) * LANES, :] = _rope(y[:, sl], ci, ai, bi, halfi).T.astype(BF16)
    kw = y[:, 2 * nkv + nqi:]
    ki_ref[...] = _rope(kw, ci, ai, bi, halfi)[:, :IDX_DIM].astype(BF16)
    wi = pltpu.roll(kw, LANES - IDX_DIM, 1) * (IDX_HEADS ** -0.5 * IDX_DIM ** -0.5)
    wit_ref[0] = wi.T[:IDX_HEADS]


def _key_to_f32(key):
    bits = jnp.where(key < 0, key ^ jnp.int32(0x7FFFFFFF), key)
    f = pltpu.bitcast(bits, F32)
    return jnp.where(f != f, jnp.inf, f)


def _attn_kernel(qt_ref, qit_ref, wit_ref, k_ref, vt_ref, ki_ref, o_ref, sc_ref, bias_ref,
                 *, tk, n_sel):
    t0 = pl.program_id(1) * Q_BLOCK
    nk = (t0 + Q_BLOCK + tk - 1) // tk
    q_chunk = (t0 + lax.broadcasted_iota(jnp.int32, (1, Q_BLOCK), 1)) // CHUNK

    qit = qit_ref[0]
    wit = wit_ref[0]
    npair = IDX_HEADS // 2
    rhs = [jnp.concatenate([qit[(2 * p) * IDX_DIM:(2 * p + 1) * IDX_DIM],
                            qit[(2 * p + 1) * IDX_DIM:(2 * p + 2) * IDX_DIM]], axis=1)
           for p in range(npair)]

    def score_tile(kt, carry):
        off = pl.multiple_of(kt * tk, tk)
        ki_t = ki_ref[0, pl.ds(off, tk), :]
        acc = jnp.zeros((tk, Q_BLOCK), F32)
        for p in range(npair):
            d = jnp.maximum(_dot(ki_t, rhs[p]), 0.0)
            acc = acc + d[:, :Q_BLOCK] * wit[2 * p:2 * p + 1, :]
            acc = acc + d[:, Q_BLOCK:] * wit[2 * p + 1:2 * p + 2, :]
        key_chunk = (off + lax.broadcasted_iota(jnp.int32, (tk, 1), 0)) // CHUNK
        sc_ref[pl.ds(off, tk), :] = jnp.where(key_chunk <= q_chunk, acc, -jnp.inf)
        return carry

    lax.fori_loop(0, nk, score_tile, 0)

    def count_ge(thr):
        def body(kt, c):
            off = pl.multiple_of(kt * tk, tk)
            m = jnp.where(sc_ref[pl.ds(off, tk), :] >= thr, 1.0, 0.0)
            return c + jnp.sum(m.reshape(tk // COUNT_ROWS, COUNT_ROWS, Q_BLOCK), axis=0)

        c = lax.fori_loop(0, nk, body, jnp.zeros((COUNT_ROWS, Q_BLOCK), F32))
        return jnp.sum(c, axis=0, keepdims=True)

    def bit_body(bi, key):
        cand = key + lax.shift_left(jnp.int32(1), 31 - bi)
        return jnp.where(count_ge(_key_to_f32(cand)) >= n_sel, cand, key)

    key = lax.fori_loop(0, 32, bit_body, jnp.full((1, Q_BLOCK), INT_MIN, jnp.int32))
    thr = jnp.where(key == INT_MIN, jnp.finfo(F32).min, _key_to_f32(key))

    def bias_tile(kt, carry):
        off = pl.multiple_of(kt * tk, tk)
        sel = sc_ref[pl.ds(off, tk), :] >= thr
        bias_ref[pl.ds(off, tk), :] = jnp.where(sel, 0.0, MASK_BIAS).astype(BF16)
        return carry

    lax.fori_loop(0, nk, bias_tile, 0)

    qt = qt_ref[0]
    rep = qt.shape[0] // HEAD_DIM // N_KV_HEADS
    cols = rep * Q_BLOCK
    eye = (lax.broadcasted_iota(jnp.int32, (Q_BLOCK, Q_BLOCK), 0)
           == lax.broadcasted_iota(jnp.int32, (Q_BLOCK, Q_BLOCK), 1))
    eye = jnp.where(eye, 1.0, 0.0).astype(BF16)
    ones = jnp.ones((ONES_ROWS, tk), BF16)
    gsls = [slice(g * HEAD_DIM, (g + 1) * HEAD_DIM) for g in range(N_KV_HEADS)]
    qaugs = []
    for g in range(N_KV_HEADS):
        qg = jnp.concatenate(
            [qt[(g * rep + r) * HEAD_DIM:(g * rep + r + 1) * HEAD_DIM] for r in range(rep)], axis=1)
        qaugs.append(jnp.concatenate([qg, jnp.concatenate([eye] * rep, axis=1)], axis=0))

    def logits(kt):
        off = pl.multiple_of(kt * tk, tk)
        bias_t = bias_ref[pl.ds(off, tk), :]
        return tuple(
            _dot(jnp.concatenate([k_ref[0, pl.ds(off, tk), gsls[g]], bias_t], axis=1), qaugs[g])
            for g in range(N_KV_HEADS))

    def body(kt, carry):
        s_cur, state = carry
        s_next = logits(jnp.minimum(kt + 1, nk - 1))
        off = pl.multiple_of(kt * tk, tk)
        out = []
        for g in range(N_KV_HEADS):
            m, acc = state[g]
            s = s_cur[g]
            vaug = jnp.concatenate([vt_ref[0, gsls[g], pl.ds(off, tk)], ones], axis=0)
            m_new = jnp.maximum(m, jnp.max(s, axis=0, keepdims=True))
            p = jnp.exp2(s - m_new).astype(BF16)
            acc = jnp.exp2(m - m_new) * acc + _dot(vaug, p)
            out.append((m_new, acc))
        return s_next, tuple(out)

    init = tuple((jnp.full((1, cols), MASK_BIAS, F32), jnp.zeros((HEAD_DIM + ONES_ROWS, cols), F32))
                 for _ in range(N_KV_HEADS))
    _, res = lax.fori_loop(0, nk, body, (logits(0), init))
    for g in range(N_KV_HEADS):
        acc = res[g][1]
        o = acc[:HEAD_DIM] / acc[HEAD_DIM:HEAD_DIM + 1]
        for r in range(rep):
            hsl = slice((g * rep + r) * HEAD_DIM, (g * rep + r + 1) * HEAD_DIM)
            o_ref[0, :, hsl] = o[:, r * Q_BLOCK:(r + 1) * Q_BLOCK].T.astype(BF16)


def _sparse_attention(x, g, w_in, q_gain, k_gain, w_o):
    b, s, d = x.shape
    n = b * s
    x2 = x.reshape(n, d)
    nq = d
    nkv = N_KV_HEADS * HEAD_DIM
    nqi = IDX_HEADS * IDX_DIM
    w_q = w_in[:, :nq].astype(BF16)
    pad = LANES - IDX_DIM - IDX_HEADS
    w_rest = jnp.pad(w_in[:, nq:], ((0, 0), (0, pad))).astype(BF16)
    nrest = w_rest.shape[1]
    c, a, bt = _rope_tables(s, HEAD_DIM, HEAD_DIM // ROT_FRACTION)
    ci, ai, bi = _rope_tables(s, IDX_DIM, IDX_DIM // ROT_FRACTION)

    tm = min(512, s)
    nt = s // tm
    row = lambda i: (i, 0)
    pos = lambda i: (i % nt, 0)
    tcol = lambda i: (i // nt, 0, i % nt)
    tab = pl.BlockSpec((tm, LANES), pos)
    qt = pl.pallas_call(
        _q_proj_kernel,
        grid=(n // tm,),
        in_specs=[pl.BlockSpec((tm, d), row), _resident((1, d)), _resident((d, nq)),
                  _resident((1, HEAD_DIM)), tab, tab, tab],
        out_specs=pl.BlockSpec((1, nq, tm), tcol),
        out_shape=jax.ShapeDtypeStruct((b, nq, s), BF16),
        compiler_params=_params("parallel"),
        name="attn_q_proj",
    )(x2, g.reshape(1, d), w_q, q_gain.reshape(1, HEAD_DIM), c, a, bt)

    k, vt, qit, ki, wit = pl.pallas_call(
        _kv_proj_kernel,
        grid=(n // tm,),
        in_specs=[pl.BlockSpec((tm, d), row), _resident((1, d)), _resident((d, nrest)),
                  _resident((1, HEAD_DIM)), tab, tab, tab, tab, tab, tab],
        out_specs=[pl.BlockSpec((tm, nkv), row), pl.BlockSpec((1, nkv, tm), tcol),
                   pl.BlockSpec((1, nqi, tm), tcol), pl.BlockSpec((tm, IDX_DIM), row),
                   pl.BlockSpec((1, IDX_HEADS, tm), tcol)],
        out_shape=[jax.ShapeDtypeStruct((n, nkv), BF16), jax.ShapeDtypeStruct((b, nkv, s), BF16),
                   jax.ShapeDtypeStruct((b, nqi, s), BF16), jax.ShapeDtypeStruct((n, IDX_DIM), BF16),
                   jax.ShapeDtypeStruct((b, IDX_HEADS, s), F32)],
        compiler_params=_params("parallel"),
        name="attn_kv_proj",
    )(x2, g.reshape(1, d), w_rest, k_gain.reshape(1, HEAD_DIM), c, a, bt, ci, ai, bi)

    n_sel = min(INDEX_TOPK, s // 4)
    tk = min(512, s)
    qcol = lambda bi_, qb: (bi_, 0, qb)
    full = lambda bi_, qb: (bi_, 0, 0)
    o = pl.pallas_call(
        functools.partial(_attn_kernel, tk=tk, n_sel=n_sel),
        grid=(b, s // Q_BLOCK),
        in_specs=[pl.BlockSpec((1, nq, Q_BLOCK), qcol), pl.BlockSpec((1, nqi, Q_BLOCK), qcol),
                  pl.BlockSpec((1, IDX_HEADS, Q_BLOCK), qcol), pl.BlockSpec((1, s, nkv), full),
                  pl.BlockSpec((1, nkv, s), full), pl.BlockSpec((1, s, IDX_DIM), full)],
        out_specs=pl.BlockSpec((1, Q_BLOCK, nq), lambda bi_, qb: (bi_, qb, 0)),
        out_shape=jax.ShapeDtypeStruct((b, s, nq), BF16),
        scratch_shapes=[pltpu.VMEM((s, Q_BLOCK), F32), pltpu.VMEM((s, Q_BLOCK), BF16)],
        compiler_params=_params("parallel", "arbitrary"),
        name="sparse_attn",
    )(qt, qit, wit, k.reshape(b, s, nkv), vt, ki.reshape(b, s, IDX_DIM))
    return _proj_res(x2, o.reshape(n, nq), w_o.astype(BF16)).reshape(b, s, d)


def _sgu_kernel(x_ref, g_ref, w_ref, b_ref, vg_ref, ws_ref, bs_ref, o_ref, *, tm):
    h = _rms(x_ref[...], g_ref[...]).astype(BF16)
    z = _dot(h, w_ref[...]) + b_ref[...]
    z = 0.5 * z * (1.0 + lax.erf(z * (2.0 ** -0.5)))
    width = z.shape[1] // 2
    u = z[:, :width]
    v = _rms(z[:, width:], vg_ref[...]).astype(BF16)
    gd = width // SGU_GROUPS
    ii = lax.broadcasted_iota(jnp.int32, (SGU_BLOCK, SGU_BLOCK), 0) // CHUNK
    jj = lax.broadcasted_iota(jnp.int32, (SGU_BLOCK, SGU_BLOCK), 1) // CHUNK
    causal = jj <= ii
    bs = bs_ref[...]
    for gi in range(SGU_GROUPS):
        ws = jnp.where(causal, ws_ref[gi], 0.0).astype(BF16)
        bias = bs[:, gi:gi + 1]
        for nb in range(tm // SGU_BLOCK):
            rs = slice(nb * SGU_BLOCK, (nb + 1) * SGU_BLOCK)
            cs = slice(gi * gd, (gi + 1) * gd)
            mixed = _dot(ws, v[rs, cs]) + bias
            o_ref[rs, cs] = (u[rs, cs] * mixed).astype(BF16)


def _spatial_gating(x, g, w_in, b_in, v_gain, w_s, b_s, w_o):
    b, s, d = x.shape
    n = b * s
    x2 = x.reshape(n, d)
    width = w_in.shape[1] // 2
    tm = min(256, s)
    row = lambda i: (i, 0)
    gated = pl.pallas_call(
        functools.partial(_sgu_kernel, tm=tm),
        grid=(n // tm,),
        in_specs=[pl.BlockSpec((tm, d), row), _resident((1, d)), _resident((d, 2 * width)),
                  _resident((1, 2 * width)), _resident((1, width)),
                  _resident((SGU_GROUPS, SGU_BLOCK, SGU_BLOCK)), _resident((SGU_BLOCK, SGU_GROUPS))],
        out_specs=pl.BlockSpec((tm, width), row),
        out_shape=jax.ShapeDtypeStruct((n, width), BF16),
        compiler_params=_params("parallel"),
        name="sgu_gate",
    )(x2, g.reshape(1, d), w_in.astype(BF16), b_in.reshape(1, 2 * width), v_gain.reshape(1, width),
      w_s, b_s.T)
    return _proj_res(x2, gated, w_o.astype(BF16)).reshape(b, s, d)


def kernel(x, norm_mix, norm_ffn, pool_w, pool_scale, attn_w_in, attn_q_gain, attn_k_gain, attn_w_o,
           sgu_w_in, sgu_b_in, sgu_v_gain, sgu_w_s, sgu_b_s, sgu_w_o, ffn_w_up, ffn_w_down):
    b, s, d = x.shape
    depth = norm_mix.shape[0]
    for i in range(depth):
        kind, j = i % 3, i // 3
        if kind == 0:
            x = _pool_mixer(x, norm_mix[i], pool_w[j].astype(BF16), pool_scale[j])
        elif kind == 1:
            x = _sparse_attention(x, norm_mix[i], attn_w_in[j], attn_q_gain[j], attn_k_gain[j],
                                  attn_w_o[j])
        else:
            x = _spatial_gating(x, norm_mix[i], sgu_w_in[j], sgu_b_in[j], sgu_v_gain[j], sgu_w_s[j],
                                sgu_b_s[j], sgu_w_o[j])
        x = _ffn(x.reshape(b * s, d), norm_ffn[i], ffn_w_up[i].astype(BF16),
                 ffn_w_down[i].astype(BF16)).reshape(b, s, d)
    return x
```

```python
import functools

import jax
import jax.numpy as jnp
from jax import lax
from jax.experimental import pallas as pl
from jax.experimental.pallas import tpu as pltpu

EPS = 1e-6
CHUNK = 64
POOL_WINDOWS = (2, 4, 8, 16)
POOL_HALO = 16
HEAD_DIM = 128
N_KV_HEADS = 4
IDX_HEADS = 16
IDX_DIM = 64
INDEX_TOPK = 256
Q_BLOCK = 128
ROPE_THETA = 500000.0
ROT_FRACTION = 4
SGU_BLOCK = 128
SGU_GROUPS = 8
LANES = 128
MXU_COLS = 256
assert CHUNK & (CHUNK - 1) == 0
INT_MIN = -(2 ** 31)
COUNT_ROWS = 64
ONES_ROWS = 16
DEN_MIN = 2.0 ** -60
LOG2E = 1.4426950408889634
MASK_BIAS = -1e30
VMEM_LIMIT_BYTES = 60 * 1024 * 1024

F32 = jnp.float32
BF16 = jnp.bfloat16


def _params(*sem):
    return pltpu.CompilerParams(dimension_semantics=sem, vmem_limit_bytes=VMEM_LIMIT_BYTES)


def _resident(shape):
    nd = len(shape)
    return pl.BlockSpec(shape, lambda *_: (0,) * nd, pipeline_mode=pl.Buffered(1))


def _rms(xf, g):
    ms = jnp.mean(xf * xf, axis=-1, keepdims=True)
    return xf * lax.rsqrt(ms + EPS) * g


def _dot(a, b):
    return jnp.dot(a, b, preferred_element_type=F32)


def _pool_kernel(x_ref, halo_ref, g_ref, w_ref, scale_ref, o_ref, *, ts):
    i = pl.program_id(1)
    x = x_ref[0]
    g = g_ref[...]
    h = _rms(x, g)
    hh = _rms(halo_ref[0], g)
    hh = jnp.where(i > 0, hh, 0.0)
    hf = jnp.concatenate([hh, h], axis=0)
    t1 = (i * ts + lax.broadcasted_iota(jnp.int32, (ts, 1), 0) + 1).astype(F32)
    cg = x.shape[1] // len(POOL_WINDOWS)
    for gi, w in enumerate(POOL_WINDOWS):
        sl = slice(gi * cg, (gi + 1) * cg)
        s = hf[:, sl]
        k = 1
        while k < w:
            s = s + pltpu.roll(s, k, 0)
            k *= 2
        mean = s[POOL_HALO:] / jnp.minimum(t1, float(w))
        p = (mean - h[:, sl]).astype(BF16)
        y = _dot(p, w_ref[gi]) * scale_ref[:, sl]
        o_ref[0, :, sl] = x[:, sl] + y


def _pool_mixer(x, g, w_bf, scale):
    b, s, d = x.shape
    ts = min(512, s)
    hb = ts // POOL_HALO
    ng = len(POOL_WINDOWS)
    return pl.pallas_call(
        functools.partial(_pool_kernel, ts=ts),
        grid=(b, s // ts),
        in_specs=[
            pl.BlockSpec((1, ts, d), lambda bi, i: (bi, i, 0)),
            pl.BlockSpec((1, POOL_HALO, d), lambda bi, i: (bi, jnp.maximum(i * hb - 1, 0), 0)),
            _resident((1, d)),
            _resident((ng, d // ng, d // ng)),
            _resident((1, d)),
        ],
        out_specs=pl.BlockSpec((1, ts, d), lambda bi, i: (bi, i, 0)),
        out_shape=jax.ShapeDtypeStruct((b, s, d), F32),
        compiler_params=_params("parallel", "parallel"),
        name="pool_mixer",
    )(x, x, g.reshape(1, d), w_bf, scale.reshape(1, d))


def _ffn_kernel(x_ref, g_ref, wu_ref, wd_ref, o_ref, h_ref):
    j = pl.program_id(1)

    @pl.when(j == 0)
    def _():
        x = x_ref[...]
        h_ref[...] = _rms(x, g_ref[...]).astype(BF16)
        o_ref[...] = x

    u = _dot(h_ref[...], wu_ref[...])
    a = jnp.square(jnp.maximum(u, 0.0)).astype(BF16)
    o_ref[...] += _dot(a, wd_ref[...])


def _ffn(x2, g, wu_bf, wd_bf, layer):
    n, d = x2.shape
    f = wu_bf.shape[2]
    tm = min(1024, n)
    tf = 512
    return pl.pallas_call(
        _ffn_kernel,
        grid=(n // tm, f // tf),
        in_specs=[
            pl.BlockSpec((tm, d), lambda i, j: (i, 0)),
            _resident((1, d)),
            pl.BlockSpec((None, d, tf), lambda i, j: (layer, 0, j)),
            pl.BlockSpec((None, tf, d), lambda i, j: (layer, j, 0)),
        ],
        out_specs=pl.BlockSpec((tm, d), lambda i, j: (i, 0)),
        out_shape=jax.ShapeDtypeStruct((n, d), F32),
        scratch_shapes=[pltpu.VMEM((tm, d), BF16)],
        compiler_params=_params("parallel", "arbitrary"),
        name="ffn",
    )(x2, g.reshape(1, d), wu_bf, wd_bf)


def _proj_res_kernel(x_ref, a_ref, w_ref, o_ref):
    o_ref[...] = x_ref[...] + _dot(a_ref[...], w_ref[...])


def _proj_res(x2, a_bf, w_bf):
    n, d = x2.shape
    kdim = a_bf.shape[1]
    tm = min(512, n)
    return pl.pallas_call(
        _proj_res_kernel,
        grid=(n // tm,),
        in_specs=[
            pl.BlockSpec((tm, d), lambda i: (i, 0)),
            pl.BlockSpec((tm, kdim), lambda i: (i, 0)),
            _resident((kdim, d)),
        ],
        out_specs=pl.BlockSpec((tm, d), lambda i: (i, 0)),
        out_shape=jax.ShapeDtypeStruct((n, d), F32),
        compiler_params=_params("parallel"),
        name="proj_res",
    )(x2, a_bf, w_bf)


def _rope_tables(s, width, rot):
    half = rot // 2
    inv = ROPE_THETA ** (-jnp.arange(half, dtype=F32) / half)
    ang = jnp.arange(s, dtype=F32)[:, None] * inv[None, :]
    cos, sin = jnp.cos(ang), jnp.sin(ang)
    pad = jnp.zeros((s, width - rot), F32)
    zero = jnp.zeros((s, half), F32)
    c = jnp.concatenate([cos, cos, pad + 1.0], axis=1)
    a = jnp.concatenate([-sin, zero, pad], axis=1)
    b = jnp.concatenate([zero, sin, pad], axis=1)
    rep = LANES // width
    return tuple(jnp.tile(t, (1, rep)) for t in (c, a, b))


def _rope_tables_t(s, rot):
    half = rot // 2
    inv = ROPE_THETA ** (-jnp.arange(half, dtype=F32) / half)
    ang = inv[:, None] * jnp.arange(s, dtype=F32)[None, :]
    return jnp.cos(ang), jnp.sin(ang)


def _rope(x, c, a, b, half):
    return x * c + pltpu.roll(x, LANES - half, 1) * a + pltpu.roll(x, half, 1) * b


def _rope_t(xt, cos, sin):
    half = cos.shape[0]
    x1, x2 = xt[:half], xt[half:2 * half]
    return jnp.concatenate([x1 * cos - x2 * sin, x2 * cos + x1 * sin, xt[2 * half:]], axis=0)


def _q_proj_kernel(x_ref, g_ref, w_ref, gain_ref, cos_ref, sin_ref, qt_ref):
    h = _rms(x_ref[...], g_ref[...]).astype(BF16)
    tm = h.shape[0]
    gain = jnp.concatenate([gain_ref[...]] * (tm // LANES), axis=1)
    cos, sin = cos_ref[...], sin_ref[...]
    for pair in range(w_ref.shape[1] // MXU_COLS):
        q = _dot(h, w_ref[:, pair * MXU_COLS:(pair + 1) * MXU_COLS])
        for hd in range(MXU_COLS // HEAD_DIM):
            qh = q[:, hd * HEAD_DIM:(hd + 1) * HEAD_DIM].T
            qh = qh * lax.rsqrt(jnp.mean(qh * qh, axis=0, keepdims=True) + EPS) * gain
            qh = _rope_t(qh, cos, sin) * (HEAD_DIM ** -0.5 * LOG2E)
            row = pair * MXU_COLS + hd * HEAD_DIM
            qt_ref[0, row:row + HEAD_DIM, :] = qh.astype(BF16)


def _kv_proj_kernel(x_ref, g_ref, w_ref, gain_ref, c_ref, a_ref, b_ref, ci_ref, ai_ref, bi_ref,
                    cosi_ref, sini_ref, k_ref, vt_ref, qit_ref, ki_ref, wit_ref):
    h = _rms(x_ref[...], g_ref[...]).astype(BF16)
    y = _dot(h, w_ref[...])
    gain = gain_ref[...]
    c, a, b = c_ref[...], a_ref[...], b_ref[...]
    half = HEAD_DIM // ROT_FRACTION // 2
    halfi = IDX_DIM // ROT_FRACTION // 2
    nkv = N_KV_HEADS * HEAD_DIM
    for hd in range(N_KV_HEADS):
        sl = slice(hd * HEAD_DIM, (hd + 1) * HEAD_DIM)
        k_ref[:, sl] = _rope(_rms(y[:, sl], gain), c, a, b, half).astype(BF16)
        vt_ref[0, sl, :] = y[:, nkv + hd * HEAD_DIM:nkv + (hd + 1) * HEAD_DIM].T.astype(BF16)
    nqi = IDX_HEADS * IDX_DIM
    cosi, sini = cosi_ref[...], sini_ref[...]
    for hd in range(IDX_HEADS):
        col = 2 * nkv + hd * IDX_DIM
        if hd % 2 == 0:
            pair_t = y[:, col:col + LANES].T
        qh = pair_t[(hd % 2) * IDX_DIM:(hd % 2 + 1) * IDX_DIM]
        qit_ref[0, hd * IDX_DIM:(hd + 1) * IDX_DIM, :] = _rope_t(qh, cosi, sini).astype(BF16)
    kw = y[:, 2 * nkv + nqi:]
    ki_ref[...] = _rope(kw, ci_ref[...], ai_ref[...], bi_ref[...], halfi)[:, :IDX_DIM].astype(BF16)
    wit_ref[0] = kw.T[IDX_DIM:IDX_DIM + IDX_HEADS] * (IDX_HEADS ** -0.5 * IDX_DIM ** -0.5)


def _key_to_f32(key):
    bits = jnp.where(key < 0, key ^ jnp.int32(0x7FFFFFFF), key)
    f = pltpu.bitcast(bits, F32)
    return jnp.where(f != f, jnp.inf, f)


def _attn_kernel(qt_ref, qit_ref, wit_ref, k_ref, vt_ref, ki_ref, kgain_ref, o_ref, sc_ref, bias_ref,
                 *, tk, n_sel):
    t0 = pl.program_id(1) * Q_BLOCK
    nk = (t0 + Q_BLOCK + tk - 1) // tk
    q_pos = t0 + lax.broadcasted_iota(jnp.int32, (1, Q_BLOCK), 1)
    key_end = (lax.shift_right_logical(q_pos, CHUNK.bit_length() - 1) + 1) * CHUNK
    key_row = lax.broadcasted_iota(jnp.int32, (tk, Q_BLOCK), 0)

    qit = qit_ref[0]
    wit = wit_ref[0]
    npair = IDX_HEADS // 2
    rhs = [jnp.concatenate([qit[(2 * p) * IDX_DIM:(2 * p + 1) * IDX_DIM],
                            qit[(2 * p + 1) * IDX_DIM:(2 * p + 2) * IDX_DIM]], axis=1)
           for p in range(npair)]

    def score_tile(kt, carry):
        off = pl.multiple_of(kt * tk, tk)
        ki_t = ki_ref[0, pl.ds(off, tk), :]
        acc = jnp.zeros((tk, Q_BLOCK), F32)
        for p in range(npair):
            d = jnp.maximum(_dot(ki_t, rhs[p]), 0.0)
            acc = acc + d[:, :Q_BLOCK] * wit[2 * p:2 * p + 1, :]
            acc = acc + d[:, Q_BLOCK:] * wit[2 * p + 1:2 * p + 2, :]
        sc_ref[pl.ds(off, tk), :] = jnp.where(key_row < key_end - off, acc, -jnp.inf)
        return carry

    lax.fori_loop(0, nk, score_tile, 0)

    def count_ge(thr):
        def body(kt, c):
            off = pl.multiple_of(kt * tk, tk)
            m = jnp.where(sc_ref[pl.ds(off, tk), :] >= thr, 1.0, 0.0)
            return c + jnp.sum(m.reshape(tk // COUNT_ROWS, COUNT_ROWS, Q_BLOCK), axis=0)

        c = lax.fori_loop(0, nk, body, jnp.zeros((COUNT_ROWS, Q_BLOCK), F32))
        return jnp.sum(c, axis=0, keepdims=True)

    def bit_body(bi, key):
        cand = key + lax.shift_left(jnp.int32(1), 31 - bi)
        return jnp.where(count_ge(_key_to_f32(cand)) >= n_sel, cand, key)

    key = lax.fori_loop(0, 32, bit_body, jnp.full((1, Q_BLOCK), INT_MIN, jnp.int32))
    thr = jnp.where(key == INT_MIN, jnp.finfo(F32).min, _key_to_f32(key))

    def bias_tile(kt, carry):
        off = pl.multiple_of(kt * tk, tk)
        sel = sc_ref[pl.ds(off, tk), :] >= thr
        bias_ref[pl.ds(off, tk), :] = jnp.where(sel, 0.0, MASK_BIAS).astype(BF16)
        return carry

    lax.fori_loop(0, nk, bias_tile, 0)

    qt = qt_ref[0]
    rep = qt.shape[0] // HEAD_DIM // N_KV_HEADS
    cols = rep * Q_BLOCK
    eye = (lax.broadcasted_iota(jnp.int32, (Q_BLOCK, Q_BLOCK), 0)
           == lax.broadcasted_iota(jnp.int32, (Q_BLOCK, Q_BLOCK), 1))
    eye = jnp.where(eye, 1.0, 0.0).astype(BF16)
    ones = jnp.ones((ONES_ROWS, tk), BF16)
    gsls = [slice(g * HEAD_DIM, (g + 1) * HEAD_DIM) for g in range(N_KV_HEADS)]
    qaugs = []
    for g in range(N_KV_HEADS):
        qg = jnp.concatenate(
            [qt[(g * rep + r) * HEAD_DIM:(g * rep + r + 1) * HEAD_DIM] for r in range(rep)], axis=1)
        qaugs.append(jnp.concatenate([qg, jnp.concatenate([eye] * rep, axis=1)], axis=0))

    def flash(shift):
        def body(kt, accs):
            off = pl.multiple_of(kt * tk, tk)
            bias_t = bias_ref[pl.ds(off, tk), :]
            out = []
            for g in range(N_KV_HEADS):
                kaug = jnp.concatenate([k_ref[0, pl.ds(off, tk), gsls[g]], bias_t], axis=1)
                vaug = jnp.concatenate([vt_ref[0, gsls[g], pl.ds(off, tk)], ones], axis=0)
                p = jnp.exp2(_dot(kaug, qaugs[g]) - shift[g]).astype(BF16)
                out.append(accs[g] + _dot(vaug, p))
            return tuple(out)

        zero = jnp.zeros((HEAD_DIM + ONES_ROWS, cols), F32)
        return lax.fori_loop(0, nk, body, (zero,) * N_KV_HEADS)

    def column_max():
        def body(kt, ms):
            off = pl.multiple_of(kt * tk, tk)
            bias_t = bias_ref[pl.ds(off, tk), :]
            out = []
            for g in range(N_KV_HEADS):
                kaug = jnp.concatenate([k_ref[0, pl.ds(off, tk), gsls[g]], bias_t], axis=1)
                out.append(jnp.maximum(ms[g], jnp.max(_dot(kaug, qaugs[g]), axis=0, keepdims=True)))
            return tuple(out)

        return lax.fori_loop(0, nk, body, (jnp.full((1, cols), MASK_BIAS, F32),) * N_KV_HEADS)

    def write(accs):
        for g in range(N_KV_HEADS):
            o = accs[g][:HEAD_DIM] / accs[g][HEAD_DIM:HEAD_DIM + 1]
            for r in range(rep):
                hsl = slice((g * rep + r) * HEAD_DIM, (g * rep + r + 1) * HEAD_DIM)
                o_ref[0, :, hsl] = o[:, r * Q_BLOCK:(r + 1) * Q_BLOCK].T.astype(BF16)

    kmax = (HEAD_DIM ** 0.5) * jnp.max(jnp.abs(kgain_ref[...]), axis=1, keepdims=True)
    qf = qt.astype(F32)
    qn = [jnp.sqrt(jnp.sum(jnp.square(qf[hd * HEAD_DIM:(hd + 1) * HEAD_DIM]), axis=0, keepdims=True))
          for hd in range(rep * N_KV_HEADS)]
    bound = [jnp.concatenate(qn[g * rep:(g + 1) * rep], axis=1) * kmax for g in range(N_KV_HEADS)]
    accs = flash(bound)
    den = jnp.concatenate([acc[HEAD_DIM:HEAD_DIM + 1] for acc in accs], axis=1)
    safe = jnp.logical_and(jnp.min(den) >= DEN_MIN, jnp.max(den) <= 1.0 / DEN_MIN)

    @pl.when(safe)
    def _():
        write(accs)

    @pl.when(jnp.logical_not(safe))
    def _():
        write(flash(column_max()))


def _sparse_attention(x, g, w_in, q_gain, k_gain, w_o):
    b, s, d = x.shape
    n = b * s
    x2 = x.reshape(n, d)
    nq = d
    nkv = N_KV_HEADS * HEAD_DIM
    nqi = IDX_HEADS * IDX_DIM
    w_q = w_in[:, :nq].astype(BF16)
    pad = LANES - IDX_DIM - IDX_HEADS
    w_rest = jnp.pad(w_in[:, nq:], ((0, 0), (0, pad))).astype(BF16)
    nrest = w_rest.shape[1]
    c, a, bt = _rope_tables(s, HEAD_DIM, HEAD_DIM // ROT_FRACTION)
    ci, ai, bi = _rope_tables(s, IDX_DIM, IDX_DIM // ROT_FRACTION)
    cos_t, sin_t = _rope_tables_t(s, HEAD_DIM // ROT_FRACTION)
    cosi_t, sini_t = _rope_tables_t(s, IDX_DIM // ROT_FRACTION)
    q_gain_b = jnp.broadcast_to(q_gain[:, None], (HEAD_DIM, LANES))

    tm = min(512, s)
    nt = s // tm
    row = lambda i: (i, 0)
    pos = lambda i: (i % nt, 0)
    tcol = lambda i: (i // nt, 0, i % nt)
    tab = pl.BlockSpec((tm, LANES), pos)
    tab_t = lambda half: pl.BlockSpec((half, tm), lambda i: (0, i % nt))
    half = HEAD_DIM // ROT_FRACTION // 2
    halfi = IDX_DIM // ROT_FRACTION // 2
    qt = pl.pallas_call(
        _q_proj_kernel,
        grid=(n // tm,),
        in_specs=[pl.BlockSpec((tm, d), row), _resident((1, d)), _resident((d, nq)),
                  _resident((HEAD_DIM, LANES)), tab_t(half), tab_t(half)],
        out_specs=pl.BlockSpec((1, nq, tm), tcol),
        out_shape=jax.ShapeDtypeStruct((b, nq, s), BF16),
        compiler_params=_params("parallel"),
        name="attn_q_proj",
    )(x2, g.reshape(1, d), w_q, q_gain_b, cos_t, sin_t)

    k, vt, qit, ki, wit = pl.pallas_call(
        _kv_proj_kernel,
        grid=(n // tm,),
        in_specs=[pl.BlockSpec((tm, d), row), _resident((1, d)), _resident((d, nrest)),
                  _resident((1, HEAD_DIM)), tab, tab, tab, tab, tab, tab, tab_t(halfi), tab_t(halfi)],
        out_specs=[pl.BlockSpec((tm, nkv), row), pl.BlockSpec((1, nkv, tm), tcol),
                   pl.BlockSpec((1, nqi, tm), tcol), pl.BlockSpec((tm, IDX_DIM), row),
                   pl.BlockSpec((1, IDX_HEADS, tm), tcol)],
        out_shape=[jax.ShapeDtypeStruct((n, nkv), BF16), jax.ShapeDtypeStruct((b, nkv, s), BF16),
                   jax.ShapeDtypeStruct((b, nqi, s), BF16), jax.ShapeDtypeStruct((n, IDX_DIM), BF16),
                   jax.ShapeDtypeStruct((b, IDX_HEADS, s), F32)],
        compiler_params=_params("parallel"),
        name="attn_kv_proj",
    )(x2, g.reshape(1, d), w_rest, k_gain.reshape(1, HEAD_DIM), c, a, bt, ci, ai, bi, cosi_t, sini_t)

    n_sel = min(INDEX_TOPK, s // 4)
    tk = min(512, s)
    qcol = lambda bi_, qb: (bi_, 0, qb)
    full = lambda bi_, qb: (bi_, 0, 0)
    o = pl.pallas_call(
        functools.partial(_attn_kernel, tk=tk, n_sel=n_sel),
        grid=(b, s // Q_BLOCK),
        in_specs=[pl.BlockSpec((1, nq, Q_BLOCK), qcol), pl.BlockSpec((1, nqi, Q_BLOCK), qcol),
                  pl.BlockSpec((1, IDX_HEADS, Q_BLOCK), qcol), pl.BlockSpec((1, s, nkv), full),
                  pl.BlockSpec((1, nkv, s), full), pl.BlockSpec((1, s, IDX_DIM), full),
                  pl.BlockSpec((1, HEAD_DIM), lambda bi_, qb: (0, 0))],
        out_specs=pl.BlockSpec((1, Q_BLOCK, nq), lambda bi_, qb: (bi_, qb, 0)),
        out_shape=jax.ShapeDtypeStruct((b, s, nq), BF16),
        scratch_shapes=[pltpu.VMEM((s, Q_BLOCK), F32), pltpu.VMEM((s, Q_BLOCK), BF16)],
        compiler_params=_params("parallel", "arbitrary"),
        name="sparse_attn",
    )(qt, qit, wit, k.reshape(b, s, nkv), vt, ki.reshape(b, s, IDX_DIM), k_gain.reshape(1, HEAD_DIM))
    return _proj_res(x2, o.reshape(n, nq), w_o.astype(BF16)).reshape(b, s, d)


def _sgu_kernel(x_ref, g_ref, w_ref, b_ref, vg_ref, ws_ref, bs_ref, o_ref, *, tm):
    h = _rms(x_ref[...], g_ref[...]).astype(BF16)
    z = _dot(h, w_ref[...]) + b_ref[...]
    z = 0.5 * z * (1.0 + lax.erf(z * (2.0 ** -0.5)))
    width = z.shape[1] // 2
    u = z[:, :width]
    v = _rms(z[:, width:], vg_ref[...]).astype(BF16)
    gd = width // SGU_GROUPS
    ii = lax.broadcasted_iota(jnp.int32, (SGU_BLOCK, SGU_BLOCK), 0) // CHUNK
    jj = lax.broadcasted_iota(jnp.int32, (SGU_BLOCK, SGU_BLOCK), 1) // CHUNK
    causal = jj <= ii
    bs = bs_ref[...]
    for gi in range(SGU_GROUPS):
        ws = jnp.where(causal, ws_ref[gi], 0.0).astype(BF16)
        bias = bs[:, gi:gi + 1]
        for nb in range(tm // SGU_BLOCK):
            rs = slice(nb * SGU_BLOCK, (nb + 1) * SGU_BLOCK)
            cs = slice(gi * gd, (gi + 1) * gd)
            mixed = _dot(ws, v[rs, cs]) + bias
            o_ref[rs, cs] = (u[rs, cs] * mixed).astype(BF16)


def _spatial_gating(x, g, w_in, b_in, v_gain, w_s, b_s, w_o):
    b, s, d = x.shape
    n = b * s
    x2 = x.reshape(n, d)
    width = w_in.shape[1] // 2
    tm = min(256, s)
    row = lambda i: (i, 0)
    gated = pl.pallas_call(
        functools.partial(_sgu_kernel, tm=tm),
        grid=(n // tm,),
        in_specs=[pl.BlockSpec((tm, d), row), _resident((1, d)), _resident((d, 2 * width)),
                  _resident((1, 2 * width)), _resident((1, width)),
                  _resident((SGU_GROUPS, SGU_BLOCK, SGU_BLOCK)), _resident((SGU_BLOCK, SGU_GROUPS))],
        out_specs=pl.BlockSpec((tm, width), row),
        out_shape=jax.ShapeDtypeStruct((n, width), BF16),
        compiler_params=_params("parallel"),
        name="sgu_gate",
    )(x2, g.reshape(1, d), w_in.astype(BF16), b_in.reshape(1, 2 * width), v_gain.reshape(1, width),
      w_s, b_s.T)
    return _proj_res(x2, gated, w_o.astype(BF16)).reshape(b, s, d)


def kernel(x, norm_mix, norm_ffn, pool_w, pool_scale, attn_w_in, attn_q_gain, attn_k_gain, attn_w_o,
           sgu_w_in, sgu_b_in, sgu_v_gain, sgu_w_s, sgu_b_s, sgu_w_o, ffn_w_up, ffn_w_down):
    b, s, d = x.shape
    depth = norm_mix.shape[0]
    wu_bf, wd_bf = ffn_w_up.astype(BF16), ffn_w_down.astype(BF16)
    for i in range(depth):
        kind, j = i % 3, i // 3
        if kind == 0:
            x = _pool_mixer(x, norm_mix[i], pool_w[j].astype(BF16), pool_scale[j])
        elif kind == 1:
            x = _sparse_attention(x, norm_mix[i], attn_w_in[j], attn_q_gain[j], attn_k_gain[j],
                                  attn_w_o[j])
        else:
            x = _spatial_gating(x, norm_mix[i], sgu_w_in[j], sgu_b_in[j], sgu_v_gain[j], sgu_w_s[j],
                                sgu_b_s[j], sgu_w_o[j])
        x = _ffn(x.reshape(b * s, d), norm_ffn[i], wu_bf, wd_bf, i).reshape(b, s, d)
    return x
```

```python
import functools

import jax
import jax.numpy as jnp
from jax import lax
from jax.experimental import pallas as pl
from jax.experimental.pallas import tpu as pltpu

EPS = 1e-6
CHUNK = 64
POOL_WINDOWS = (2, 4, 8, 16)
POOL_HALO = 16
HEAD_DIM = 128
N_KV_HEADS = 4
IDX_HEADS = 16
IDX_DIM = 64
INDEX_TOPK = 256
Q_BLOCK = 128
ROPE_THETA = 500000.0
ROT_FRACTION = 4
SGU_BLOCK = 128
SGU_GROUPS = 8
LANES = 128
MXU_COLS = 256
assert CHUNK & (CHUNK - 1) == 0
INT_MIN = -(2 ** 31)
COUNT_ROWS = 64
ONES_ROWS = 16
DEN_MIN = 2.0 ** -60
LOG2E = 1.4426950408889634
MASK_BIAS = -1e30
VMEM_LIMIT_BYTES = 60 * 1024 * 1024

F32 = jnp.float32
BF16 = jnp.bfloat16


def _params(*sem):
    return pltpu.CompilerParams(dimension_semantics=sem, vmem_limit_bytes=VMEM_LIMIT_BYTES)


def _resident(shape):
    nd = len(shape)
    return pl.BlockSpec(shape, lambda *_: (0,) * nd, pipeline_mode=pl.Buffered(1))


def _rms(xf, g):
    ms = jnp.mean(xf * xf, axis=-1, keepdims=True)
    return xf * lax.rsqrt(ms + EPS) * g


def _dot(a, b):
    return jnp.dot(a, b, preferred_element_type=F32)


def _pool_kernel(x_ref, halo_ref, g_ref, w_ref, scale_ref, o_ref, *, ts):
    i = pl.program_id(1)
    x = x_ref[0]
    g = g_ref[...]
    h = _rms(x, g)
    hh = _rms(halo_ref[0], g)
    hh = jnp.where(i > 0, hh, 0.0)
    hf = jnp.concatenate([hh, h], axis=0)
    t1 = (i * ts + lax.broadcasted_iota(jnp.int32, (ts, 1), 0) + 1).astype(F32)
    cg = x.shape[1] // len(POOL_WINDOWS)
    for gi, w in enumerate(POOL_WINDOWS):
        sl = slice(gi * cg, (gi + 1) * cg)
        s = hf[:, sl]
        k = 1
        while k < w:
            s = s + pltpu.roll(s, k, 0)
            k *= 2
        mean = s[POOL_HALO:] / jnp.minimum(t1, float(w))
        p = (mean - h[:, sl]).astype(BF16)
        y = _dot(p, w_ref[gi]) * scale_ref[:, sl]
        o_ref[0, :, sl] = x[:, sl] + y


def _pool_mixer(x, g, w_bf, scale):
    b, s, d = x.shape
    ts = min(512, s)
    hb = ts // POOL_HALO
    ng = len(POOL_WINDOWS)
    return pl.pallas_call(
        functools.partial(_pool_kernel, ts=ts),
        grid=(b, s // ts),
        in_specs=[
            pl.BlockSpec((1, ts, d), lambda bi, i: (bi, i, 0)),
            pl.BlockSpec((1, POOL_HALO, d), lambda bi, i: (bi, jnp.maximum(i * hb - 1, 0), 0)),
            _resident((1, d)),
            _resident((ng, d // ng, d // ng)),
            _resident((1, d)),
        ],
        out_specs=pl.BlockSpec((1, ts, d), lambda bi, i: (bi, i, 0)),
        out_shape=jax.ShapeDtypeStruct((b, s, d), F32),
        compiler_params=_params("parallel", "parallel"),
        name="pool_mixer",
    )(x, x, g.reshape(1, d), w_bf, scale.reshape(1, d))


def _ffn_kernel(x_ref, g_ref, wu_ref, wd_ref, *rest, cast_next):
    if cast_next:
        nu_ref, nd_ref, o_ref, nu_bf_ref, nd_bf_ref, h_ref = rest
        nu_bf_ref[...] = nu_ref[...].astype(BF16)
        nd_bf_ref[...] = nd_ref[...].astype(BF16)
    else:
        o_ref, h_ref = rest
    j = pl.program_id(1)

    @pl.when(j == 0)
    def _():
        x = x_ref[...]
        h_ref[...] = _rms(x, g_ref[...]).astype(BF16)
        o_ref[...] = x

    u = _dot(h_ref[...], wu_ref[...])
    a = jnp.square(jnp.maximum(u, 0.0)).astype(BF16)
    o_ref[...] += _dot(a, wd_ref[...])


def _ffn(x2, g, wu_bf, wd_bf, next_weights=None):
    n, d = x2.shape
    f = wu_bf.shape[1]
    tm = min(1024, n)
    tf = 512
    ni, nj = n // tm, f // tf
    in_specs = [
        pl.BlockSpec((tm, d), lambda i, j: (i, 0)),
        _resident((1, d)),
        pl.BlockSpec((d, tf), lambda i, j: (0, j)),
        pl.BlockSpec((tf, d), lambda i, j: (j, 0)),
    ]
    out_specs = [pl.BlockSpec((tm, d), lambda i, j: (i, 0))]
    out_shape = [jax.ShapeDtypeStruct((n, d), F32)]
    args = [x2, g.reshape(1, d), wu_bf, wd_bf]
    if next_weights is not None:
        w_up, w_down, layer = next_weights
        ru, rd = d // (ni * nj), f // (ni * nj)
        in_specs += [pl.BlockSpec((None, ru, f), lambda i, j: (layer, i * nj + j, 0)),
                     pl.BlockSpec((None, rd, d), lambda i, j: (layer, i * nj + j, 0))]
        out_specs += [pl.BlockSpec((ru, f), lambda i, j: (i * nj + j, 0)),
                      pl.BlockSpec((rd, d), lambda i, j: (i * nj + j, 0))]
        out_shape += [jax.ShapeDtypeStruct((d, f), BF16), jax.ShapeDtypeStruct((f, d), BF16)]
        args += [w_up, w_down]
    return pl.pallas_call(
        functools.partial(_ffn_kernel, cast_next=next_weights is not None),
        grid=(ni, nj),
        in_specs=in_specs,
        out_specs=out_specs,
        out_shape=out_shape,
        scratch_shapes=[pltpu.VMEM((tm, d), BF16)],
        compiler_params=_params("arbitrary", "arbitrary"),
        name="ffn",
    )(*args)


def _proj_res_kernel(x_ref, a_ref, w_ref, o_ref):
    o_ref[...] = x_ref[...] + _dot(a_ref[...], w_ref[...])


def _proj_res(x2, a_bf, w_bf):
    n, d = x2.shape
    kdim = a_bf.shape[1]
    tm = min(512, n)
    return pl.pallas_call(
        _proj_res_kernel,
        grid=(n // tm,),
        in_specs=[
            pl.BlockSpec((tm, d), lambda i: (i, 0)),
            pl.BlockSpec((tm, kdim), lambda i: (i, 0)),
            _resident((kdim, d)),
        ],
        out_specs=pl.BlockSpec((tm, d), lambda i: (i, 0)),
        out_shape=jax.ShapeDtypeStruct((n, d), F32),
        compiler_params=_params("parallel"),
        name="proj_res",
    )(x2, a_bf, w_bf)


def _rope_tables(s, width, rot):
    half = rot // 2
    inv = ROPE_THETA ** (-jnp.arange(half, dtype=F32) / half)
    ang = jnp.arange(s, dtype=F32)[:, None] * inv[None, :]
    cos, sin = jnp.cos(ang), jnp.sin(ang)
    pad = jnp.zeros((s, width - rot), F32)
    zero = jnp.zeros((s, half), F32)
    c = jnp.concatenate([cos, cos, pad + 1.0], axis=1)
    a = jnp.concatenate([-sin, zero, pad], axis=1)
    b = jnp.concatenate([zero, sin, pad], axis=1)
    rep = LANES // width
    return tuple(jnp.tile(t, (1, rep)) for t in (c, a, b))


def _rope_tables_t(s, rot):
    half = rot // 2
    inv = ROPE_THETA ** (-jnp.arange(half, dtype=F32) / half)
    ang = inv[:, None] * jnp.arange(s, dtype=F32)[None, :]
    return jnp.cos(ang), jnp.sin(ang)


def _rope(x, c, a, b, half):
    return x * c + pltpu.roll(x, LANES - half, 1) * a + pltpu.roll(x, half, 1) * b


def _rope_t(xt, cos, sin):
    half = cos.shape[0]
    x1, x2 = xt[:half], xt[half:2 * half]
    return jnp.concatenate([x1 * cos - x2 * sin, x2 * cos + x1 * sin, xt[2 * half:]], axis=0)


def _q_proj_kernel(x_ref, g_ref, w_ref, gain_ref, cos_ref, sin_ref, qt_ref):
    h = _rms(x_ref[...], g_ref[...]).astype(BF16)
    tm = h.shape[0]
    gain = jnp.concatenate([gain_ref[...]] * (tm // LANES), axis=1)
    cos, sin = cos_ref[...], sin_ref[...]
    for pair in range(w_ref.shape[1] // MXU_COLS):
        q = _dot(h, w_ref[:, pair * MXU_COLS:(pair + 1) * MXU_COLS])
        for hd in range(MXU_COLS // HEAD_DIM):
            qh = q[:, hd * HEAD_DIM:(hd + 1) * HEAD_DIM].T
            qh = qh * lax.rsqrt(jnp.mean(qh * qh, axis=0, keepdims=True) + EPS) * gain
            qh = _rope_t(qh, cos, sin) * (HEAD_DIM ** -0.5 * LOG2E)
            row = pair * MXU_COLS + hd * HEAD_DIM
            qt_ref[0, row:row + HEAD_DIM, :] = qh.astype(BF16)


def _kv_proj_kernel(x_ref, g_ref, w_ref, gain_ref, c_ref, a_ref, b_ref, ci_ref, ai_ref, bi_ref,
                    cosi_ref, sini_ref, k_ref, vt_ref, qit_ref, ki_ref, wit_ref):
    h = _rms(x_ref[...], g_ref[...]).astype(BF16)
    y = _dot(h, w_ref[...])
    gain = gain_ref[...]
    c, a, b = c_ref[...], a_ref[...], b_ref[...]
    half = HEAD_DIM // ROT_FRACTION // 2
    halfi = IDX_DIM // ROT_FRACTION // 2
    nkv = N_KV_HEADS * HEAD_DIM
    for hd in range(N_KV_HEADS):
        sl = slice(hd * HEAD_DIM, (hd + 1) * HEAD_DIM)
        k_ref[:, sl] = _rope(_rms(y[:, sl], gain), c, a, b, half).astype(BF16)
        vt_ref[0, sl, :] = y[:, nkv + hd * HEAD_DIM:nkv + (hd + 1) * HEAD_DIM].T.astype(BF16)
    nqi = IDX_HEADS * IDX_DIM
    cosi, sini = cosi_ref[...], sini_ref[...]
    for hd in range(IDX_HEADS):
        col = 2 * nkv + hd * IDX_DIM
        if hd % 2 == 0:
            pair_t = y[:, col:col + LANES].T
        qh = pair_t[(hd % 2) * IDX_DIM:(hd % 2 + 1) * IDX_DIM]
        qit_ref[0, hd * IDX_DIM:(hd + 1) * IDX_DIM, :] = _rope_t(qh, cosi, sini).astype(BF16)
    kw = y[:, 2 * nkv + nqi:]
    ki_ref[...] = _rope(kw, ci_ref[...], ai_ref[...], bi_ref[...], halfi)[:, :IDX_DIM].astype(BF16)
    wit_ref[0] = kw.T[IDX_DIM:IDX_DIM + IDX_HEADS] * (IDX_HEADS ** -0.5 * IDX_DIM ** -0.5)


def _key_to_f32(key):
    bits = jnp.where(key < 0, key ^ jnp.int32(0x7FFFFFFF), key)
    f = pltpu.bitcast(bits, F32)
    return jnp.where(f != f, jnp.inf, f)


def _attn_kernel(qt_ref, qit_ref, wit_ref, k_ref, vt_ref, ki_ref, kgain_ref, o_ref, sc_ref, bias_ref,
                 *, tk, tkf, n_sel):
    t0 = pl.program_id(1) * Q_BLOCK
    nk = (t0 + Q_BLOCK + tk - 1) // tk
    q_pos = t0 + lax.broadcasted_iota(jnp.int32, (1, Q_BLOCK), 1)
    key_end = (lax.shift_right_logical(q_pos, CHUNK.bit_length() - 1) + 1) * CHUNK
    key_row = lax.broadcasted_iota(jnp.int32, (tk, Q_BLOCK), 0)

    qit = qit_ref[0]
    wit = wit_ref[0]
    npair = IDX_HEADS // 2
    rhs = [jnp.concatenate([qit[(2 * p) * IDX_DIM:(2 * p + 1) * IDX_DIM],
                            qit[(2 * p + 1) * IDX_DIM:(2 * p + 2) * IDX_DIM]], axis=1)
           for p in range(npair)]

    def score_tile(kt, carry):
        off = pl.multiple_of(kt * tk, tk)
        ki_t = ki_ref[0, pl.ds(off, tk), :]
        acc = jnp.zeros((tk, Q_BLOCK), F32)
        for p in range(npair):
            d = jnp.maximum(_dot(ki_t, rhs[p]), 0.0)
            acc = acc + d[:, :Q_BLOCK] * wit[2 * p:2 * p + 1, :]
            acc = acc + d[:, Q_BLOCK:] * wit[2 * p + 1:2 * p + 2, :]
        sc_ref[pl.ds(off, tk), :] = jnp.where(key_row < key_end - off, acc, -jnp.inf)
        return carry

    lax.fori_loop(0, nk, score_tile, 0)

    def count_ge(thr):
        def body(kt, c):
            off = pl.multiple_of(kt * tk, tk)
            m = jnp.where(sc_ref[pl.ds(off, tk), :] >= thr, 1.0, 0.0)
            return c + jnp.sum(m.reshape(tk // COUNT_ROWS, COUNT_ROWS, Q_BLOCK), axis=0)

        c = lax.fori_loop(0, nk, body, jnp.zeros((COUNT_ROWS, Q_BLOCK), F32))
        return jnp.sum(c, axis=0, keepdims=True)

    def bit_body(bi, key):
        cand = key + lax.shift_left(jnp.int32(1), 31 - bi)
        return jnp.where(count_ge(_key_to_f32(cand)) >= n_sel, cand, key)

    key = lax.fori_loop(0, 32, bit_body, jnp.full((1, Q_BLOCK), INT_MIN, jnp.int32))
    thr = jnp.where(key == INT_MIN, jnp.finfo(F32).min, _key_to_f32(key))

    def bias_tile(kt, carry):
        off = pl.multiple_of(kt * tk, tk)
        sel = sc_ref[pl.ds(off, tk), :] >= thr
        bias_ref[pl.ds(off, tk), :] = jnp.where(sel, 0.0, MASK_BIAS).astype(BF16)
        return carry

    lax.fori_loop(0, nk, bias_tile, 0)

    nkf = (t0 + Q_BLOCK + tkf - 1) // tkf

    def mask_tile(kt, carry):
        off = pl.multiple_of(kt * tk, tk)
        bias_ref[pl.ds(off, tk), :] = jnp.full((tk, Q_BLOCK), MASK_BIAS, BF16)
        return carry

    lax.fori_loop(nk, nkf * (tkf // tk), mask_tile, 0)

    qt = qt_ref[0]
    rep = qt.shape[0] // HEAD_DIM // N_KV_HEADS
    cols = rep * Q_BLOCK
    eye = (lax.broadcasted_iota(jnp.int32, (Q_BLOCK, Q_BLOCK), 0)
           == lax.broadcasted_iota(jnp.int32, (Q_BLOCK, Q_BLOCK), 1))
    eye = jnp.where(eye, 1.0, 0.0).astype(BF16)
    ones = jnp.ones((ONES_ROWS, tkf), BF16)
    gsls = [slice(g * HEAD_DIM, (g + 1) * HEAD_DIM) for g in range(N_KV_HEADS)]
    qaugs = []
    for g in range(N_KV_HEADS):
        qg = jnp.concatenate(
            [qt[(g * rep + r) * HEAD_DIM:(g * rep + r + 1) * HEAD_DIM] for r in range(rep)], axis=1)
        qaugs.append(jnp.concatenate([qg, jnp.concatenate([eye] * rep, axis=1)], axis=0))

    def flash(shift):
        def body(kt, accs):
            off = pl.multiple_of(kt * tkf, tkf)
            bias_t = bias_ref[pl.ds(off, tkf), :]
            out = []
            for g in range(N_KV_HEADS):
                kaug = jnp.concatenate([k_ref[0, pl.ds(off, tkf), gsls[g]], bias_t], axis=1)
                vaug = jnp.concatenate([vt_ref[0, gsls[g], pl.ds(off, tkf)], ones], axis=0)
                p = jnp.exp2(_dot(kaug, qaugs[g]) - shift[g]).astype(BF16)
                out.append(accs[g] + _dot(vaug, p))
            return tuple(out)

        zero = jnp.zeros((HEAD_DIM + ONES_ROWS, cols), F32)
        return lax.fori_loop(0, nkf, body, (zero,) * N_KV_HEADS)

    def column_max():
        def body(kt, ms):
            off = pl.multiple_of(kt * tkf, tkf)
            bias_t = bias_ref[pl.ds(off, tkf), :]
            out = []
            for g in range(N_KV_HEADS):
                kaug = jnp.concatenate([k_ref[0, pl.ds(off, tkf), gsls[g]], bias_t], axis=1)
                out.append(jnp.maximum(ms[g], jnp.max(_dot(kaug, qaugs[g]), axis=0, keepdims=True)))
            return tuple(out)

        return lax.fori_loop(0, nkf, body, (jnp.full((1, cols), MASK_BIAS, F32),) * N_KV_HEADS)

    def write(accs):
        for g in range(N_KV_HEADS):
            o = accs[g][:HEAD_DIM] / accs[g][HEAD_DIM:HEAD_DIM + 1]
            for r in range(rep):
                hsl = slice((g * rep + r) * HEAD_DIM, (g * rep + r + 1) * HEAD_DIM)
                o_ref[0, :, hsl] = o[:, r * Q_BLOCK:(r + 1) * Q_BLOCK].T.astype(BF16)

    kmax = (HEAD_DIM ** 0.5) * jnp.max(jnp.abs(kgain_ref[...]), axis=1, keepdims=True)
    qf = qt.astype(F32)
    qn = [jnp.sqrt(jnp.sum(jnp.square(qf[hd * HEAD_DIM:(hd + 1) * HEAD_DIM]), axis=0, keepdims=True))
          for hd in range(rep * N_KV_HEADS)]
    bound = [jnp.concatenate(qn[g * rep:(g + 1) * rep], axis=1) * kmax for g in range(N_KV_HEADS)]
    accs = flash(bound)
    den = jnp.concatenate([acc[HEAD_DIM:HEAD_DIM + 1] for acc in accs], axis=1)
    safe = jnp.logical_and(jnp.min(den) >= DEN_MIN, jnp.max(den) <= 1.0 / DEN_MIN)

    @pl.when(safe)
    def _():
        write(accs)

    @pl.when(jnp.logical_not(safe))
    def _():
        write(flash(column_max()))


def _sparse_attention(x, g, w_in, q_gain, k_gain, w_o):
    b, s, d = x.shape
    n = b * s
    x2 = x.reshape(n, d)
    nq = d
    nkv = N_KV_HEADS * HEAD_DIM
    nqi = IDX_HEADS * IDX_DIM
    w_q = w_in[:, :nq].astype(BF16)
    pad = LANES - IDX_DIM - IDX_HEADS
    w_rest = jnp.pad(w_in[:, nq:], ((0, 0), (0, pad))).astype(BF16)
    nrest = w_rest.shape[1]
    c, a, bt = _rope_tables(s, HEAD_DIM, HEAD_DIM // ROT_FRACTION)
    ci, ai, bi = _rope_tables(s, IDX_DIM, IDX_DIM // ROT_FRACTION)
    cos_t, sin_t = _rope_tables_t(s, HEAD_DIM // ROT_FRACTION)
    cosi_t, sini_t = _rope_tables_t(s, IDX_DIM // ROT_FRACTION)
    q_gain_b = jnp.broadcast_to(q_gain[:, None], (HEAD_DIM, LANES))

    tm = min(512, s)
    nt = s // tm
    row = lambda i: (i, 0)
    pos = lambda i: (i % nt, 0)
    tcol = lambda i: (i // nt, 0, i % nt)
    tab = pl.BlockSpec((tm, LANES), pos)
    tab_t = lambda half: pl.BlockSpec((half, tm), lambda i: (0, i % nt))
    half = HEAD_DIM // ROT_FRACTION // 2
    halfi = IDX_DIM // ROT_FRACTION // 2
    qt = pl.pallas_call(
        _q_proj_kernel,
        grid=(n // tm,),
        in_specs=[pl.BlockSpec((tm, d), row), _resident((1, d)), _resident((d, nq)),
                  _resident((HEAD_DIM, LANES)), tab_t(half), tab_t(half)],
        out_specs=pl.BlockSpec((1, nq, tm), tcol),
        out_shape=jax.ShapeDtypeStruct((b, nq, s), BF16),
        compiler_params=_params("parallel"),
        name="attn_q_proj",
    )(x2, g.reshape(1, d), w_q, q_gain_b, cos_t, sin_t)

    k, vt, qit, ki, wit = pl.pallas_call(
        _kv_proj_kernel,
        grid=(n // tm,),
        in_specs=[pl.BlockSpec((tm, d), row), _resident((1, d)), _resident((d, nrest)),
                  _resident((1, HEAD_DIM)), tab, tab, tab, tab, tab, tab, tab_t(halfi), tab_t(halfi)],
        out_specs=[pl.BlockSpec((tm, nkv), row), pl.BlockSpec((1, nkv, tm), tcol),
                   pl.BlockSpec((1, nqi, tm), tcol), pl.BlockSpec((tm, IDX_DIM), row),
                   pl.BlockSpec((1, IDX_HEADS, tm), tcol)],
        out_shape=[jax.ShapeDtypeStruct((n, nkv), BF16), jax.ShapeDtypeStruct((b, nkv, s), BF16),
                   jax.ShapeDtypeStruct((b, nqi, s), BF16), jax.ShapeDtypeStruct((n, IDX_DIM), BF16),
                   jax.ShapeDtypeStruct((b, IDX_HEADS, s), F32)],
        compiler_params=_params("parallel"),
        name="attn_kv_proj",
    )(x2, g.reshape(1, d), w_rest, k_gain.reshape(1, HEAD_DIM), c, a, bt, ci, ai, bi, cosi_t, sini_t)

    n_sel = min(INDEX_TOPK, s // 4)
    tk = min(512, s)
    qcol = lambda bi_, qb: (bi_, 0, qb)
    full = lambda bi_, qb: (bi_, 0, 0)
    o = pl.pallas_call(
        functools.partial(_attn_kernel, tk=tk, tkf=min(1024, s), n_sel=n_sel),
        grid=(b, s // Q_BLOCK),
        in_specs=[pl.BlockSpec((1, nq, Q_BLOCK), qcol), pl.BlockSpec((1, nqi, Q_BLOCK), qcol),
                  pl.BlockSpec((1, IDX_HEADS, Q_BLOCK), qcol), pl.BlockSpec((1, s, nkv), full),
                  pl.BlockSpec((1, nkv, s), full), pl.BlockSpec((1, s, IDX_DIM), full),
                  pl.BlockSpec((1, HEAD_DIM), lambda bi_, qb: (0, 0))],
        out_specs=pl.BlockSpec((1, Q_BLOCK, nq), lambda bi_, qb: (bi_, qb, 0)),
        out_shape=jax.ShapeDtypeStruct((b, s, nq), BF16),
        scratch_shapes=[pltpu.VMEM((s, Q_BLOCK), F32), pltpu.VMEM((s, Q_BLOCK), BF16)],
        compiler_params=_params("parallel", "arbitrary"),
        name="sparse_attn",
    )(qt, qit, wit, k.reshape(b, s, nkv), vt, ki.reshape(b, s, IDX_DIM), k_gain.reshape(1, HEAD_DIM))
    return _proj_res(x2, o.reshape(n, nq), w_o.astype(BF16)).reshape(b, s, d)


def _sgu_kernel(x_ref, g_ref, w_ref, b_ref, vg_ref, ws_ref, bs_ref, o_ref, *, tm):
    h = _rms(x_ref[...], g_ref[...]).astype(BF16)
    z = _dot(h, w_ref[...]) + b_ref[...]
    z = 0.5 * z * (1.0 + lax.erf(z * (2.0 ** -0.5)))
    width = z.shape[1] // 2
    u = z[:, :width]
    v = _rms(z[:, width:], vg_ref[...]).astype(BF16)
    gd = width // SGU_GROUPS
    ii = lax.broadcasted_iota(jnp.int32, (SGU_BLOCK, SGU_BLOCK), 0) // CHUNK
    jj = lax.broadcasted_iota(jnp.int32, (SGU_BLOCK, SGU_BLOCK), 1) // CHUNK
    causal = jj <= ii
    bs = bs_ref[...]
    for gi in range(SGU_GROUPS):
        ws = jnp.where(causal, ws_ref[gi], 0.0).astype(BF16)
        bias = bs[:, gi:gi + 1]
        for nb in range(tm // SGU_BLOCK):
            rs = slice(nb * SGU_BLOCK, (nb + 1) * SGU_BLOCK)
            cs = slice(gi * gd, (gi + 1) * gd)
            mixed = _dot(ws, v[rs, cs]) + bias
            o_ref[rs, cs] = (u[rs, cs] * mixed).astype(BF16)


def _spatial_gating(x, g, w_in, b_in, v_gain, w_s, b_s, w_o):
    b, s, d = x.shape
    n = b * s
    x2 = x.reshape(n, d)
    width = w_in.shape[1] // 2
    tm = min(256, s)
    row = lambda i: (i, 0)
    gated = pl.pallas_call(
        functools.partial(_sgu_kernel, tm=tm),
        grid=(n // tm,),
        in_specs=[pl.BlockSpec((tm, d), row), _resident((1, d)), _resident((d, 2 * width)),
                  _resident((1, 2 * width)), _resident((1, width)),
                  _resident((SGU_GROUPS, SGU_BLOCK, SGU_BLOCK)), _resident((SGU_BLOCK, SGU_GROUPS))],
        out_specs=pl.BlockSpec((tm, width), row),
        out_shape=jax.ShapeDtypeStruct((n, width), BF16),
        compiler_params=_params("parallel"),
        name="sgu_gate",
    )(x2, g.reshape(1, d), w_in.astype(BF16), b_in.reshape(1, 2 * width), v_gain.reshape(1, width),
      w_s, b_s.T)
    return _proj_res(x2, gated, w_o.astype(BF16)).reshape(b, s, d)


def kernel(x, norm_mix, norm_ffn, pool_w, pool_scale, attn_w_in, attn_q_gain, attn_k_gain, attn_w_o,
           sgu_w_in, sgu_b_in, sgu_v_gain, sgu_w_s, sgu_b_s, sgu_w_o, ffn_w_up, ffn_w_down):
    b, s, d = x.shape
    depth = norm_mix.shape[0]
    wu_bf, wd_bf = ffn_w_up[0].astype(BF16), ffn_w_down[0].astype(BF16)
    for i in range(depth):
        kind, j = i % 3, i // 3
        if kind == 0:
            x = _pool_mixer(x, norm_mix[i], pool_w[j].astype(BF16), pool_scale[j])
        elif kind == 1:
            x = _sparse_attention(x, norm_mix[i], attn_w_in[j], attn_q_gain[j], attn_k_gain[j],
                                  attn_w_o[j])
        else:
            x = _spatial_gating(x, norm_mix[i], sgu_w_in[j], sgu_b_in[j], sgu_v_gain[j], sgu_w_s[j],
                                sgu_b_s[j], sgu_w_o[j])
        nxt = (ffn_w_up, ffn_w_down, i + 1) if i + 1 < depth else None
        x, *cast = _ffn(x.reshape(b * s, d), norm_ffn[i], wu_bf, wd_bf, nxt)
        x = x.reshape(b, s, d)
        if cast:
            wu_bf, wd_bf = cast
    return x
```

```python
import functools

import jax
import jax.numpy as jnp
from jax import lax
from jax.experimental import pallas as pl
from jax.experimental.pallas import tpu as pltpu

EPS = 1e-6
CHUNK = 64
POOL_WINDOWS = (2, 4, 8, 16)
POOL_HALO = 16
HEAD_DIM = 128
N_KV_HEADS = 4
IDX_HEADS = 16
IDX_DIM = 64
INDEX_TOPK = 256
Q_BLOCK = 128
ROPE_THETA = 500000.0
ROT_FRACTION = 4
SGU_BLOCK = 128
SGU_GROUPS = 8
LANES = 128
MXU_COLS = 256
assert CHUNK & (CHUNK - 1) == 0
INT_MIN = -(2 ** 31)
COUNT_ROWS = 64
ONES_ROWS = 16
DEN_MIN = 2.0 ** -60
LOG2E = 1.4426950408889634
MASK_BIAS = -1e30
VMEM_LIMIT_BYTES = 60 * 1024 * 1024

F32 = jnp.float32
BF16 = jnp.bfloat16


def _params(*sem):
    return pltpu.CompilerParams(dimension_semantics=sem, vmem_limit_bytes=VMEM_LIMIT_BYTES)


def _resident(shape):
    nd = len(shape)
    return pl.BlockSpec(shape, lambda *_: (0,) * nd, pipeline_mode=pl.Buffered(1))


def _rms(xf, g):
    ms = jnp.mean(xf * xf, axis=-1, keepdims=True)
    return xf * lax.rsqrt(ms + EPS) * g


def _dot(a, b):
    return jnp.dot(a, b, preferred_element_type=F32)


def _pool_kernel(x_ref, halo_ref, g_ref, w_ref, scale_ref, o_ref, *, ts):
    i = pl.program_id(1)
    x = x_ref[0]
    g = g_ref[...]
    h = _rms(x, g)
    hh = _rms(halo_ref[0], g)
    hh = jnp.where(i > 0, hh, 0.0)
    hf = jnp.concatenate([hh, h], axis=0)
    t1 = (i * ts + lax.broadcasted_iota(jnp.int32, (ts, 1), 0) + 1).astype(F32)
    cg = x.shape[1] // len(POOL_WINDOWS)
    for gi, w in enumerate(POOL_WINDOWS):
        sl = slice(gi * cg, (gi + 1) * cg)
        s = hf[:, sl]
        k = 1
        while k < w:
            s = s + pltpu.roll(s, k, 0)
            k *= 2
        mean = s[POOL_HALO:] / jnp.minimum(t1, float(w))
        p = (mean - h[:, sl]).astype(BF16)
        y = _dot(p, w_ref[gi]) * scale_ref[:, sl]
        o_ref[0, :, sl] = x[:, sl] + y


def _pool_mixer(x, g, w_bf, scale):
    b, s, d = x.shape
    ts = min(512, s)
    hb = ts // POOL_HALO
    ng = len(POOL_WINDOWS)
    return pl.pallas_call(
        functools.partial(_pool_kernel, ts=ts),
        grid=(b, s // ts),
        in_specs=[
            pl.BlockSpec((1, ts, d), lambda bi, i: (bi, i, 0)),
            pl.BlockSpec((1, POOL_HALO, d), lambda bi, i: (bi, jnp.maximum(i * hb - 1, 0), 0)),
            _resident((1, d)),
            _resident((ng, d // ng, d // ng)),
            _resident((1, d)),
        ],
        out_specs=pl.BlockSpec((1, ts, d), lambda bi, i: (bi, i, 0)),
        out_shape=jax.ShapeDtypeStruct((b, s, d), F32),
        compiler_params=_params("parallel", "parallel"),
        name="pool_mixer",
    )(x, x, g.reshape(1, d), w_bf, scale.reshape(1, d))


def _ffn_kernel(x_ref, g_ref, wu_ref, wd_ref, o_ref, h_ref):
    j = pl.program_id(1)

    @pl.when(j == 0)
    def _():
        x = x_ref[...]
        h_ref[...] = _rms(x, g_ref[...]).astype(BF16)
        o_ref[...] = x

    u = _dot(h_ref[...], wu_ref[...].astype(BF16))
    a = jnp.square(jnp.maximum(u, 0.0)).astype(BF16)
    o_ref[...] += _dot(a, wd_ref[...].astype(BF16))


def _ffn(x2, g, w_up, w_down, layer):
    n, d = x2.shape
    f = w_up.shape[2]
    tm = min(1024, n)
    tf = 512
    return pl.pallas_call(
        _ffn_kernel,
        grid=(n // tm, f // tf),
        in_specs=[
            pl.BlockSpec((tm, d), lambda i, j: (i, 0)),
            _resident((1, d)),
            pl.BlockSpec((None, d, tf), lambda i, j: (layer, 0, j)),
            pl.BlockSpec((None, tf, d), lambda i, j: (layer, j, 0)),
        ],
        out_specs=pl.BlockSpec((tm, d), lambda i, j: (i, 0)),
        out_shape=jax.ShapeDtypeStruct((n, d), F32),
        scratch_shapes=[pltpu.VMEM((tm, d), BF16)],
        compiler_params=_params("parallel", "arbitrary"),
        name="ffn",
    )(x2, g.reshape(1, d), w_up, w_down)


def _proj_res_kernel(x_ref, a_ref, w_ref, o_ref):
    o_ref[...] = x_ref[...] + _dot(a_ref[...], w_ref[...])


def _proj_res(x2, a_bf, w_bf):
    n, d = x2.shape
    kdim = a_bf.shape[1]
    tm = min(512, n)
    return pl.pallas_call(
        _proj_res_kernel,
        grid=(n // tm,),
        in_specs=[
            pl.BlockSpec((tm, d), lambda i: (i, 0)),
            pl.BlockSpec((tm, kdim), lambda i: (i, 0)),
            _resident((kdim, d)),
        ],
        out_specs=pl.BlockSpec((tm, d), lambda i: (i, 0)),
        out_shape=jax.ShapeDtypeStruct((n, d), F32),
        compiler_params=_params("parallel"),
        name="proj_res",
    )(x2, a_bf, w_bf)


def _rope_tables(s, width, rot):
    half = rot // 2
    inv = ROPE_THETA ** (-jnp.arange(half, dtype=F32) / half)
    ang = jnp.arange(s, dtype=F32)[:, None] * inv[None, :]
    cos, sin = jnp.cos(ang), jnp.sin(ang)
    pad = jnp.zeros((s, width - rot), F32)
    zero = jnp.zeros((s, half), F32)
    c = jnp.concatenate([cos, cos, pad + 1.0], axis=1)
    a = jnp.concatenate([-sin, zero, pad], axis=1)
    b = jnp.concatenate([zero, sin, pad], axis=1)
    rep = LANES // width
    return tuple(jnp.tile(t, (1, rep)) for t in (c, a, b))


def _rope_tables_t(s, rot):
    half = rot // 2
    inv = ROPE_THETA ** (-jnp.arange(half, dtype=F32) / half)
    ang = inv[:, None] * jnp.arange(s, dtype=F32)[None, :]
    return jnp.cos(ang), jnp.sin(ang)


def _rope(x, c, a, b, half):
    return x * c + pltpu.roll(x, LANES - half, 1) * a + pltpu.roll(x, half, 1) * b


def _rope_t(xt, cos, sin):
    half = cos.shape[0]
    x1, x2 = xt[:half], xt[half:2 * half]
    return jnp.concatenate([x1 * cos - x2 * sin, x2 * cos + x1 * sin, xt[2 * half:]], axis=0)


def _q_proj_kernel(x_ref, g_ref, w_ref, gain_ref, cos_ref, sin_ref, qt_ref):
    h = _rms(x_ref[...], g_ref[...]).astype(BF16)
    tm = h.shape[0]
    gain = jnp.concatenate([gain_ref[...]] * (tm // LANES), axis=1)
    cos, sin = cos_ref[...], sin_ref[...]
    for pair in range(w_ref.shape[1] // MXU_COLS):
        q = _dot(h, w_ref[:, pair * MXU_COLS:(pair + 1) * MXU_COLS])
        for hd in range(MXU_COLS // HEAD_DIM):
            qh = q[:, hd * HEAD_DIM:(hd + 1) * HEAD_DIM].T
            qh = qh * lax.rsqrt(jnp.mean(qh * qh, axis=0, keepdims=True) + EPS) * gain
            qh = _rope_t(qh, cos, sin) * (HEAD_DIM ** -0.5 * LOG2E)
            row = pair * MXU_COLS + hd * HEAD_DIM
            qt_ref[0, row:row + HEAD_DIM, :] = qh.astype(BF16)


def _kv_proj_kernel(x_ref, g_ref, w_ref, gain_ref, c_ref, a_ref, b_ref, ci_ref, ai_ref, bi_ref,
                    cosi_ref, sini_ref, k_ref, vt_ref, qit_ref, ki_ref, wit_ref):
    h = _rms(x_ref[...], g_ref[...]).astype(BF16)
    y = _dot(h, w_ref[...])
    gain = gain_ref[...]
    c, a, b = c_ref[...], a_ref[...], b_ref[...]
    half = HEAD_DIM // ROT_FRACTION // 2
    halfi = IDX_DIM // ROT_FRACTION // 2
    nkv = N_KV_HEADS * HEAD_DIM
    for hd in range(N_KV_HEADS):
        sl = slice(hd * HEAD_DIM, (hd + 1) * HEAD_DIM)
        k_ref[:, sl] = _rope(_rms(y[:, sl], gain), c, a, b, half).astype(BF16)
        vt_ref[0, sl, :] = y[:, nkv + hd * HEAD_DIM:nkv + (hd + 1) * HEAD_DIM].T.astype(BF16)
    nqi = IDX_HEADS * IDX_DIM
    cosi, sini = cosi_ref[...], sini_ref[...]
    for hd in range(IDX_HEADS):
        col = 2 * nkv + hd * IDX_DIM
        if hd % 2 == 0:
            pair_t = y[:, col:col + LANES].T
        qh = pair_t[(hd % 2) * IDX_DIM:(hd % 2 + 1) * IDX_DIM]
        qit_ref[0, hd * IDX_DIM:(hd + 1) * IDX_DIM, :] = _rope_t(qh, cosi, sini).astype(BF16)
    kw = y[:, 2 * nkv + nqi:]
    ki_ref[...] = _rope(kw, ci_ref[...], ai_ref[...], bi_ref[...], halfi)[:, :IDX_DIM].astype(BF16)
    wit_ref[0] = kw.T[IDX_DIM:IDX_DIM + IDX_HEADS] * (IDX_HEADS ** -0.5 * IDX_DIM ** -0.5)


def _key_to_f32(key):
    bits = jnp.where(key < 0, key ^ jnp.int32(0x7FFFFFFF), key)
    f = pltpu.bitcast(bits, F32)
    return jnp.where(f != f, jnp.inf, f)


def _attn_kernel(qt_ref, qit_ref, wit_ref, k_ref, vt_ref, ki_ref, kgain_ref, o_ref, sc_ref, bias_ref,
                 *, tk, tkf, n_sel):
    t0 = pl.program_id(1) * Q_BLOCK
    nk = (t0 + Q_BLOCK + tk - 1) // tk
    q_pos = t0 + lax.broadcasted_iota(jnp.int32, (1, Q_BLOCK), 1)
    key_end = (lax.shift_right_logical(q_pos, CHUNK.bit_length() - 1) + 1) * CHUNK
    key_row = lax.broadcasted_iota(jnp.int32, (tk, Q_BLOCK), 0)

    qit = qit_ref[0]
    wit = wit_ref[0]
    npair = IDX_HEADS // 2
    rhs = [jnp.concatenate([qit[(2 * p) * IDX_DIM:(2 * p + 1) * IDX_DIM],
                            qit[(2 * p + 1) * IDX_DIM:(2 * p + 2) * IDX_DIM]], axis=1)
           for p in range(npair)]

    def score_tile(kt, carry):
        off = pl.multiple_of(kt * tk, tk)
        ki_t = ki_ref[0, pl.ds(off, tk), :]
        acc = jnp.zeros((tk, Q_BLOCK), F32)
        for p in range(npair):
            d = jnp.maximum(_dot(ki_t, rhs[p]), 0.0)
            acc = acc + d[:, :Q_BLOCK] * wit[2 * p:2 * p + 1, :]
            acc = acc + d[:, Q_BLOCK:] * wit[2 * p + 1:2 * p + 2, :]
        sc_ref[pl.ds(off, tk), :] = jnp.where(key_row < key_end - off, acc, -jnp.inf)
        return carry

    lax.fori_loop(0, nk, score_tile, 0)

    def count_ge(thr):
        def body(kt, c):
            off = pl.multiple_of(kt * tk, tk)
            m = jnp.where(sc_ref[pl.ds(off, tk), :] >= thr, 1.0, 0.0)
            return c + jnp.sum(m.reshape(tk // COUNT_ROWS, COUNT_ROWS, Q_BLOCK), axis=0)

        c = lax.fori_loop(0, nk, body, jnp.zeros((COUNT_ROWS, Q_BLOCK), F32))
        return jnp.sum(c, axis=0, keepdims=True)

    def bit_body(bi, key):
        cand = key + lax.shift_left(jnp.int32(1), 31 - bi)
        return jnp.where(count_ge(_key_to_f32(cand)) >= n_sel, cand, key)

    key = lax.fori_loop(0, 32, bit_body, jnp.full((1, Q_BLOCK), INT_MIN, jnp.int32))
    thr = jnp.where(key == INT_MIN, jnp.finfo(F32).min, _key_to_f32(key))

    def bias_tile(kt, carry):
        off = pl.multiple_of(kt * tk, tk)
        sel = sc_ref[pl.ds(off, tk), :] >= thr
        bias_ref[pl.ds(off, tk), :] = jnp.where(sel, 0.0, MASK_BIAS).astype(BF16)
        return carry

    lax.fori_loop(0, nk, bias_tile, 0)

    nkf = (t0 + Q_BLOCK + tkf - 1) // tkf

    def mask_tile(kt, carry):
        off = pl.multiple_of(kt * tk, tk)
        bias_ref[pl.ds(off, tk), :] = jnp.full((tk, Q_BLOCK), MASK_BIAS, BF16)
        return carry

    lax.fori_loop(nk, nkf * (tkf // tk), mask_tile, 0)

    qt = qt_ref[0]
    rep = qt.shape[0] // HEAD_DIM // N_KV_HEADS
    cols = rep * Q_BLOCK
    eye = (lax.broadcasted_iota(jnp.int32, (Q_BLOCK, Q_BLOCK), 0)
           == lax.broadcasted_iota(jnp.int32, (Q_BLOCK, Q_BLOCK), 1))
    eye = jnp.where(eye, 1.0, 0.0).astype(BF16)
    ones = jnp.ones((ONES_ROWS, tkf), BF16)
    gsls = [slice(g * HEAD_DIM, (g + 1) * HEAD_DIM) for g in range(N_KV_HEADS)]
    qaugs = []
    for g in range(N_KV_HEADS):
        qg = jnp.concatenate(
            [qt[(g * rep + r) * HEAD_DIM:(g * rep + r + 1) * HEAD_DIM] for r in range(rep)], axis=1)
        qaugs.append(jnp.concatenate([qg, jnp.concatenate([eye] * rep, axis=1)], axis=0))

    def flash(shift):
        def body(kt, accs):
            off = pl.multiple_of(kt * tkf, tkf)
            bias_t = bias_ref[pl.ds(off, tkf), :]
            out = []
            for g in range(N_KV_HEADS):
                kaug = jnp.concatenate([k_ref[0, pl.ds(off, tkf), gsls[g]], bias_t], axis=1)
                vaug = jnp.concatenate([vt_ref[0, gsls[g], pl.ds(off, tkf)], ones], axis=0)
                p = jnp.exp2(_dot(kaug, qaugs[g]) - shift[g]).astype(BF16)
                out.append(accs[g] + _dot(vaug, p))
            return tuple(out)

        zero = jnp.zeros((HEAD_DIM + ONES_ROWS, cols), F32)
        return lax.fori_loop(0, nkf, body, (zero,) * N_KV_HEADS)

    def column_max():
        def body(kt, ms):
            off = pl.multiple_of(kt * tkf, tkf)
            bias_t = bias_ref[pl.ds(off, tkf), :]
            out = []
            for g in range(N_KV_HEADS):
                kaug = jnp.concatenate([k_ref[0, pl.ds(off, tkf), gsls[g]], bias_t], axis=1)
                out.append(jnp.maximum(ms[g], jnp.max(_dot(kaug, qaugs[g]), axis=0, keepdims=True)))
            return tuple(out)

        return lax.fori_loop(0, nkf, body, (jnp.full((1, cols), MASK_BIAS, F32),) * N_KV_HEADS)

    def write(accs):
        for g in range(N_KV_HEADS):
            o = accs[g][:HEAD_DIM] / accs[g][HEAD_DIM:HEAD_DIM + 1]
            for r in range(rep):
                hsl = slice((g * rep + r) * HEAD_DIM, (g * rep + r + 1) * HEAD_DIM)
                o_ref[0, :, hsl] = o[:, r * Q_BLOCK:(r + 1) * Q_BLOCK].T.astype(BF16)

    kmax = (HEAD_DIM ** 0.5) * jnp.max(jnp.abs(kgain_ref[...]), axis=1, keepdims=True)
    qf = qt.astype(F32)
    qn = [jnp.sqrt(jnp.sum(jnp.square(qf[hd * HEAD_DIM:(hd + 1) * HEAD_DIM]), axis=0, keepdims=True))
          for hd in range(rep * N_KV_HEADS)]
    bound = [jnp.concatenate(qn[g * rep:(g + 1) * rep], axis=1) * kmax for g in range(N_KV_HEADS)]
    accs = flash(bound)
    den = jnp.concatenate([acc[HEAD_DIM:HEAD_DIM + 1] for acc in accs], axis=1)
    safe = jnp.logical_and(jnp.min(den) >= DEN_MIN, jnp.max(den) <= 1.0 / DEN_MIN)

    @pl.when(safe)
    def _():
        write(accs)

    @pl.when(jnp.logical_not(safe))
    def _():
        write(flash(column_max()))


def _sparse_attention(x, g, w_in, q_gain, k_gain, w_o):
    b, s, d = x.shape
    n = b * s
    x2 = x.reshape(n, d)
    nq = d
    nkv = N_KV_HEADS * HEAD_DIM
    nqi = IDX_HEADS * IDX_DIM
    w_q = w_in[:, :nq].astype(BF16)
    pad = LANES - IDX_DIM - IDX_HEADS
    w_rest = jnp.pad(w_in[:, nq:], ((0, 0), (0, pad))).astype(BF16)
    nrest = w_rest.shape[1]
    c, a, bt = _rope_tables(s, HEAD_DIM, HEAD_DIM // ROT_FRACTION)
    ci, ai, bi = _rope_tables(s, IDX_DIM, IDX_DIM // ROT_FRACTION)
    cos_t, sin_t = _rope_tables_t(s, HEAD_DIM // ROT_FRACTION)
    cosi_t, sini_t = _rope_tables_t(s, IDX_DIM // ROT_FRACTION)
    q_gain_b = jnp.broadcast_to(q_gain[:, None], (HEAD_DIM, LANES))

    tm = min(512, s)
    nt = s // tm
    row = lambda i: (i, 0)
    pos = lambda i: (i % nt, 0)
    tcol = lambda i: (i // nt, 0, i % nt)
    tab = pl.BlockSpec((tm, LANES), pos)
    tab_t = lambda half: pl.BlockSpec((half, tm), lambda i: (0, i % nt))
    half = HEAD_DIM // ROT_FRACTION // 2
    halfi = IDX_DIM // ROT_FRACTION // 2
    qt = pl.pallas_call(
        _q_proj_kernel,
        grid=(n // tm,),
        in_specs=[pl.BlockSpec((tm, d), row), _resident((1, d)), _resident((d, nq)),
                  _resident((HEAD_DIM, LANES)), tab_t(half), tab_t(half)],
        out_specs=pl.BlockSpec((1, nq, tm), tcol),
        out_shape=jax.ShapeDtypeStruct((b, nq, s), BF16),
        compiler_params=_params("parallel"),
        name="attn_q_proj",
    )(x2, g.reshape(1, d), w_q, q_gain_b, cos_t, sin_t)

    k, vt, qit, ki, wit = pl.pallas_call(
        _kv_proj_kernel,
        grid=(n // tm,),
        in_specs=[pl.BlockSpec((tm, d), row), _resident((1, d)), _resident((d, nrest)),
                  _resident((1, HEAD_DIM)), tab, tab, tab, tab, tab, tab, tab_t(halfi), tab_t(halfi)],
        out_specs=[pl.BlockSpec((tm, nkv), row), pl.BlockSpec((1, nkv, tm), tcol),
                   pl.BlockSpec((1, nqi, tm), tcol), pl.BlockSpec((tm, IDX_DIM), row),
                   pl.BlockSpec((1, IDX_HEADS, tm), tcol)],
        out_shape=[jax.ShapeDtypeStruct((n, nkv), BF16), jax.ShapeDtypeStruct((b, nkv, s), BF16),
                   jax.ShapeDtypeStruct((b, nqi, s), BF16), jax.ShapeDtypeStruct((n, IDX_DIM), BF16),
                   jax.ShapeDtypeStruct((b, IDX_HEADS, s), F32)],
        compiler_params=_params("parallel"),
        name="attn_kv_proj",
    )(x2, g.reshape(1, d), w_rest, k_gain.reshape(1, HEAD_DIM), c, a, bt, ci, ai, bi, cosi_t, sini_t)

    n_sel = min(INDEX_TOPK, s // 4)
    tk = min(512, s)
    qcol = lambda bi_, qb: (bi_, 0, qb)
    full = lambda bi_, qb: (bi_, 0, 0)
    o = pl.pallas_call(
        functools.partial(_attn_kernel, tk=tk, tkf=min(1024, s), n_sel=n_sel),
        grid=(b, s // Q_BLOCK),
        in_specs=[pl.BlockSpec((1, nq, Q_BLOCK), qcol), pl.BlockSpec((1, nqi, Q_BLOCK), qcol),
                  pl.BlockSpec((1, IDX_HEADS, Q_BLOCK), qcol), pl.BlockSpec((1, s, nkv), full),
                  pl.BlockSpec((1, nkv, s), full), pl.BlockSpec((1, s, IDX_DIM), full),
                  pl.BlockSpec((1, HEAD_DIM), lambda bi_, qb: (0, 0))],
        out_specs=pl.BlockSpec((1, Q_BLOCK, nq), lambda bi_, qb: (bi_, qb, 0)),
        out_shape=jax.ShapeDtypeStruct((b, s, nq), BF16),
        scratch_shapes=[pltpu.VMEM((s, Q_BLOCK), F32), pltpu.VMEM((s, Q_BLOCK), BF16)],
        compiler_params=_params("parallel", "arbitrary"),
        name="sparse_attn",
    )(qt, qit, wit, k.reshape(b, s, nkv), vt, ki.reshape(b, s, IDX_DIM), k_gain.reshape(1, HEAD_DIM))
    return _proj_res(x2, o.reshape(n, nq), w_o.astype(BF16)).reshape(b, s, d)


def _sgu_kernel(x_ref, g_ref, w_ref, b_ref, vg_ref, ws_ref, bs_ref, o_ref, *, tm):
    h = _rms(x_ref[...], g_ref[...]).astype(BF16)
    z = _dot(h, w_ref[...]) + b_ref[...]
    z = 0.5 * z * (1.0 + lax.erf(z * (2.0 ** -0.5)))
    width = z.shape[1] // 2
    u = z[:, :width]
    v = _rms(z[:, width:], vg_ref[...]).astype(BF16)
    gd = width // SGU_GROUPS
    ii = lax.broadcasted_iota(jnp.int32, (SGU_BLOCK, SGU_BLOCK), 0) // CHUNK
    jj = lax.broadcasted_iota(jnp.int32, (SGU_BLOCK, SGU_BLOCK), 1) // CHUNK
    causal = jj <= ii
    bs = bs_ref[...]
    for gi in range(SGU_GROUPS):
        ws = jnp.where(causal, ws_ref[gi], 0.0).astype(BF16)
        bias = bs[:, gi:gi + 1]
        for nb in range(tm // SGU_BLOCK):
            rs = slice(nb * SGU_BLOCK, (nb + 1) * SGU_BLOCK)
            cs = slice(gi * gd, (gi + 1) * gd)
            mixed = _dot(ws, v[rs, cs]) + bias
            o_ref[rs, cs] = (u[rs, cs] * mixed).astype(BF16)


def _spatial_gating(x, g, w_in, b_in, v_gain, w_s, b_s, w_o):
    b, s, d = x.shape
    n = b * s
    x2 = x.reshape(n, d)
    width = w_in.shape[1] // 2
    tm = min(256, s)
    row = lambda i: (i, 0)
    gated = pl.pallas_call(
        functools.partial(_sgu_kernel, tm=tm),
        grid=(n // tm,),
        in_specs=[pl.BlockSpec((tm, d), row), _resident((1, d)), _resident((d, 2 * width)),
                  _resident((1, 2 * width)), _resident((1, width)),
                  _resident((SGU_GROUPS, SGU_BLOCK, SGU_BLOCK)), _resident((SGU_BLOCK, SGU_GROUPS))],
        out_specs=pl.BlockSpec((tm, width), row),
        out_shape=jax.ShapeDtypeStruct((n, width), BF16),
        compiler_params=_params("parallel"),
        name="sgu_gate",
    )(x2, g.reshape(1, d), w_in.astype(BF16), b_in.reshape(1, 2 * width), v_gain.reshape(1, width),
      w_s, b_s.T)
    return _proj_res(x2, gated, w_o.astype(BF16)).reshape(b, s, d)


def kernel(x, norm_mix, norm_ffn, pool_w, pool_scale, attn_w_in, attn_q_gain, attn_k_gain, attn_w_o,
           sgu_w_in, sgu_b_in, sgu_v_gain, sgu_w_s, sgu_b_s, sgu_w_o, ffn_w_up, ffn_w_down):
    b, s, d = x.shape
    depth = norm_mix.shape[0]
    for i in range(depth):
        kind, j = i % 3, i // 3
        if kind == 0:
            x = _pool_mixer(x, norm_mix[i], pool_w[j].astype(BF16), pool_scale[j])
        elif kind == 1:
            x = _sparse_attention(x, norm_mix[i], attn_w_in[j], attn_q_gain[j], attn_k_gain[j],
                                  attn_w_o[j])
        else:
            x = _spatial_gating(x, norm_mix[i], sgu_w_in[j], sgu_b_in[j], sgu_v_gain[j], sgu_w_s[j],
                                sgu_b_s[j], sgu_w_o[j])
        x = _ffn(x.reshape(b * s, d), norm_ffn[i], ffn_w_up, ffn_w_down, i).reshape(b, s, d)
    return x
```

```python
import functools

import jax
import jax.numpy as jnp
from jax import lax
from jax.experimental import pallas as pl
from jax.experimental.pallas import tpu as pltpu

EPS = 1e-6
CHUNK = 64
POOL_WINDOWS = (2, 4, 8, 16)
POOL_HALO = 16
HEAD_DIM = 128
N_KV_HEADS = 4
IDX_HEADS = 16
IDX_DIM = 64
INDEX_TOPK = 256
Q_BLOCK = 128
ROPE_THETA = 500000.0
ROT_FRACTION = 4
SGU_BLOCK = 128
SGU_GROUPS = 8
LANES = 128
MXU_COLS = 256
assert CHUNK & (CHUNK - 1) == 0
INT_MIN = -(2 ** 31)
COUNT_ROWS = 64
ONES_ROWS = 16
DEN_MIN = 2.0 ** -60
LOG2E = 1.4426950408889634
MASK_BIAS = -1e30
VMEM_LIMIT_BYTES = 60 * 1024 * 1024

F32 = jnp.float32
BF16 = jnp.bfloat16


def _params(*sem):
    return pltpu.CompilerParams(dimension_semantics=sem, vmem_limit_bytes=VMEM_LIMIT_BYTES)


def _resident(shape):
    nd = len(shape)
    return pl.BlockSpec(shape, lambda *_: (0,) * nd, pipeline_mode=pl.Buffered(1))


def _cast_once(w_ref, wbf_ref):
    @pl.when(pl.program_id(0) == 0)
    def _():
        wbf_ref[...] = w_ref[...].astype(BF16)


def _rms(xf, g):
    ms = jnp.mean(xf * xf, axis=-1, keepdims=True)
    return xf * lax.rsqrt(ms + EPS) * g


def _dot(a, b):
    return jnp.dot(a, b, preferred_element_type=F32)


def _pool_kernel(x_ref, halo_ref, g_ref, w_ref, scale_ref, o_ref, *, ts):
    i = pl.program_id(1)
    x = x_ref[0]
    g = g_ref[...]
    h = _rms(x, g)
    hh = _rms(halo_ref[0], g)
    hh = jnp.where(i > 0, hh, 0.0)
    hf = jnp.concatenate([hh, h], axis=0)
    t1 = (i * ts + lax.broadcasted_iota(jnp.int32, (ts, 1), 0) + 1).astype(F32)
    cg = x.shape[1] // len(POOL_WINDOWS)
    for gi, w in enumerate(POOL_WINDOWS):
        sl = slice(gi * cg, (gi + 1) * cg)
        s = hf[:, sl]
        k = 1
        while k < w:
            s = s + pltpu.roll(s, k, 0)
            k *= 2
        mean = s[POOL_HALO:] / jnp.minimum(t1, float(w))
        p = (mean - h[:, sl]).astype(BF16)
        y = _dot(p, w_ref[gi]) * scale_ref[:, sl]
        o_ref[0, :, sl] = x[:, sl] + y


def _pool_mixer(x, g, w_bf, scale):
    b, s, d = x.shape
    ts = min(512, s)
    hb = ts // POOL_HALO
    ng = len(POOL_WINDOWS)
    return pl.pallas_call(
        functools.partial(_pool_kernel, ts=ts),
        grid=(b, s // ts),
        in_specs=[
            pl.BlockSpec((1, ts, d), lambda bi, i: (bi, i, 0)),
            pl.BlockSpec((1, POOL_HALO, d), lambda bi, i: (bi, jnp.maximum(i * hb - 1, 0), 0)),
            _resident((1, d)),
            _resident((ng, d // ng, d // ng)),
            _resident((1, d)),
        ],
        out_specs=pl.BlockSpec((1, ts, d), lambda bi, i: (bi, i, 0)),
        out_shape=jax.ShapeDtypeStruct((b, s, d), F32),
        compiler_params=_params("parallel", "parallel"),
        name="pool_mixer",
    )(x, x, g.reshape(1, d), w_bf, scale.reshape(1, d))


def _ffn_kernel(x_ref, g_ref, wu_ref, wd_ref, o_ref, h_ref):
    j = pl.program_id(1)

    @pl.when(j == 0)
    def _():
        x = x_ref[...]
        h_ref[...] = _rms(x, g_ref[...]).astype(BF16)
        o_ref[...] = x

    u = _dot(h_ref[...], wu_ref[...].astype(BF16))
    a = jnp.square(jnp.maximum(u, 0.0)).astype(BF16)
    o_ref[...] += _dot(a, wd_ref[...].astype(BF16))


def _ffn(x2, g, w_up, w_down, layer):
    n, d = x2.shape
    f = w_up.shape[2]
    tm = min(1024, n)
    tf = 512
    return pl.pallas_call(
        _ffn_kernel,
        grid=(n // tm, f // tf),
        in_specs=[
            pl.BlockSpec((tm, d), lambda i, j: (i, 0)),
            _resident((1, d)),
            pl.BlockSpec((None, d, tf), lambda i, j: (layer, 0, j)),
            pl.BlockSpec((None, tf, d), lambda i, j: (layer, j, 0)),
        ],
        out_specs=pl.BlockSpec((tm, d), lambda i, j: (i, 0)),
        out_shape=jax.ShapeDtypeStruct((n, d), F32),
        scratch_shapes=[pltpu.VMEM((tm, d), BF16)],
        compiler_params=_params("parallel", "arbitrary"),
        name="ffn",
    )(x2, g.reshape(1, d), w_up, w_down)


def _proj_res_kernel(x_ref, a_ref, w_ref, o_ref, wbf_ref):
    _cast_once(w_ref, wbf_ref)
    o_ref[...] = x_ref[...] + _dot(a_ref[...], wbf_ref[...])


def _proj_res(x2, a_bf, w):
    n, d = x2.shape
    kdim = a_bf.shape[1]
    tm = min(512, n)
    return pl.pallas_call(
        _proj_res_kernel,
        grid=(n // tm,),
        in_specs=[
            pl.BlockSpec((tm, d), lambda i: (i, 0)),
            pl.BlockSpec((tm, kdim), lambda i: (i, 0)),
            _resident((kdim, d)),
        ],
        out_specs=pl.BlockSpec((tm, d), lambda i: (i, 0)),
        out_shape=jax.ShapeDtypeStruct((n, d), F32),
        scratch_shapes=[pltpu.VMEM((kdim, d), BF16)],
        compiler_params=_params("arbitrary"),
        name="proj_res",
    )(x2, a_bf, w)


def _rope_tables(s, width, rot):
    half = rot // 2
    inv = ROPE_THETA ** (-jnp.arange(half, dtype=F32) / half)
    ang = jnp.arange(s, dtype=F32)[:, None] * inv[None, :]
    cos, sin = jnp.cos(ang), jnp.sin(ang)
    pad = jnp.zeros((s, width - rot), F32)
    zero = jnp.zeros((s, half), F32)
    c = jnp.concatenate([cos, cos, pad + 1.0], axis=1)
    a = jnp.concatenate([-sin, zero, pad], axis=1)
    b = jnp.concatenate([zero, sin, pad], axis=1)
    rep = LANES // width
    return tuple(jnp.tile(t, (1, rep)) for t in (c, a, b))


def _rope_tables_t(s, rot):
    half = rot // 2
    inv = ROPE_THETA ** (-jnp.arange(half, dtype=F32) / half)
    ang = inv[:, None] * jnp.arange(s, dtype=F32)[None, :]
    return jnp.cos(ang), jnp.sin(ang)


def _rope(x, c, a, b, half):
    return x * c + pltpu.roll(x, LANES - half, 1) * a + pltpu.roll(x, half, 1) * b


def _rope_t(xt, cos, sin):
    half = cos.shape[0]
    x1, x2 = xt[:half], xt[half:2 * half]
    return jnp.concatenate([x1 * cos - x2 * sin, x2 * cos + x1 * sin, xt[2 * half:]], axis=0)


def _q_proj_kernel(x_ref, g_ref, w_ref, gain_ref, cos_ref, sin_ref, qt_ref, wbf_ref):
    _cast_once(w_ref, wbf_ref)
    h = _rms(x_ref[...], g_ref[...]).astype(BF16)
    tm = h.shape[0]
    gain = jnp.concatenate([gain_ref[...]] * (tm // LANES), axis=1)
    cos, sin = cos_ref[...], sin_ref[...]
    for pair in range(wbf_ref.shape[1] // MXU_COLS):
        q = _dot(h, wbf_ref[:, pair * MXU_COLS:(pair + 1) * MXU_COLS])
        for hd in range(MXU_COLS // HEAD_DIM):
            qh = q[:, hd * HEAD_DIM:(hd + 1) * HEAD_DIM].T
            qh = qh * lax.rsqrt(jnp.mean(qh * qh, axis=0, keepdims=True) + EPS) * gain
            qh = _rope_t(qh, cos, sin) * (HEAD_DIM ** -0.5 * LOG2E)
            row = pair * MXU_COLS + hd * HEAD_DIM
            qt_ref[0, row:row + HEAD_DIM, :] = qh.astype(BF16)


def _kv_proj_kernel(x_ref, g_ref, w_ref, wtail_ref, gain_ref, c_ref, a_ref, b_ref, ci_ref, ai_ref,
                    bi_ref, cosi_ref, sini_ref, k_ref, vt_ref, qit_ref, ki_ref, wit_ref, wbf_ref):
    _cast_once(w_ref, wbf_ref)
    h = _rms(x_ref[...], g_ref[...]).astype(BF16)
    y = _dot(h, wbf_ref[...])
    gain = gain_ref[...]
    c, a, b = c_ref[...], a_ref[...], b_ref[...]
    half = HEAD_DIM // ROT_FRACTION // 2
    halfi = IDX_DIM // ROT_FRACTION // 2
    nkv = N_KV_HEADS * HEAD_DIM
    for hd in range(N_KV_HEADS):
        sl = slice(hd * HEAD_DIM, (hd + 1) * HEAD_DIM)
        k_ref[:, sl] = _rope(_rms(y[:, sl], gain), c, a, b, half).astype(BF16)
        vt_ref[0, sl, :] = y[:, nkv + hd * HEAD_DIM:nkv + (hd + 1) * HEAD_DIM].T.astype(BF16)
    nqi = IDX_HEADS * IDX_DIM
    cosi, sini = cosi_ref[...], sini_ref[...]
    for hd in range(IDX_HEADS):
        col = 2 * nkv + hd * IDX_DIM
        if hd % 2 == 0:
            pair_t = y[:, col:col + LANES].T
        qh = pair_t[(hd % 2) * IDX_DIM:(hd % 2 + 1) * IDX_DIM]
        qit_ref[0, hd * IDX_DIM:(hd + 1) * IDX_DIM, :] = _rope_t(qh, cosi, sini).astype(BF16)
    kw = _dot(h, wtail_ref[...])
    ki_ref[...] = _rope(kw, ci_ref[...], ai_ref[...], bi_ref[...], halfi)[:, :IDX_DIM].astype(BF16)
    wit_ref[0] = kw.T[IDX_DIM:IDX_DIM + IDX_HEADS] * (IDX_HEADS ** -0.5 * IDX_DIM ** -0.5)


def _key_to_f32(key):
    bits = jnp.where(key < 0, key ^ jnp.int32(0x7FFFFFFF), key)
    f = pltpu.bitcast(bits, F32)
    return jnp.where(f != f, jnp.inf, f)


def _attn_kernel(qt_ref, qit_ref, wit_ref, k_ref, vt_ref, ki_ref, kgain_ref, o_ref, sc_ref, bias_ref,
                 *, tk, tkf, n_sel):
    t0 = pl.program_id(1) * Q_BLOCK
    nk = (t0 + Q_BLOCK + tk - 1) // tk
    q_pos = t0 + lax.broadcasted_iota(jnp.int32, (1, Q_BLOCK), 1)
    key_end = (lax.shift_right_logical(q_pos, CHUNK.bit_length() - 1) + 1) * CHUNK
    key_row = lax.broadcasted_iota(jnp.int32, (tk, Q_BLOCK), 0)

    qit = qit_ref[0]
    wit = wit_ref[0]
    npair = IDX_HEADS // 2
    rhs = [jnp.concatenate([qit[(2 * p) * IDX_DIM:(2 * p + 1) * IDX_DIM],
                            qit[(2 * p + 1) * IDX_DIM:(2 * p + 2) * IDX_DIM]], axis=1)
           for p in range(npair)]

    def score_tile(kt, carry):
        off = pl.multiple_of(kt * tk, tk)
        ki_t = ki_ref[0, pl.ds(off, tk), :]
        acc = jnp.zeros((tk, Q_BLOCK), F32)
        for p in range(npair):
            d = jnp.maximum(_dot(ki_t, rhs[p]), 0.0)
            acc = acc + d[:, :Q_BLOCK] * wit[2 * p:2 * p + 1, :]
            acc = acc + d[:, Q_BLOCK:] * wit[2 * p + 1:2 * p + 2, :]
        sc_ref[pl.ds(off, tk), :] = jnp.where(key_row < key_end - off, acc, -jnp.inf)
        return carry

    lax.fori_loop(0, nk, score_tile, 0)

    def count_ge(thr):
        def body(kt, c):
            off = pl.multiple_of(kt * tk, tk)
            m = jnp.where(sc_ref[pl.ds(off, tk), :] >= thr, 1.0, 0.0)
            return c + jnp.sum(m.reshape(tk // COUNT_ROWS, COUNT_ROWS, Q_BLOCK), axis=0)

        c = lax.fori_loop(0, nk, body, jnp.zeros((COUNT_ROWS, Q_BLOCK), F32))
        return jnp.sum(c, axis=0, keepdims=True)

    def bit_body(bi, key):
        cand = key + lax.shift_left(jnp.int32(1), 31 - bi)
        return jnp.where(count_ge(_key_to_f32(cand)) >= n_sel, cand, key)

    nbits = jnp.where(t0 + Q_BLOCK <= n_sel, 0, 32)
    key = lax.fori_loop(0, nbits, bit_body, jnp.full((1, Q_BLOCK), INT_MIN, jnp.int32))
    thr = jnp.where(key == INT_MIN, jnp.finfo(F32).min, _key_to_f32(key))

    def bias_tile(kt, carry):
        off = pl.multiple_of(kt * tk, tk)
        sel = sc_ref[pl.ds(off, tk), :] >= thr
        bias_ref[pl.ds(off, tk), :] = jnp.where(sel, 0.0, MASK_BIAS).astype(BF16)
        return carry

    lax.fori_loop(0, nk, bias_tile, 0)

    nkf = (t0 + Q_BLOCK + tkf - 1) // tkf

    def mask_tile(kt, carry):
        off = pl.multiple_of(kt * tk, tk)
        bias_ref[pl.ds(off, tk), :] = jnp.full((tk, Q_BLOCK), MASK_BIAS, BF16)
        return carry

    lax.fori_loop(nk, nkf * (tkf // tk), mask_tile, 0)

    qt = qt_ref[0]
    rep = qt.shape[0] // HEAD_DIM // N_KV_HEADS
    cols = rep * Q_BLOCK
    eye = (lax.broadcasted_iota(jnp.int32, (Q_BLOCK, Q_BLOCK), 0)
           == lax.broadcasted_iota(jnp.int32, (Q_BLOCK, Q_BLOCK), 1))
    eye = jnp.where(eye, 1.0, 0.0).astype(BF16)
    ones = jnp.ones((ONES_ROWS, tkf), BF16)
    gsls = [slice(g * HEAD_DIM, (g + 1) * HEAD_DIM) for g in range(N_KV_HEADS)]
    qaugs = []
    for g in range(N_KV_HEADS):
        qg = jnp.concatenate(
            [qt[(g * rep + r) * HEAD_DIM:(g * rep + r + 1) * HEAD_DIM] for r in range(rep)], axis=1)
        qaugs.append(jnp.concatenate([qg, jnp.concatenate([eye] * rep, axis=1)], axis=0))

    def flash(shift):
        def body(kt, accs):
            off = pl.multiple_of(kt * tkf, tkf)
            bias_t = bias_ref[pl.ds(off, tkf), :]
            out = []
            for g in range(N_KV_HEADS):
                kaug = jnp.concatenate([k_ref[0, pl.ds(off, tkf), gsls[g]], bias_t], axis=1)
                vaug = jnp.concatenate([vt_ref[0, gsls[g], pl.ds(off, tkf)], ones], axis=0)
                p = jnp.exp2(_dot(kaug, qaugs[g]) - shift[g]).astype(BF16)
                out.append(accs[g] + _dot(vaug, p))
            return tuple(out)

        zero = jnp.zeros((HEAD_DIM + ONES_ROWS, cols), F32)
        return lax.fori_loop(0, nkf, body, (zero,) * N_KV_HEADS)

    def column_max():
        def body(kt, ms):
            off = pl.multiple_of(kt * tkf, tkf)
            bias_t = bias_ref[pl.ds(off, tkf), :]
            out = []
            for g in range(N_KV_HEADS):
                kaug = jnp.concatenate([k_ref[0, pl.ds(off, tkf), gsls[g]], bias_t], axis=1)
                out.append(jnp.maximum(ms[g], jnp.max(_dot(kaug, qaugs[g]), axis=0, keepdims=True)))
            return tuple(out)

        return lax.fori_loop(0, nkf, body, (jnp.full((1, cols), MASK_BIAS, F32),) * N_KV_HEADS)

    def write(accs):
        for g in range(N_KV_HEADS):
            o = accs[g][:HEAD_DIM] / accs[g][HEAD_DIM:HEAD_DIM + 1]
            for r in range(rep):
                hsl = slice((g * rep + r) * HEAD_DIM, (g * rep + r + 1) * HEAD_DIM)
                o_ref[0, :, hsl] = o[:, r * Q_BLOCK:(r + 1) * Q_BLOCK].T.astype(BF16)

    kmax = (HEAD_DIM ** 0.5) * jnp.max(jnp.abs(kgain_ref[...]), axis=1, keepdims=True)
    qf = qt.astype(F32)
    qn = [jnp.sqrt(jnp.sum(jnp.square(qf[hd * HEAD_DIM:(hd + 1) * HEAD_DIM]), axis=0, keepdims=True))
          for hd in range(rep * N_KV_HEADS)]
    bound = [jnp.concatenate(qn[g * rep:(g + 1) * rep], axis=1) * kmax for g in range(N_KV_HEADS)]
    accs = flash(bound)
    den = jnp.concatenate([acc[HEAD_DIM:HEAD_DIM + 1] for acc in accs], axis=1)
    safe = jnp.logical_and(jnp.min(den) >= DEN_MIN, jnp.max(den) <= 1.0 / DEN_MIN)

    @pl.when(safe)
    def _():
        write(accs)

    @pl.when(jnp.logical_not(safe))
    def _():
        write(flash(column_max()))


def _sparse_attention(x, g, w_in, q_gain, k_gain, w_o):
    b, s, d = x.shape
    n = b * s
    x2 = x.reshape(n, d)
    nq = d
    nkv = N_KV_HEADS * HEAD_DIM
    nqi = IDX_HEADS * IDX_DIM
    nmid = 2 * nkv + nqi
    assert nmid == nq
    pad = LANES - IDX_DIM - IDX_HEADS
    w_tail = jnp.pad(w_in[:, nq + nmid:], ((0, 0), (0, pad))).astype(BF16)
    wblock = lambda col: pl.BlockSpec((d, nq), lambda i: (0, col), pipeline_mode=pl.Buffered(1))
    c, a, bt = _rope_tables(s, HEAD_DIM, HEAD_DIM // ROT_FRACTION)
    ci, ai, bi = _rope_tables(s, IDX_DIM, IDX_DIM // ROT_FRACTION)
    cos_t, sin_t = _rope_tables_t(s, HEAD_DIM // ROT_FRACTION)
    cosi_t, sini_t = _rope_tables_t(s, IDX_DIM // ROT_FRACTION)
    q_gain_b = jnp.broadcast_to(q_gain[:, None], (HEAD_DIM, LANES))

    tm = min(512, s)
    nt = s // tm
    row = lambda i: (i, 0)
    pos = lambda i: (i % nt, 0)
    tcol = lambda i: (i // nt, 0, i % nt)
    tab = pl.BlockSpec((tm, LANES), pos)
    tab_t = lambda half: pl.BlockSpec((half, tm), lambda i: (0, i % nt))
    half = HEAD_DIM // ROT_FRACTION // 2
    halfi = IDX_DIM // ROT_FRACTION // 2
    qt = pl.pallas_call(
        _q_proj_kernel,
        grid=(n // tm,),
        in_specs=[pl.BlockSpec((tm, d), row), _resident((1, d)), wblock(0),
                  _resident((HEAD_DIM, LANES)), tab_t(half), tab_t(half)],
        out_specs=pl.BlockSpec((1, nq, tm), tcol),
        out_shape=jax.ShapeDtypeStruct((b, nq, s), BF16),
        scratch_shapes=[pltpu.VMEM((d, nq), BF16)],
        compiler_params=_params("arbitrary"),
        name="attn_q_proj",
    )(x2, g.reshape(1, d), w_in, q_gain_b, cos_t, sin_t)

    k, vt, qit, ki, wit = pl.pallas_call(
        _kv_proj_kernel,
        grid=(n // tm,),
        in_specs=[pl.BlockSpec((tm, d), row), _resident((1, d)), wblock(1), _resident((d, LANES)),
                  _resident((1, HEAD_DIM)), tab, tab, tab, tab, tab, tab, tab_t(halfi), tab_t(halfi)],
        out_specs=[pl.BlockSpec((tm, nkv), row), pl.BlockSpec((1, nkv, tm), tcol),
                   pl.BlockSpec((1, nqi, tm), tcol), pl.BlockSpec((tm, IDX_DIM), row),
                   pl.BlockSpec((1, IDX_HEADS, tm), tcol)],
        out_shape=[jax.ShapeDtypeStruct((n, nkv), BF16), jax.ShapeDtypeStruct((b, nkv, s), BF16),
                   jax.ShapeDtypeStruct((b, nqi, s), BF16), jax.ShapeDtypeStruct((n, IDX_DIM), BF16),
                   jax.ShapeDtypeStruct((b, IDX_HEADS, s), F32)],
        scratch_shapes=[pltpu.VMEM((d, nmid), BF16)],
        compiler_params=_params("arbitrary"),
        name="attn_kv_proj",
    )(x2, g.reshape(1, d), w_in, w_tail, k_gain.reshape(1, HEAD_DIM), c, a, bt, ci, ai, bi, cosi_t,
      sini_t)

    n_sel = min(INDEX_TOPK, s // 4)
    tk = min(512, s)
    qcol = lambda bi_, qb: (bi_, 0, qb)
    full = lambda bi_, qb: (bi_, 0, 0)
    o = pl.pallas_call(
        functools.partial(_attn_kernel, tk=tk, tkf=min(1024, s), n_sel=n_sel),
        grid=(b, s // Q_BLOCK),
        in_specs=[pl.BlockSpec((1, nq, Q_BLOCK), qcol), pl.BlockSpec((1, nqi, Q_BLOCK), qcol),
                  pl.BlockSpec((1, IDX_HEADS, Q_BLOCK), qcol), pl.BlockSpec((1, s, nkv), full),
                  pl.BlockSpec((1, nkv, s), full), pl.BlockSpec((1, s, IDX_DIM), full),
                  pl.BlockSpec((1, HEAD_DIM), lambda bi_, qb: (0, 0))],
        out_specs=pl.BlockSpec((1, Q_BLOCK, nq), lambda bi_, qb: (bi_, qb, 0)),
        out_shape=jax.ShapeDtypeStruct((b, s, nq), BF16),
        scratch_shapes=[pltpu.VMEM((s, Q_BLOCK), F32), pltpu.VMEM((s, Q_BLOCK), BF16)],
        compiler_params=_params("parallel", "arbitrary"),
        name="sparse_attn",
    )(qt, qit, wit, k.reshape(b, s, nkv), vt, ki.reshape(b, s, IDX_DIM), k_gain.reshape(1, HEAD_DIM))
    return _proj_res(x2, o.reshape(n, nq), w_o).reshape(b, s, d)


def _sgu_kernel(x_ref, g_ref, w_ref, b_ref, vg_ref, ws_ref, bs_ref, o_ref, *, tm):
    h = _rms(x_ref[...], g_ref[...]).astype(BF16)
    z = _dot(h, w_ref[...]) + b_ref[...]
    z = 0.5 * z * (1.0 + lax.erf(z * (2.0 ** -0.5)))
    width = z.shape[1] // 2
    u = z[:, :width]
    v = _rms(z[:, width:], vg_ref[...]).astype(BF16)
    gd = width // SGU_GROUPS
    ii = lax.broadcasted_iota(jnp.int32, (SGU_BLOCK, SGU_BLOCK), 0) // CHUNK
    jj = lax.broadcasted_iota(jnp.int32, (SGU_BLOCK, SGU_BLOCK), 1) // CHUNK
    causal = jj <= ii
    bs = bs_ref[...]
    for gi in range(SGU_GROUPS):
        ws = jnp.where(causal, ws_ref[gi], 0.0).astype(BF16)
        bias = bs[:, gi:gi + 1]
        for nb in range(tm // SGU_BLOCK):
            rs = slice(nb * SGU_BLOCK, (nb + 1) * SGU_BLOCK)
            cs = slice(gi * gd, (gi + 1) * gd)
            mixed = _dot(ws, v[rs, cs]) + bias
            o_ref[rs, cs] = (u[rs, cs] * mixed).astype(BF16)


def _spatial_gating(x, g, w_in, b_in, v_gain, w_s, b_s, w_o):
    b, s, d = x.shape
    n = b * s
    x2 = x.reshape(n, d)
    width = w_in.shape[1] // 2
    tm = min(256, s)
    row = lambda i: (i, 0)
    gated = pl.pallas_call(
        functools.partial(_sgu_kernel, tm=tm),
        grid=(n // tm,),
        in_specs=[pl.BlockSpec((tm, d), row), _resident((1, d)), _resident((d, 2 * width)),
                  _resident((1, 2 * width)), _resident((1, width)),
                  _resident((SGU_GROUPS, SGU_BLOCK, SGU_BLOCK)), _resident((SGU_BLOCK, SGU_GROUPS))],
        out_specs=pl.BlockSpec((tm, width), row),
        out_shape=jax.ShapeDtypeStruct((n, width), BF16),
        compiler_params=_params("parallel"),
        name="sgu_gate",
    )(x2, g.reshape(1, d), w_in.astype(BF16), b_in.reshape(1, 2 * width), v_gain.reshape(1, width),
      w_s, b_s.T)
    return _proj_res(x2, gated, w_o).reshape(b, s, d)


def kernel(x, norm_mix, norm_ffn, pool_w, pool_scale, attn_w_in, attn_q_gain, attn_k_gain, attn_w_o,
           sgu_w_in, sgu_b_in, sgu_v_gain, sgu_w_s, sgu_b_s, sgu_w_o, ffn_w_up, ffn_w_down):
    b, s, d = x.shape
    depth = norm_mix.shape[0]
    for i in range(depth):
        kind, j = i % 3, i // 3
        if kind == 0:
            x = _pool_mixer(x, norm_mix[i], pool_w[j].astype(BF16), pool_scale[j])
        elif kind == 1:
            x = _sparse_attention(x, norm_mix[i], attn_w_in[j], attn_q_gain[j], attn_k_gain[j],
                                  attn_w_o[j])
        else:
            x = _spatial_gating(x, norm_mix[i], sgu_w_in[j], sgu_b_in[j], sgu_v_gain[j], sgu_w_s[j],
                                sgu_b_s[j], sgu_w_o[j])
        x = _ffn(x.reshape(b * s, d), norm_ffn[i], ffn_w_up, ffn_w_down, i).reshape(b, s, d)
    return x
```

```python
import functools

import jax
import jax.numpy as jnp
from jax import lax
from jax.experimental import pallas as pl
from jax.experimental.pallas import tpu as pltpu

EPS = 1e-6
CHUNK = 64
POOL_WINDOWS = (2, 4, 8, 16)
POOL_HALO = 16
HEAD_DIM = 128
N_KV_HEADS = 4
IDX_HEADS = 16
IDX_DIM = 64
INDEX_TOPK = 256
Q_BLOCK = 128
ROPE_THETA = 500000.0
ROT_FRACTION = 4
SGU_BLOCK = 128
SGU_GROUPS = 8
LANES = 128
MXU_COLS = 256
assert CHUNK & (CHUNK - 1) == 0
INT_MIN = -(2 ** 31)
COUNT_ROWS = 64
ONES_ROWS = 16
DEN_MIN = 2.0 ** -60
LOG2E = 1.4426950408889634
MASK_BIAS = -1e30
VMEM_LIMIT_BYTES = 60 * 1024 * 1024

F32 = jnp.float32
BF16 = jnp.bfloat16


def _params(*sem):
    return pltpu.CompilerParams(dimension_semantics=sem, vmem_limit_bytes=VMEM_LIMIT_BYTES)


def _resident(shape):
    nd = len(shape)
    return pl.BlockSpec(shape, lambda *_: (0,) * nd, pipeline_mode=pl.Buffered(1))


def _cast_once(w_ref, wbf_ref):
    @pl.when(pl.program_id(0) == 0)
    def _():
        wbf_ref[...] = w_ref[...].astype(BF16)


def _rms(xf, g):
    ms = jnp.mean(xf * xf, axis=-1, keepdims=True)
    return xf * lax.rsqrt(ms + EPS) * g


def _dot(a, b):
    return jnp.dot(a, b, preferred_element_type=F32)


def _pool_kernel(x_ref, halo_ref, g_ref, w_ref, scale_ref, o_ref, *, ts):
    i = pl.program_id(1)
    x = x_ref[0]
    g = g_ref[...]
    h = _rms(x, g)
    hh = _rms(halo_ref[0], g)
    hh = jnp.where(i > 0, hh, 0.0)
    hf = jnp.concatenate([hh, h], axis=0)
    t1 = (i * ts + lax.broadcasted_iota(jnp.int32, (ts, 1), 0) + 1).astype(F32)
    cg = x.shape[1] // len(POOL_WINDOWS)
    for gi, w in enumerate(POOL_WINDOWS):
        sl = slice(gi * cg, (gi + 1) * cg)
        s = hf[:, sl]
        k = 1
        while k < w:
            s = s + pltpu.roll(s, k, 0)
            k *= 2
        mean = s[POOL_HALO:] / jnp.minimum(t1, float(w))
        p = (mean - h[:, sl]).astype(BF16)
        y = _dot(p, w_ref[gi]) * scale_ref[:, sl]
        o_ref[0, :, sl] = x[:, sl] + y


def _pool_mixer(x, g, w_bf, scale):
    b, s, d = x.shape
    ts = min(512, s)
    hb = ts // POOL_HALO
    ng = len(POOL_WINDOWS)
    return pl.pallas_call(
        functools.partial(_pool_kernel, ts=ts),
        grid=(b, s // ts),
        in_specs=[
            pl.BlockSpec((1, ts, d), lambda bi, i: (bi, i, 0)),
            pl.BlockSpec((1, POOL_HALO, d), lambda bi, i: (bi, jnp.maximum(i * hb - 1, 0), 0)),
            _resident((1, d)),
            _resident((ng, d // ng, d // ng)),
            _resident((1, d)),
        ],
        out_specs=pl.BlockSpec((1, ts, d), lambda bi, i: (bi, i, 0)),
        out_shape=jax.ShapeDtypeStruct((b, s, d), F32),
        compiler_params=_params("parallel", "parallel"),
        name="pool_mixer",
    )(x, x, g.reshape(1, d), w_bf, scale.reshape(1, d))


def _ffn_kernel(x_ref, g_ref, wu_ref, wd_ref, o_ref, h_ref):
    j = pl.program_id(1)

    @pl.when(j == 0)
    def _():
        x = x_ref[...]
        h_ref[...] = _rms(x, g_ref[...]).astype(BF16)
        o_ref[...] = x

    u = _dot(h_ref[...], wu_ref[...].astype(BF16))
    a = jnp.square(jnp.maximum(u, 0.0)).astype(BF16)
    o_ref[...] += _dot(a, wd_ref[...].astype(BF16))


def _ffn(x2, g, w_up, w_down, layer):
    n, d = x2.shape
    f = w_up.shape[2]
    tm = min(1024, n)
    tf = 512
    return pl.pallas_call(
        _ffn_kernel,
        grid=(n // tm, f // tf),
        in_specs=[
            pl.BlockSpec((tm, d), lambda i, j: (i, 0)),
            _resident((1, d)),
            pl.BlockSpec((None, d, tf), lambda i, j: (layer, 0, j)),
            pl.BlockSpec((None, tf, d), lambda i, j: (layer, j, 0)),
        ],
        out_specs=pl.BlockSpec((tm, d), lambda i, j: (i, 0)),
        out_shape=jax.ShapeDtypeStruct((n, d), F32),
        scratch_shapes=[pltpu.VMEM((tm, d), BF16)],
        compiler_params=_params("parallel", "arbitrary"),
        name="ffn",
    )(x2, g.reshape(1, d), w_up, w_down)


def _proj_res_kernel(x_ref, a_ref, w_ref, o_ref, wbf_ref):
    _cast_once(w_ref, wbf_ref)
    o_ref[...] = x_ref[...] + _dot(a_ref[...], wbf_ref[...])


def _proj_res(x2, a_bf, w, layer):
    n, d = x2.shape
    kdim = a_bf.shape[1]
    tm = min(512, n)
    return pl.pallas_call(
        _proj_res_kernel,
        grid=(n // tm,),
        in_specs=[
            pl.BlockSpec((tm, d), lambda i: (i, 0)),
            pl.BlockSpec((tm, kdim), lambda i: (i, 0)),
            pl.BlockSpec((None, kdim, d), lambda i: (layer, 0, 0), pipeline_mode=pl.Buffered(1)),
        ],
        out_specs=pl.BlockSpec((tm, d), lambda i: (i, 0)),
        out_shape=jax.ShapeDtypeStruct((n, d), F32),
        scratch_shapes=[pltpu.VMEM((kdim, d), BF16)],
        compiler_params=_params("arbitrary"),
        name="proj_res",
    )(x2, a_bf, w)


def _rope_tables(s, width, rot):
    half = rot // 2
    inv = ROPE_THETA ** (-jnp.arange(half, dtype=F32) / half)
    ang = jnp.arange(s, dtype=F32)[:, None] * inv[None, :]
    cos, sin = jnp.cos(ang), jnp.sin(ang)
    pad = jnp.zeros((s, width - rot), F32)
    zero = jnp.zeros((s, half), F32)
    c = jnp.concatenate([cos, cos, pad + 1.0], axis=1)
    a = jnp.concatenate([-sin, zero, pad], axis=1)
    b = jnp.concatenate([zero, sin, pad], axis=1)
    rep = LANES // width
    return tuple(jnp.tile(t, (1, rep)) for t in (c, a, b))


def _rope_tables_t(s, rot):
    half = rot // 2
    inv = ROPE_THETA ** (-jnp.arange(half, dtype=F32) / half)
    ang = inv[:, None] * jnp.arange(s, dtype=F32)[None, :]
    return jnp.cos(ang), jnp.sin(ang)


def _rope(x, c, a, b, half):
    return x * c + pltpu.roll(x, LANES - half, 1) * a + pltpu.roll(x, half, 1) * b


def _rope_t(xt, cos, sin):
    half = cos.shape[0]
    x1, x2 = xt[:half], xt[half:2 * half]
    return jnp.concatenate([x1 * cos - x2 * sin, x2 * cos + x1 * sin, xt[2 * half:]], axis=0)


def _q_proj_kernel(x_ref, g_ref, w_ref, gain_ref, cos_ref, sin_ref, qt_ref, wbf_ref):
    _cast_once(w_ref, wbf_ref)
    h = _rms(x_ref[...], g_ref[...]).astype(BF16)
    tm = h.shape[0]
    gain = jnp.concatenate([gain_ref[...]] * (tm // LANES), axis=1)
    cos, sin = cos_ref[...], sin_ref[...]
    for pair in range(wbf_ref.shape[1] // MXU_COLS):
        q = _dot(h, wbf_ref[:, pair * MXU_COLS:(pair + 1) * MXU_COLS])
        for hd in range(MXU_COLS // HEAD_DIM):
            qh = q[:, hd * HEAD_DIM:(hd + 1) * HEAD_DIM].T
            qh = qh * lax.rsqrt(jnp.mean(qh * qh, axis=0, keepdims=True) + EPS) * gain
            qh = _rope_t(qh, cos, sin) * (HEAD_DIM ** -0.5 * LOG2E)
            row = pair * MXU_COLS + hd * HEAD_DIM
            qt_ref[0, row:row + HEAD_DIM, :] = qh.astype(BF16)


def _kv_proj_kernel(x_ref, g_ref, w_ref, wtail_ref, gain_ref, c_ref, a_ref, b_ref, ci_ref, ai_ref,
                    bi_ref, cosi_ref, sini_ref, k_ref, vt_ref, qit_ref, ki_ref, wit_ref, wbf_ref):
    _cast_once(w_ref, wbf_ref)
    h = _rms(x_ref[...], g_ref[...]).astype(BF16)
    y = _dot(h, wbf_ref[...])
    gain = gain_ref[...]
    c, a, b = c_ref[...], a_ref[...], b_ref[...]
    half = HEAD_DIM // ROT_FRACTION // 2
    halfi = IDX_DIM // ROT_FRACTION // 2
    nkv = N_KV_HEADS * HEAD_DIM
    for hd in range(N_KV_HEADS):
        sl = slice(hd * HEAD_DIM, (hd + 1) * HEAD_DIM)
        k_ref[:, sl] = _rope(_rms(y[:, sl], gain), c, a, b, half).astype(BF16)
        vt_ref[0, sl, :] = y[:, nkv + hd * HEAD_DIM:nkv + (hd + 1) * HEAD_DIM].T.astype(BF16)
    nqi = IDX_HEADS * IDX_DIM
    cosi, sini = cosi_ref[...], sini_ref[...]
    for hd in range(IDX_HEADS):
        col = 2 * nkv + hd * IDX_DIM
        if hd % 2 == 0:
            pair_t = y[:, col:col + LANES].T
        qh = pair_t[(hd % 2) * IDX_DIM:(hd % 2 + 1) * IDX_DIM]
        qit_ref[0, hd * IDX_DIM:(hd + 1) * IDX_DIM, :] = _rope_t(qh, cosi, sini).astype(BF16)
    kw = _dot(h, wtail_ref[...])
    ki_ref[...] = _rope(kw, ci_ref[...], ai_ref[...], bi_ref[...], halfi)[:, :IDX_DIM].astype(BF16)
    wit_ref[0] = kw.T[IDX_DIM:IDX_DIM + IDX_HEADS] * (IDX_HEADS ** -0.5 * IDX_DIM ** -0.5)


def _key_to_f32(key):
    bits = jnp.where(key < 0, key ^ jnp.int32(0x7FFFFFFF), key)
    f = pltpu.bitcast(bits, F32)
    return jnp.where(f != f, jnp.inf, f)


def _attn_kernel(qt_ref, qit_ref, wit_ref, k_ref, vt_ref, ki_ref, kgain_ref, o_ref, sc_ref, bias_ref,
                 *, tk, tkf, n_sel):
    t0 = pl.program_id(1) * Q_BLOCK
    nk = (t0 + Q_BLOCK + tk - 1) // tk
    q_pos = t0 + lax.broadcasted_iota(jnp.int32, (1, Q_BLOCK), 1)
    key_end = (lax.shift_right_logical(q_pos, CHUNK.bit_length() - 1) + 1) * CHUNK
    key_row = lax.broadcasted_iota(jnp.int32, (tk, Q_BLOCK), 0)

    qit = qit_ref[0]
    wit = wit_ref[0]
    npair = IDX_HEADS // 2
    rhs = [jnp.concatenate([qit[(2 * p) * IDX_DIM:(2 * p + 1) * IDX_DIM],
                            qit[(2 * p + 1) * IDX_DIM:(2 * p + 2) * IDX_DIM]], axis=1)
           for p in range(npair)]

    def score_tile(kt, carry):
        off = pl.multiple_of(kt * tk, tk)
        ki_t = ki_ref[0, pl.ds(off, tk), :]
        acc = jnp.zeros((tk, Q_BLOCK), F32)
        for p in range(npair):
            d = jnp.maximum(_dot(ki_t, rhs[p]), 0.0)
            acc = acc + d[:, :Q_BLOCK] * wit[2 * p:2 * p + 1, :]
            acc = acc + d[:, Q_BLOCK:] * wit[2 * p + 1:2 * p + 2, :]
        sc_ref[pl.ds(off, tk), :] = jnp.where(key_row < key_end - off, acc, -jnp.inf)
        return carry

    lax.fori_loop(0, nk, score_tile, 0)

    def count_ge(thr):
        def body(kt, c):
            off = pl.multiple_of(kt * tk, tk)
            m = jnp.where(sc_ref[pl.ds(off, tk), :] >= thr, 1.0, 0.0)
            return c + jnp.sum(m.reshape(tk // COUNT_ROWS, COUNT_ROWS, Q_BLOCK), axis=0)

        c = lax.fori_loop(0, nk, body, jnp.zeros((COUNT_ROWS, Q_BLOCK), F32))
        return jnp.sum(c, axis=0, keepdims=True)

    def bit_body(bi, key):
        cand = key + lax.shift_left(jnp.int32(1), 31 - bi)
        return jnp.where(count_ge(_key_to_f32(cand)) >= n_sel, cand, key)

    nbits = jnp.where(t0 + Q_BLOCK <= n_sel, 0, 32)
    key = lax.fori_loop(0, nbits, bit_body, jnp.full((1, Q_BLOCK), INT_MIN, jnp.int32))
    thr = jnp.where(key == INT_MIN, jnp.finfo(F32).min, _key_to_f32(key))

    def bias_tile(kt, carry):
        off = pl.multiple_of(kt * tk, tk)
        sel = sc_ref[pl.ds(off, tk), :] >= thr
        bias_ref[pl.ds(off, tk), :] = jnp.where(sel, 0.0, MASK_BIAS).astype(BF16)
        return carry

    lax.fori_loop(0, nk, bias_tile, 0)

    nkf = (t0 + Q_BLOCK + tkf - 1) // tkf

    def mask_tile(kt, carry):
        off = pl.multiple_of(kt * tk, tk)
        bias_ref[pl.ds(off, tk), :] = jnp.full((tk, Q_BLOCK), MASK_BIAS, BF16)
        return carry

    lax.fori_loop(nk, nkf * (tkf // tk), mask_tile, 0)

    qt = qt_ref[0]
    rep = qt.shape[0] // HEAD_DIM // N_KV_HEADS
    cols = rep * Q_BLOCK
    eye = (lax.broadcasted_iota(jnp.int32, (Q_BLOCK, Q_BLOCK), 0)
           == lax.broadcasted_iota(jnp.int32, (Q_BLOCK, Q_BLOCK), 1))
    eye = jnp.where(eye, 1.0, 0.0).astype(BF16)
    ones = jnp.ones((ONES_ROWS, tkf), BF16)
    gsls = [slice(g * HEAD_DIM, (g + 1) * HEAD_DIM) for g in range(N_KV_HEADS)]
    qaugs = []
    for g in range(N_KV_HEADS):
        qg = jnp.concatenate(
            [qt[(g * rep + r) * HEAD_DIM:(g * rep + r + 1) * HEAD_DIM] for r in range(rep)], axis=1)
        qaugs.append(jnp.concatenate([qg, jnp.concatenate([eye] * rep, axis=1)], axis=0))

    def flash(shift):
        def body(kt, accs):
            off = pl.multiple_of(kt * tkf, tkf)
            bias_t = bias_ref[pl.ds(off, tkf), :]
            out = []
            for g in range(N_KV_HEADS):
                kaug = jnp.concatenate([k_ref[0, pl.ds(off, tkf), gsls[g]], bias_t], axis=1)
                vaug = jnp.concatenate([vt_ref[0, gsls[g], pl.ds(off, tkf)], ones], axis=0)
                p = jnp.exp2(_dot(kaug, qaugs[g]) - shift[g]).astype(BF16)
                out.append(accs[g] + _dot(vaug, p))
            return tuple(out)

        zero = jnp.zeros((HEAD_DIM + ONES_ROWS, cols), F32)
        return lax.fori_loop(0, nkf, body, (zero,) * N_KV_HEADS)

    def column_max():
        def body(kt, ms):
            off = pl.multiple_of(kt * tkf, tkf)
            bias_t = bias_ref[pl.ds(off, tkf), :]
            out = []
            for g in range(N_KV_HEADS):
                kaug = jnp.concatenate([k_ref[0, pl.ds(off, tkf), gsls[g]], bias_t], axis=1)
                out.append(jnp.maximum(ms[g], jnp.max(_dot(kaug, qaugs[g]), axis=0, keepdims=True)))
            return tuple(out)

        return lax.fori_loop(0, nkf, body, (jnp.full((1, cols), MASK_BIAS, F32),) * N_KV_HEADS)

    def write(accs):
        for g in range(N_KV_HEADS):
            o = accs[g][:HEAD_DIM] / accs[g][HEAD_DIM:HEAD_DIM + 1]
            for r in range(rep):
                hsl = slice((g * rep + r) * HEAD_DIM, (g * rep + r + 1) * HEAD_DIM)
                o_ref[0, :, hsl] = o[:, r * Q_BLOCK:(r + 1) * Q_BLOCK].T.astype(BF16)

    kmax = (HEAD_DIM ** 0.5) * jnp.max(jnp.abs(kgain_ref[...]), axis=1, keepdims=True)
    qf = qt.astype(F32)
    qn = [jnp.sqrt(jnp.sum(jnp.square(qf[hd * HEAD_DIM:(hd + 1) * HEAD_DIM]), axis=0, keepdims=True))
          for hd in range(rep * N_KV_HEADS)]
    bound = [jnp.concatenate(qn[g * rep:(g + 1) * rep], axis=1) * kmax for g in range(N_KV_HEADS)]
    accs = flash(bound)
    den = jnp.concatenate([acc[HEAD_DIM:HEAD_DIM + 1] for acc in accs], axis=1)
    safe = jnp.logical_and(jnp.min(den) >= DEN_MIN, jnp.max(den) <= 1.0 / DEN_MIN)

    @pl.when(safe)
    def _():
        write(accs)

    @pl.when(jnp.logical_not(safe))
    def _():
        write(flash(column_max()))


def _sparse_attention(x, g, w_in, q_gain, k_gain, w_o, layer):
    b, s, d = x.shape
    n = b * s
    x2 = x.reshape(n, d)
    nq = d
    nkv = N_KV_HEADS * HEAD_DIM
    nqi = IDX_HEADS * IDX_DIM
    nmid = 2 * nkv + nqi
    assert nmid == nq
    pad = LANES - IDX_DIM - IDX_HEADS
    w_tail = jnp.pad(w_in[layer, :, nq + nmid:], ((0, 0), (0, pad))).astype(BF16)
    wblock = lambda col: pl.BlockSpec((None, d, nq), lambda i: (layer, 0, col),
                                      pipeline_mode=pl.Buffered(1))
    c, a, bt = _rope_tables(s, HEAD_DIM, HEAD_DIM // ROT_FRACTION)
    ci, ai, bi = _rope_tables(s, IDX_DIM, IDX_DIM // ROT_FRACTION)
    cos_t, sin_t = _rope_tables_t(s, HEAD_DIM // ROT_FRACTION)
    cosi_t, sini_t = _rope_tables_t(s, IDX_DIM // ROT_FRACTION)
    q_gain_b = jnp.broadcast_to(q_gain[:, None], (HEAD_DIM, LANES))

    tm = min(512, s)
    nt = s // tm
    row = lambda i: (i, 0)
    pos = lambda i: (i % nt, 0)
    tcol = lambda i: (i // nt, 0, i % nt)
    tab = pl.BlockSpec((tm, LANES), pos)
    tab_t = lambda half: pl.BlockSpec((half, tm), lambda i: (0, i % nt))
    half = HEAD_DIM // ROT_FRACTION // 2
    halfi = IDX_DIM // ROT_FRACTION // 2
    qt = pl.pallas_call(
        _q_proj_kernel,
        grid=(n // tm,),
        in_specs=[pl.BlockSpec((tm, d), row), _resident((1, d)), wblock(0),
                  _resident((HEAD_DIM, LANES)), tab_t(half), tab_t(half)],
        out_specs=pl.BlockSpec((1, nq, tm), tcol),
        out_shape=jax.ShapeDtypeStruct((b, nq, s), BF16),
        scratch_shapes=[pltpu.VMEM((d, nq), BF16)],
        compiler_params=_params("arbitrary"),
        name="attn_q_proj",
    )(x2, g.reshape(1, d), w_in, q_gain_b, cos_t, sin_t)

    k, vt, qit, ki, wit = pl.pallas_call(
        _kv_proj_kernel,
        grid=(n // tm,),
        in_specs=[pl.BlockSpec((tm, d), row), _resident((1, d)), wblock(1), _resident((d, LANES)),
                  _resident((1, HEAD_DIM)), tab, tab, tab, tab, tab, tab, tab_t(halfi), tab_t(halfi)],
        out_specs=[pl.BlockSpec((tm, nkv), row), pl.BlockSpec((1, nkv, tm), tcol),
                   pl.BlockSpec((1, nqi, tm), tcol), pl.BlockSpec((tm, IDX_DIM), row),
                   pl.BlockSpec((1, IDX_HEADS, tm), tcol)],
        out_shape=[jax.ShapeDtypeStruct((n, nkv), BF16), jax.ShapeDtypeStruct((b, nkv, s), BF16),
                   jax.ShapeDtypeStruct((b, nqi, s), BF16), jax.ShapeDtypeStruct((n, IDX_DIM), BF16),
                   jax.ShapeDtypeStruct((b, IDX_HEADS, s), F32)],
        scratch_shapes=[pltpu.VMEM((d, nmid), BF16)],
        compiler_params=_params("arbitrary"),
        name="attn_kv_proj",
    )(x2, g.reshape(1, d), w_in, w_tail, k_gain.reshape(1, HEAD_DIM), c, a, bt, ci, ai, bi, cosi_t,
      sini_t)

    n_sel = min(INDEX_TOPK, s // 4)
    tk = min(512, s)
    qcol = lambda bi_, qb: (bi_, 0, qb)
    full = lambda bi_, qb: (bi_, 0, 0)
    o = pl.pallas_call(
        functools.partial(_attn_kernel, tk=tk, tkf=min(1024, s), n_sel=n_sel),
        grid=(b, s // Q_BLOCK),
        in_specs=[pl.BlockSpec((1, nq, Q_BLOCK), qcol), pl.BlockSpec((1, nqi, Q_BLOCK), qcol),
                  pl.BlockSpec((1, IDX_HEADS, Q_BLOCK), qcol), pl.BlockSpec((1, s, nkv), full),
                  pl.BlockSpec((1, nkv, s), full), pl.BlockSpec((1, s, IDX_DIM), full),
                  pl.BlockSpec((1, HEAD_DIM), lambda bi_, qb: (0, 0))],
        out_specs=pl.BlockSpec((1, Q_BLOCK, nq), lambda bi_, qb: (bi_, qb, 0)),
        out_shape=jax.ShapeDtypeStruct((b, s, nq), BF16),
        scratch_shapes=[pltpu.VMEM((s, Q_BLOCK), F32), pltpu.VMEM((s, Q_BLOCK), BF16)],
        compiler_params=_params("parallel", "arbitrary"),
        name="sparse_attn",
    )(qt, qit, wit, k.reshape(b, s, nkv), vt, ki.reshape(b, s, IDX_DIM), k_gain.reshape(1, HEAD_DIM))
    return _proj_res(x2, o.reshape(n, nq), w_o, layer).reshape(b, s, d)


def _sgu_kernel(x_ref, g_ref, w_ref, b_ref, vg_ref, ws_ref, bs_ref, o_ref, *, tm, sub):
    ii = lax.broadcasted_iota(jnp.int32, (SGU_BLOCK, SGU_BLOCK), 0) // CHUNK
    jj = lax.broadcasted_iota(jnp.int32, (SGU_BLOCK, SGU_BLOCK), 1) // CHUNK
    causal = jj <= ii
    ws = [jnp.where(causal, ws_ref[gi], 0.0).astype(BF16) for gi in range(SGU_GROUPS)]
    bs = bs_ref[...]
    width = w_ref.shape[1] // 2
    gd = width // SGU_GROUPS
    for st in range(tm // sub):
        r0 = st * sub
        h = _rms(x_ref[r0:r0 + sub, :], g_ref[...]).astype(BF16)
        z = _dot(h, w_ref[...]) + b_ref[...]
        z = 0.5 * z * (1.0 + lax.erf(z * (2.0 ** -0.5)))
        u = z[:, :width]
        v = _rms(z[:, width:], vg_ref[...]).astype(BF16)
        for gi in range(SGU_GROUPS):
            cs = slice(gi * gd, (gi + 1) * gd)
            for nb in range(sub // SGU_BLOCK):
                rs = slice(nb * SGU_BLOCK, (nb + 1) * SGU_BLOCK)
                mixed = _dot(ws[gi], v[rs, cs]) + bs[:, gi:gi + 1]
                o_ref[r0 + nb * SGU_BLOCK:r0 + (nb + 1) * SGU_BLOCK, cs] = (u[rs, cs] * mixed).astype(BF16)


def _spatial_gating(x, g, w_in, b_in, v_gain, w_s, b_s, w_o, layer):
    b, s, d = x.shape
    n = b * s
    x2 = x.reshape(n, d)
    width = w_in.shape[1] // 2
    tm = min(512, s)
    row = lambda i: (i, 0)
    gated = pl.pallas_call(
        functools.partial(_sgu_kernel, tm=tm, sub=min(256, tm)),
        grid=(n // tm,),
        in_specs=[pl.BlockSpec((tm, d), row), _resident((1, d)), _resident((d, 2 * width)),
                  _resident((1, 2 * width)), _resident((1, width)),
                  _resident((SGU_GROUPS, SGU_BLOCK, SGU_BLOCK)), _resident((SGU_BLOCK, SGU_GROUPS))],
        out_specs=pl.BlockSpec((tm, width), row),
        out_shape=jax.ShapeDtypeStruct((n, width), BF16),
        compiler_params=_params("parallel"),
        name="sgu_gate",
    )(x2, g.reshape(1, d), w_in.astype(BF16), b_in.reshape(1, 2 * width), v_gain.reshape(1, width),
      w_s, b_s.T)
    return _proj_res(x2, gated, w_o, layer).reshape(b, s, d)


def kernel(x, norm_mix, norm_ffn, pool_w, pool_scale, attn_w_in, attn_q_gain, attn_k_gain, attn_w_o,
           sgu_w_in, sgu_b_in, sgu_v_gain, sgu_w_s, sgu_b_s, sgu_w_o, ffn_w_up, ffn_w_down):
    b, s, d = x.shape
    depth = norm_mix.shape[0]
    for i in range(depth):
        kind, j = i % 3, i // 3
        if kind == 0:
            x = _pool_mixer(x, norm_mix[i], pool_w[j].astype(BF16), pool_scale[j])
        elif kind == 1:
            x = _sparse_attention(x, norm_mix[i], attn_w_in, attn_q_gain[j], attn_k_gain[j],
                                  attn_w_o, j)
        else:
            x = _spatial_gating(x, norm_mix[i], sgu_w_in[j], sgu_b_in[j], sgu_v_gain[j], sgu_w_s[j],
                                sgu_b_s[j], sgu_w_o, j)
        x = _ffn(x.reshape(b * s, d), norm_ffn[i], ffn_w_up, ffn_w_down, i).reshape(b, s, d)
    return x
```

```python
import functools

import jax
import jax.numpy as jnp
from jax import lax
from jax.experimental import pallas as pl
from jax.experimental.pallas import tpu as pltpu

EPS = 1e-6
CHUNK = 64
POOL_WINDOWS = (2, 4, 8, 16)
POOL_HALO = 16
HEAD_DIM = 128
N_KV_HEADS = 4
IDX_HEADS = 16
IDX_DIM = 64
INDEX_TOPK = 256
Q_BLOCK = 128
ROPE_THETA = 500000.0
ROT_FRACTION = 4
SGU_BLOCK = 128
SGU_GROUPS = 8
LANES = 128
MXU_COLS = 256
assert CHUNK & (CHUNK - 1) == 0
INT_MIN = -(2 ** 31)
COUNT_ROWS = 64
ONES_ROWS = 16
DEN_MIN = 2.0 ** -60
LOG2E = 1.4426950408889634
MASK_BIAS = -1e30
VMEM_LIMIT_BYTES = 60 * 1024 * 1024

F32 = jnp.float32
BF16 = jnp.bfloat16


def _params(*sem):
    return pltpu.CompilerParams(dimension_semantics=sem, vmem_limit_bytes=VMEM_LIMIT_BYTES)


def _resident(shape):
    nd = len(shape)
    return pl.BlockSpec(shape, lambda *_: (0,) * nd, pipeline_mode=pl.Buffered(1))


def _cast_once(w_ref, wbf_ref, transpose=False):
    @pl.when(pl.program_id(0) == 0)
    def _():
        if transpose:
            for r in range(0, w_ref.shape[0], LANES):
                wbf_ref[:, r:r + LANES] = w_ref[r:r + LANES, :].T.astype(BF16)
        else:
            wbf_ref[...] = w_ref[...].astype(BF16)


def _rms(xf, g):
    ms = jnp.mean(xf * xf, axis=-1, keepdims=True)
    return xf * lax.rsqrt(ms + EPS) * g


def _dot(a, b):
    return jnp.dot(a, b, preferred_element_type=F32)


def _pool_kernel(x_ref, halo_ref, g_ref, w_ref, scale_ref, o_ref, *, ts):
    i = pl.program_id(1)
    x = x_ref[0]
    g = g_ref[...]
    h = _rms(x, g)
    hh = _rms(halo_ref[0], g)
    hh = jnp.where(i > 0, hh, 0.0)
    hf = jnp.concatenate([hh, h], axis=0)
    t1 = (i * ts + lax.broadcasted_iota(jnp.int32, (ts, 1), 0) + 1).astype(F32)
    cg = x.shape[1] // len(POOL_WINDOWS)
    for gi, w in enumerate(POOL_WINDOWS):
        sl = slice(gi * cg, (gi + 1) * cg)
        s = hf[:, sl]
        k = 1
        while k < w:
            s = s + pltpu.roll(s, k, 0)
            k *= 2
        mean = s[POOL_HALO:] / jnp.minimum(t1, float(w))
        p = (mean - h[:, sl]).astype(BF16)
        y = _dot(p, w_ref[gi]) * scale_ref[:, sl]
        o_ref[0, :, sl] = x[:, sl] + y


def _pool_mixer(x, g, w_bf, scale):
    b, s, d = x.shape
    ts = min(512, s)
    hb = ts // POOL_HALO
    ng = len(POOL_WINDOWS)
    return pl.pallas_call(
        functools.partial(_pool_kernel, ts=ts),
        grid=(b, s // ts),
        in_specs=[
            pl.BlockSpec((1, ts, d), lambda bi, i: (bi, i, 0)),
            pl.BlockSpec((1, POOL_HALO, d), lambda bi, i: (bi, jnp.maximum(i * hb - 1, 0), 0)),
            _resident((1, d)),
            _resident((ng, d // ng, d // ng)),
            _resident((1, d)),
        ],
        out_specs=pl.BlockSpec((1, ts, d), lambda bi, i: (bi, i, 0)),
        out_shape=jax.ShapeDtypeStruct((b, s, d), F32),
        compiler_params=_params("parallel", "parallel"),
        name="pool_mixer",
    )(x, x, g.reshape(1, d), w_bf, scale.reshape(1, d))


def _ffn_kernel(x_ref, g_ref, wu_ref, wd_ref, o_ref, h_ref):
    j = pl.program_id(1)

    @pl.when(j == 0)
    def _():
        x = x_ref[...]
        h_ref[...] = _rms(x, g_ref[...]).astype(BF16)
        o_ref[...] = x

    u = _dot(h_ref[...], wu_ref[...].astype(BF16))
    a = jnp.square(jnp.maximum(u, 0.0)).astype(BF16)
    o_ref[...] += _dot(a, wd_ref[...].astype(BF16))


def _ffn(x2, g, w_up, w_down, layer):
    n, d = x2.shape
    f = w_up.shape[2]
    tm = min(1024, n)
    tf = 512
    return pl.pallas_call(
        _ffn_kernel,
        grid=(n // tm, f // tf),
        in_specs=[
            pl.BlockSpec((tm, d), lambda i, j: (i, 0)),
            _resident((1, d)),
            pl.BlockSpec((None, d, tf), lambda i, j: (layer, 0, j)),
            pl.BlockSpec((None, tf, d), lambda i, j: (layer, j, 0)),
        ],
        out_specs=pl.BlockSpec((tm, d), lambda i, j: (i, 0)),
        out_shape=jax.ShapeDtypeStruct((n, d), F32),
        scratch_shapes=[pltpu.VMEM((tm, d), BF16)],
        compiler_params=_params("parallel", "arbitrary"),
        name="ffn",
    )(x2, g.reshape(1, d), w_up, w_down)


def _proj_res_kernel(x_ref, a_ref, w_ref, o_ref, wbf_ref):
    _cast_once(w_ref, wbf_ref)
    o_ref[...] = x_ref[...] + _dot(a_ref[...], wbf_ref[...])


def _proj_res(x2, a_bf, w, layer):
    n, d = x2.shape
    kdim = a_bf.shape[1]
    tm = min(512, n)
    return pl.pallas_call(
        _proj_res_kernel,
        grid=(n // tm,),
        in_specs=[
            pl.BlockSpec((tm, d), lambda i: (i, 0)),
            pl.BlockSpec((tm, kdim), lambda i: (i, 0)),
            pl.BlockSpec((None, kdim, d), lambda i: (layer, 0, 0), pipeline_mode=pl.Buffered(1)),
        ],
        out_specs=pl.BlockSpec((tm, d), lambda i: (i, 0)),
        out_shape=jax.ShapeDtypeStruct((n, d), F32),
        scratch_shapes=[pltpu.VMEM((kdim, d), BF16)],
        compiler_params=_params("arbitrary"),
        name="proj_res",
    )(x2, a_bf, w)


def _rope_tables(s, width, rot):
    half = rot // 2
    inv = ROPE_THETA ** (-jnp.arange(half, dtype=F32) / half)
    ang = jnp.arange(s, dtype=F32)[:, None] * inv[None, :]
    cos, sin = jnp.cos(ang), jnp.sin(ang)
    pad = jnp.zeros((s, width - rot), F32)
    zero = jnp.zeros((s, half), F32)
    c = jnp.concatenate([cos, cos, pad + 1.0], axis=1)
    a = jnp.concatenate([-sin, zero, pad], axis=1)
    b = jnp.concatenate([zero, sin, pad], axis=1)
    rep = LANES // width
    return tuple(jnp.tile(t, (1, rep)) for t in (c, a, b))


def _rope_tables_t(s, rot):
    half = rot // 2
    inv = ROPE_THETA ** (-jnp.arange(half, dtype=F32) / half)
    ang = inv[:, None] * jnp.arange(s, dtype=F32)[None, :]
    return jnp.cos(ang), jnp.sin(ang)


def _rope(x, c, a, b, half):
    return x * c + pltpu.roll(x, LANES - half, 1) * a + pltpu.roll(x, half, 1) * b


def _rope_t(xt, cos, sin):
    half = cos.shape[0]
    x1, x2 = xt[:half], xt[half:2 * half]
    return jnp.concatenate([x1 * cos - x2 * sin, x2 * cos + x1 * sin, xt[2 * half:]], axis=0)


def _q_proj_kernel(x_ref, g_ref, w_ref, gain_ref, cos_ref, sin_ref, qt_ref, wbf_ref):
    _cast_once(w_ref, wbf_ref, transpose=True)
    h = _rms(x_ref[...], g_ref[...]).astype(BF16)
    tm = h.shape[0]
    gain = jnp.concatenate([gain_ref[...]] * (tm // LANES), axis=1)
    cos, sin = cos_ref[...], sin_ref[...]
    for pair in range(wbf_ref.shape[1] // MXU_COLS):
        q = _dot(h, wbf_ref[:, pair * MXU_COLS:(pair + 1) * MXU_COLS])
        for hd in range(MXU_COLS // HEAD_DIM):
            qh = q[:, hd * HEAD_DIM:(hd + 1) * HEAD_DIM].T
            qh = qh * lax.rsqrt(jnp.mean(qh * qh, axis=0, keepdims=True) + EPS) * gain
            qh = _rope_t(qh, cos, sin) * (HEAD_DIM ** -0.5 * LOG2E)
            row = pair * MXU_COLS + hd * HEAD_DIM
            qt_ref[0, row:row + HEAD_DIM, :] = qh.astype(BF16)


def _kv_proj_kernel(x_ref, g_ref, w_ref, wtail_ref, gain_ref, c_ref, a_ref, b_ref, ci_ref, ai_ref,
                    bi_ref, cosi_ref, sini_ref, k_ref, vt_ref, qit_ref, ki_ref, wit_ref, wbf_ref):
    _cast_once(w_ref, wbf_ref, transpose=True)
    h = _rms(x_ref[...], g_ref[...]).astype(BF16)
    y = _dot(h, wbf_ref[...])
    gain = gain_ref[...]
    c, a, b = c_ref[...], a_ref[...], b_ref[...]
    half = HEAD_DIM // ROT_FRACTION // 2
    halfi = IDX_DIM // ROT_FRACTION // 2
    nkv = N_KV_HEADS * HEAD_DIM
    for hd in range(N_KV_HEADS):
        sl = slice(hd * HEAD_DIM, (hd + 1) * HEAD_DIM)
        k_ref[:, sl] = _rope(_rms(y[:, sl], gain), c, a, b, half).astype(BF16)
        vt_ref[0, sl, :] = y[:, nkv + hd * HEAD_DIM:nkv + (hd + 1) * HEAD_DIM].T.astype(BF16)
    nqi = IDX_HEADS * IDX_DIM
    cosi, sini = cosi_ref[...], sini_ref[...]
    for hd in range(IDX_HEADS):
        col = 2 * nkv + hd * IDX_DIM
        if hd % 2 == 0:
            pair_t = y[:, col:col + LANES].T
        qh = pair_t[(hd % 2) * IDX_DIM:(hd % 2 + 1) * IDX_DIM]
        qit_ref[0, hd * IDX_DIM:(hd + 1) * IDX_DIM, :] = _rope_t(qh, cosi, sini).astype(BF16)
    kw = _dot(h, wtail_ref[...])
    ki_ref[...] = _rope(kw, ci_ref[...], ai_ref[...], bi_ref[...], halfi)[:, :IDX_DIM].astype(BF16)
    wit_ref[0] = kw.T[IDX_DIM:IDX_DIM + IDX_HEADS] * (IDX_HEADS ** -0.5 * IDX_DIM ** -0.5)


def _key_to_f32(key):
    bits = jnp.where(key < 0, key ^ jnp.int32(0x7FFFFFFF), key)
    f = pltpu.bitcast(bits, F32)
    return jnp.where(f != f, jnp.inf, f)


def _attn_kernel(qt_ref, qit_ref, wit_ref, k_ref, vt_ref, ki_ref, kgain_ref, o_ref, sc_ref, bias_ref,
                 *, tk, tkf, n_sel):
    t0 = pl.program_id(1) * Q_BLOCK
    nk = (t0 + Q_BLOCK + tk - 1) // tk
    q_pos = t0 + lax.broadcasted_iota(jnp.int32, (1, Q_BLOCK), 1)
    key_end = (lax.shift_right_logical(q_pos, CHUNK.bit_length() - 1) + 1) * CHUNK
    key_row = lax.broadcasted_iota(jnp.int32, (tk, Q_BLOCK), 0)

    qit = qit_ref[0]
    wit = wit_ref[0]
    npair = IDX_HEADS // 2
    rhs = [jnp.concatenate([qit[(2 * p) * IDX_DIM:(2 * p + 1) * IDX_DIM],
                            qit[(2 * p + 1) * IDX_DIM:(2 * p + 2) * IDX_DIM]], axis=1)
           for p in range(npair)]

    def score_tile(kt, carry):
        off = pl.multiple_of(kt * tk, tk)
        ki_t = ki_ref[0, pl.ds(off, tk), :]
        acc = jnp.zeros((tk, Q_BLOCK), F32)
        for p in range(npair):
            d = jnp.maximum(_dot(ki_t, rhs[p]), 0.0)
            acc = acc + d[:, :Q_BLOCK] * wit[2 * p:2 * p + 1, :]
            acc = acc + d[:, Q_BLOCK:] * wit[2 * p + 1:2 * p + 2, :]
        sc_ref[pl.ds(off, tk), :] = jnp.where(key_row < key_end - off, acc, -jnp.inf)
        return carry

    lax.fori_loop(0, nk, score_tile, 0)

    def count(pred):
        def body(kt, c):
            off = pl.multiple_of(kt * tk, tk)
            m = jnp.where(pred(sc_ref[pl.ds(off, tk), :], key_row + off), 1.0, 0.0)
            return c + jnp.sum(m.reshape(tk // COUNT_ROWS, COUNT_ROWS, Q_BLOCK), axis=0)

        c = lax.fori_loop(0, nk, body, jnp.zeros((COUNT_ROWS, Q_BLOCK), F32))
        return jnp.sum(c, axis=0, keepdims=True)

    def bit_body(bi, carry):
        key, cnt = carry
        cand = key + lax.shift_left(jnp.int32(1), 31 - bi)
        cand_f = _key_to_f32(cand)
        c = count(lambda x, pos: x >= cand_f)
        ok = c >= n_sel
        return jnp.where(ok, cand, key), jnp.where(ok, c, cnt)

    nbits = jnp.where(t0 + Q_BLOCK <= n_sel, 0, 32)
    key, cnt = lax.fori_loop(0, nbits, bit_body, (jnp.full((1, Q_BLOCK), INT_MIN, jnp.int32),
                                                  jnp.zeros((1, Q_BLOCK), F32)))
    thr = jnp.where(key == INT_MIN, jnp.finfo(F32).min, _key_to_f32(key))

    pos_bits = (sc_ref.shape[0] - 1).bit_length() + 1

    def tie_bound():
        need = n_sel - count(lambda x, pos: x > thr)

        def body(bi, end):
            cand = end + lax.shift_left(jnp.int32(1), pos_bits - 1 - bi)
            c = count(lambda x, pos: jnp.logical_and(x == thr, pos < cand))
            return jnp.where(c <= need, cand, end)

        return lax.fori_loop(0, pos_bits, body, jnp.zeros((1, Q_BLOCK), jnp.int32))

    pos_end = lax.cond(jnp.max(cnt) > n_sel, tie_bound,
                       lambda: jnp.full((1, Q_BLOCK), 2 ** pos_bits, jnp.int32))

    def bias_tile(kt, carry):
        off = pl.multiple_of(kt * tk, tk)
        x = sc_ref[pl.ds(off, tk), :]
        sel = jnp.logical_or(x > thr, jnp.logical_and(x == thr, key_row + off < pos_end))
        bias_ref[pl.ds(off, tk), :] = jnp.where(sel, 0.0, MASK_BIAS).astype(BF16)
        return carry

    lax.fori_loop(0, nk, bias_tile, 0)

    nkf = (t0 + Q_BLOCK + tkf - 1) // tkf

    def mask_tile(kt, carry):
        off = pl.multiple_of(kt * tk, tk)
        bias_ref[pl.ds(off, tk), :] = jnp.full((tk, Q_BLOCK), MASK_BIAS, BF16)
        return carry

    lax.fori_loop(nk, nkf * (tkf // tk), mask_tile, 0)

    qt = qt_ref[0]
    rep = qt.shape[0] // HEAD_DIM // N_KV_HEADS
    cols = rep * Q_BLOCK
    eye = (lax.broadcasted_iota(jnp.int32, (Q_BLOCK, Q_BLOCK), 0)
           == lax.broadcasted_iota(jnp.int32, (Q_BLOCK, Q_BLOCK), 1))
    eye = jnp.where(eye, 1.0, 0.0).astype(BF16)
    ones = jnp.ones((ONES_ROWS, tkf), BF16)
    gsls = [slice(g * HEAD_DIM, (g + 1) * HEAD_DIM) for g in range(N_KV_HEADS)]
    qaugs = []
    for g in range(N_KV_HEADS):
        qg = jnp.concatenate(
            [qt[(g * rep + r) * HEAD_DIM:(g * rep + r + 1) * HEAD_DIM] for r in range(rep)], axis=1)
        qaugs.append(jnp.concatenate([qg, jnp.concatenate([eye] * rep, axis=1)], axis=0))

    def flash(shift):
        def body(kt, accs):
            off = pl.multiple_of(kt * tkf, tkf)
            bias_t = bias_ref[pl.ds(off, tkf), :]
            out = []
            for g in range(N_KV_HEADS):
                kaug = jnp.concatenate([k_ref[0, pl.ds(off, tkf), gsls[g]], bias_t], axis=1)
                vaug = jnp.concatenate([vt_ref[0, gsls[g], pl.ds(off, tkf)], ones], axis=0)
                p = jnp.exp2(_dot(kaug, qaugs[g]) - shift[g]).astype(BF16)
                out.append(accs[g] + _dot(vaug, p))
            return tuple(out)

        zero = jnp.zeros((HEAD_DIM + ONES_ROWS, cols), F32)
        return lax.fori_loop(0, nkf, body, (zero,) * N_KV_HEADS)

    def column_max():
        def body(kt, ms):
            off = pl.multiple_of(kt * tkf, tkf)
            bias_t = bias_ref[pl.ds(off, tkf), :]
            out = []
            for g in range(N_KV_HEADS):
                kaug = jnp.concatenate([k_ref[0, pl.ds(off, tkf), gsls[g]], bias_t], axis=1)
                out.append(jnp.maximum(ms[g], jnp.max(_dot(kaug, qaugs[g]), axis=0, keepdims=True)))
            return tuple(out)

        return lax.fori_loop(0, nkf, body, (jnp.full((1, cols), MASK_BIAS, F32),) * N_KV_HEADS)

    def write(accs):
        for g in range(N_KV_HEADS):
            o = accs[g][:HEAD_DIM] / accs[g][HEAD_DIM:HEAD_DIM + 1]
            for r in range(rep):
                hsl = slice((g * rep + r) * HEAD_DIM, (g * rep + r + 1) * HEAD_DIM)
                o_ref[0, :, hsl] = o[:, r * Q_BLOCK:(r + 1) * Q_BLOCK].T.astype(BF16)

    kmax = (HEAD_DIM ** 0.5) * jnp.max(jnp.abs(kgain_ref[...]), axis=1, keepdims=True)
    qf = qt.astype(F32)
    qn = [jnp.sqrt(jnp.sum(jnp.square(qf[hd * HEAD_DIM:(hd + 1) * HEAD_DIM]), axis=0, keepdims=True))
          for hd in range(rep * N_KV_HEADS)]
    bound = [jnp.concatenate(qn[g * rep:(g + 1) * rep], axis=1) * kmax for g in range(N_KV_HEADS)]
    accs = flash(bound)
    den = jnp.concatenate([acc[HEAD_DIM:HEAD_DIM + 1] for acc in accs], axis=1)
    safe = jnp.logical_and(jnp.min(den) >= DEN_MIN, jnp.max(den) <= 1.0 / DEN_MIN)

    @pl.when(safe)
    def _():
        write(accs)

    @pl.when(jnp.logical_not(safe))
    def _():
        write(flash(column_max()))


def _sparse_attention(x, g, w_in, q_gain, k_gain, w_o, layer):
    b, s, d = x.shape
    n = b * s
    x2 = x.reshape(n, d)
    nq = d
    nkv = N_KV_HEADS * HEAD_DIM
    nqi = IDX_HEADS * IDX_DIM
    nmid = 2 * nkv + nqi
    assert nmid == nq
    pad = LANES - IDX_DIM - IDX_HEADS
    w_tail = jnp.pad(w_in[layer, :, nq + nmid:], ((0, 0), (0, pad))).astype(BF16)
    w_in_t = jnp.swapaxes(w_in, 1, 2)
    wblock = lambda blk: pl.BlockSpec((None, nq, d), lambda i: (layer, blk, 0),
                                      pipeline_mode=pl.Buffered(1))
    c, a, bt = _rope_tables(s, HEAD_DIM, HEAD_DIM // ROT_FRACTION)
    ci, ai, bi = _rope_tables(s, IDX_DIM, IDX_DIM // ROT_FRACTION)
    cos_t, sin_t = _rope_tables_t(s, HEAD_DIM // ROT_FRACTION)
    cosi_t, sini_t = _rope_tables_t(s, IDX_DIM // ROT_FRACTION)
    q_gain_b = jnp.broadcast_to(q_gain[:, None], (HEAD_DIM, LANES))

    tm = min(512, s)
    nt = s // tm
    row = lambda i: (i, 0)
    pos = lambda i: (i % nt, 0)
    tcol = lambda i: (i // nt, 0, i % nt)
    tab = pl.BlockSpec((tm, LANES), pos)
    tab_t = lambda half: pl.BlockSpec((half, tm), lambda i: (0, i % nt))
    half = HEAD_DIM // ROT_FRACTION // 2
    halfi = IDX_DIM // ROT_FRACTION // 2
    qt = pl.pallas_call(
        _q_proj_kernel,
        grid=(n // tm,),
        in_specs=[pl.BlockSpec((tm, d), row), _resident((1, d)), wblock(0),
                  _resident((HEAD_DIM, LANES)), tab_t(half), tab_t(half)],
        out_specs=pl.BlockSpec((1, nq, tm), tcol),
        out_shape=jax.ShapeDtypeStruct((b, nq, s), BF16),
        scratch_shapes=[pltpu.VMEM((d, nq), BF16)],
        compiler_params=_params("arbitrary"),
        name="attn_q_proj",
    )(x2, g.reshape(1, d), w_in_t, q_gain_b, cos_t, sin_t)

    k, vt, qit, ki, wit = pl.pallas_call(
        _kv_proj_kernel,
        grid=(n // tm,),
        in_specs=[pl.BlockSpec((tm, d), row), _resident((1, d)), wblock(1), _resident((d, LANES)),
                  _resident((1, HEAD_DIM)), tab, tab, tab, tab, tab, tab, tab_t(halfi), tab_t(halfi)],
        out_specs=[pl.BlockSpec((tm, nkv), row), pl.BlockSpec((1, nkv, tm), tcol),
                   pl.BlockSpec((1, nqi, tm), tcol), pl.BlockSpec((tm, IDX_DIM), row),
                   pl.BlockSpec((1, IDX_HEADS, tm), tcol)],
        out_shape=[jax.ShapeDtypeStruct((n, nkv), BF16), jax.ShapeDtypeStruct((b, nkv, s), BF16),
                   jax.ShapeDtypeStruct((b, nqi, s), BF16), jax.ShapeDtypeStruct((n, IDX_DIM), BF16),
                   jax.ShapeDtypeStruct((b, IDX_HEADS, s), F32)],
        scratch_shapes=[pltpu.VMEM((d, nmid), BF16)],
        compiler_params=_params("arbitrary"),
        name="attn_kv_proj",
    )(x2, g.reshape(1, d), w_in_t, w_tail, k_gain.reshape(1, HEAD_DIM), c, a, bt, ci, ai, bi, cosi_t,
      sini_t)

    n_sel = min(INDEX_TOPK, s // 4)
    tk = min(512, s)
    qcol = lambda bi_, qb: (bi_, 0, qb)
    full = lambda bi_, qb: (bi_, 0, 0)
    o = pl.pallas_call(
        functools.partial(_attn_kernel, tk=tk, tkf=min(1024, s), n_sel=n_sel),
        grid=(b, s // Q_BLOCK),
        in_specs=[pl.BlockSpec((1, nq, Q_BLOCK), qcol), pl.BlockSpec((1, nqi, Q_BLOCK), qcol),
                  pl.BlockSpec((1, IDX_HEADS, Q_BLOCK), qcol), pl.BlockSpec((1, s, nkv), full),
                  pl.BlockSpec((1, nkv, s), full), pl.BlockSpec((1, s, IDX_DIM), full),
                  pl.BlockSpec((1, HEAD_DIM), lambda bi_, qb: (0, 0))],
        out_specs=pl.BlockSpec((1, Q_BLOCK, nq), lambda bi_, qb: (bi_, qb, 0)),
        out_shape=jax.ShapeDtypeStruct((b, s, nq), BF16),
        scratch_shapes=[pltpu.VMEM((s, Q_BLOCK), F32), pltpu.VMEM((s, Q_BLOCK), BF16)],
        compiler_params=_params("parallel", "arbitrary"),
        name="sparse_attn",
    )(qt, qit, wit, k.reshape(b, s, nkv), vt, ki.reshape(b, s, IDX_DIM), k_gain.reshape(1, HEAD_DIM))
    return _proj_res(x2, o.reshape(n, nq), w_o, layer).reshape(b, s, d)


def _sgu_kernel(x_ref, g_ref, w_ref, b_ref, vg_ref, ws_ref, bs_ref, o_ref, *, tm, sub):
    ii = lax.broadcasted_iota(jnp.int32, (SGU_BLOCK, SGU_BLOCK), 0) // CHUNK
    jj = lax.broadcasted_iota(jnp.int32, (SGU_BLOCK, SGU_BLOCK), 1) // CHUNK
    causal = jj <= ii
    ws = [jnp.where(causal, ws_ref[gi], 0.0).astype(BF16) for gi in range(SGU_GROUPS)]
    bs = bs_ref[...]
    width = w_ref.shape[1] // 2
    gd = width // SGU_GROUPS
    for st in range(tm // sub):
        r0 = st * sub
        h = _rms(x_ref[r0:r0 + sub, :], g_ref[...]).astype(BF16)
        z = _dot(h, w_ref[...]) + b_ref[...]
        z = 0.5 * z * (1.0 + lax.erf(z * (2.0 ** -0.5)))
        u = z[:, :width]
        v = _rms(z[:, width:], vg_ref[...]).astype(BF16)
        for gi in range(SGU_GROUPS):
            cs = slice(gi * gd, (gi + 1) * gd)
            for nb in range(sub // SGU_BLOCK):
                rs = slice(nb * SGU_BLOCK, (nb + 1) * SGU_BLOCK)
                mixed = _dot(ws[gi], v[rs, cs]) + bs[:, gi:gi + 1]
                o_ref[r0 + nb * SGU_BLOCK:r0 + (nb + 1) * SGU_BLOCK, cs] = (u[rs, cs] * mixed).astype(BF16)


def _spatial_gating(x, g, w_in, b_in, v_gain, w_s, b_s, w_o, layer):
    b, s, d = x.shape
    n = b * s
    x2 = x.reshape(n, d)
    width = w_in.shape[1] // 2
    tm = min(512, s)
    row = lambda i: (i, 0)
    gated = pl.pallas_call(
        functools.partial(_sgu_kernel, tm=tm, sub=min(256, tm)),
        grid=(n // tm,),
        in_specs=[pl.BlockSpec((tm, d), row), _resident((1, d)), _resident((d, 2 * width)),
                  _resident((1, 2 * width)), _resident((1, width)),
                  _resident((SGU_GROUPS, SGU_BLOCK, SGU_BLOCK)), _resident((SGU_BLOCK, SGU_GROUPS))],
        out_specs=pl.BlockSpec((tm, width), row),
        out_shape=jax.ShapeDtypeStruct((n, width), BF16),
        compiler_params=_params("parallel"),
        name="sgu_gate",
    )(x2, g.reshape(1, d), w_in.astype(BF16), b_in.reshape(1, 2 * width), v_gain.reshape(1, width),
      w_s, b_s.T)
    return _proj_res(x2, gated, w_o, layer).reshape(b, s, d)


def kernel(x, norm_mix, norm_ffn, pool_w, pool_scale, attn_w_in, attn_q_gain, attn_k_gain, attn_w_o,
           sgu_w_in, sgu_b_in, sgu_v_gain, sgu_w_s, sgu_b_s, sgu_w_o, ffn_w_up, ffn_w_down):
    b, s, d = x.shape
    depth = norm_mix.shape[0]
    for i in range(depth):
        kind, j = i % 3, i // 3
        if kind == 0:
            x = _pool_mixer(x, norm_mix[i], pool_w[j].astype(BF16), pool_scale[j])
        elif kind == 1:
            x = _sparse_attention(x, norm_mix[i], attn_w_in, attn_q_gain[j], attn_k_gain[j],
                                  attn_w_o, j)
        else:
            x = _spatial_gating(x, norm_mix[i], sgu_w_in[j], sgu_b_in[j], sgu_v_gain[j], sgu_w_s[j],
                                sgu_b_s[j], sgu_w_o, j)
        x = _ffn(x.reshape(b * s, d), norm_ffn[i], ffn_w_up, ffn_w_down, i).reshape(b, s, d)
    return x
```

```python
import functools

import jax
import jax.numpy as jnp
from jax import lax
from jax.experimental import pallas as pl
from jax.experimental.pallas import tpu as pltpu

EPS = 1e-6
CHUNK = 64
POOL_WINDOWS = (2, 4, 8, 16)
POOL_HALO = 16
HEAD_DIM = 128
N_KV_HEADS = 4
IDX_HEADS = 16
IDX_DIM = 64
INDEX_TOPK = 256
Q_BLOCK = 128
ROPE_THETA = 500000.0
ROT_FRACTION = 4
SGU_BLOCK = 128
SGU_GROUPS = 8
LANES = 128
MXU_COLS = 256
assert CHUNK & (CHUNK - 1) == 0
INT_MIN = -(2 ** 31)
COUNT_ROWS = 64
ONES_ROWS = 16
DEN_MIN = 2.0 ** -60
LOG2E = 1.4426950408889634
MASK_BIAS = -1e30
VMEM_LIMIT_BYTES = 60 * 1024 * 1024

F32 = jnp.float32
BF16 = jnp.bfloat16


def _params(*sem):
    return pltpu.CompilerParams(dimension_semantics=sem, vmem_limit_bytes=VMEM_LIMIT_BYTES)


def _resident(shape):
    nd = len(shape)
    return pl.BlockSpec(shape, lambda *_: (0,) * nd, pipeline_mode=pl.Buffered(1))


def _cast_once(w_ref, wbf_ref, transpose=False):
    @pl.when(pl.program_id(0) == 0)
    def _():
        if transpose:
            for r in range(0, w_ref.shape[0], LANES):
                wbf_ref[:, r:r + LANES] = w_ref[r:r + LANES, :].T.astype(BF16)
        else:
            wbf_ref[...] = w_ref[...].astype(BF16)


def _rms(xf, g):
    ms = jnp.mean(xf * xf, axis=-1, keepdims=True)
    return xf * lax.rsqrt(ms + EPS) * g


def _dot(a, b):
    return jnp.dot(a, b, preferred_element_type=F32)


def _pool_kernel(x_ref, halo_ref, g_ref, w_ref, scale_ref, o_ref, *, ts):
    i = pl.program_id(1)
    x = x_ref[0]
    g = g_ref[...]
    h = _rms(x, g)
    hh = _rms(halo_ref[0], g)
    hh = jnp.where(i > 0, hh, 0.0)
    hf = jnp.concatenate([hh, h], axis=0)
    t1 = (i * ts + lax.broadcasted_iota(jnp.int32, (ts, 1), 0) + 1).astype(F32)
    cg = x.shape[1] // len(POOL_WINDOWS)
    for gi, w in enumerate(POOL_WINDOWS):
        sl = slice(gi * cg, (gi + 1) * cg)
        s = hf[:, sl]
        k = 1
        while k < w:
            s = s + pltpu.roll(s, k, 0)
            k *= 2
        mean = s[POOL_HALO:] / jnp.minimum(t1, float(w))
        p = (mean - h[:, sl]).astype(BF16)
        y = _dot(p, w_ref[gi]) * scale_ref[:, sl]
        o_ref[0, :, sl] = x[:, sl] + y


def _pool_mixer(x, g, w_bf, scale):
    b, s, d = x.shape
    ts = min(512, s)
    hb = ts // POOL_HALO
    ng = len(POOL_WINDOWS)
    return pl.pallas_call(
        functools.partial(_pool_kernel, ts=ts),
        grid=(b, s // ts),
        in_specs=[
            pl.BlockSpec((1, ts, d), lambda bi, i: (bi, i, 0)),
            pl.BlockSpec((1, POOL_HALO, d), lambda bi, i: (bi, jnp.maximum(i * hb - 1, 0), 0)),
            _resident((1, d)),
            _resident((ng, d // ng, d // ng)),
            _resident((1, d)),
        ],
        out_specs=pl.BlockSpec((1, ts, d), lambda bi, i: (bi, i, 0)),
        out_shape=jax.ShapeDtypeStruct((b, s, d), F32),
        compiler_params=_params("parallel", "parallel"),
        name="pool_mixer",
    )(x, x, g.reshape(1, d), w_bf, scale.reshape(1, d))


def _ffn_kernel(x_ref, g_ref, wu_ref, wd_ref, o_ref, h_ref):
    j = pl.program_id(1)

    @pl.when(j == 0)
    def _():
        x = x_ref[...]
        h_ref[...] = _rms(x, g_ref[...]).astype(BF16)
        o_ref[...] = x

    u = _dot(h_ref[...], wu_ref[...].astype(BF16))
    a = jnp.square(jnp.maximum(u, 0.0)).astype(BF16)
    o_ref[...] += _dot(a, wd_ref[...].astype(BF16))


def _ffn(x2, g, w_up, w_down, layer):
    n, d = x2.shape
    f = w_up.shape[2]
    tm = min(1024, n)
    tf = 512
    return pl.pallas_call(
        _ffn_kernel,
        grid=(n // tm, f // tf),
        in_specs=[
            pl.BlockSpec((tm, d), lambda i, j: (i, 0)),
            _resident((1, d)),
            pl.BlockSpec((None, d, tf), lambda i, j: (layer, 0, j)),
            pl.BlockSpec((None, tf, d), lambda i, j: (layer, j, 0)),
        ],
        out_specs=pl.BlockSpec((tm, d), lambda i, j: (i, 0)),
        out_shape=jax.ShapeDtypeStruct((n, d), F32),
        scratch_shapes=[pltpu.VMEM((tm, d), BF16)],
        compiler_params=_params("parallel", "arbitrary"),
        name="ffn",
    )(x2, g.reshape(1, d), w_up, w_down)


def _proj_res_kernel(x_ref, a_ref, w_ref, o_ref, wbf_ref):
    _cast_once(w_ref, wbf_ref)
    o_ref[...] = x_ref[...] + _dot(a_ref[...], wbf_ref[...])


def _proj_res(x2, a_bf, w, layer):
    n, d = x2.shape
    kdim = a_bf.shape[1]
    tm = min(512, n)
    return pl.pallas_call(
        _proj_res_kernel,
        grid=(n // tm,),
        in_specs=[
            pl.BlockSpec((tm, d), lambda i: (i, 0)),
            pl.BlockSpec((tm, kdim), lambda i: (i, 0)),
            pl.BlockSpec((None, kdim, d), lambda i: (layer, 0, 0), pipeline_mode=pl.Buffered(1)),
        ],
        out_specs=pl.BlockSpec((tm, d), lambda i: (i, 0)),
        out_shape=jax.ShapeDtypeStruct((n, d), F32),
        scratch_shapes=[pltpu.VMEM((kdim, d), BF16)],
        compiler_params=_params("arbitrary"),
        name="proj_res",
    )(x2, a_bf, w)


def _rope_tables(s, width, rot):
    half = rot // 2
    inv = ROPE_THETA ** (-jnp.arange(half, dtype=F32) / half)
    ang = jnp.arange(s, dtype=F32)[:, None] * inv[None, :]
    cos, sin = jnp.cos(ang), jnp.sin(ang)
    pad = jnp.zeros((s, width - rot), F32)
    zero = jnp.zeros((s, half), F32)
    c = jnp.concatenate([cos, cos, pad + 1.0], axis=1)
    a = jnp.concatenate([-sin, zero, pad], axis=1)
    b = jnp.concatenate([zero, sin, pad], axis=1)
    rep = LANES // width
    return tuple(jnp.tile(t, (1, rep)) for t in (c, a, b))


def _rope_tables_t(s, rot):
    half = rot // 2
    inv = ROPE_THETA ** (-jnp.arange(half, dtype=F32) / half)
    ang = inv[:, None] * jnp.arange(s, dtype=F32)[None, :]
    return jnp.cos(ang), jnp.sin(ang)


def _rope(x, c, a, b, half):
    return x * c + pltpu.roll(x, LANES - half, 1) * a + pltpu.roll(x, half, 1) * b


def _rope_t(xt, cos, sin):
    half = cos.shape[0]
    x1, x2 = xt[:half], xt[half:2 * half]
    return jnp.concatenate([x1 * cos - x2 * sin, x2 * cos + x1 * sin, xt[2 * half:]], axis=0)


def _q_proj_kernel(x_ref, g_ref, w_ref, gain_ref, cos_ref, sin_ref, qt_ref, qn_ref, wbf_ref):
    _cast_once(w_ref, wbf_ref, transpose=True)
    h = _rms(x_ref[...], g_ref[...]).astype(BF16)
    tm = h.shape[0]
    gain = jnp.concatenate([gain_ref[...]] * (tm // LANES), axis=1)
    cos, sin = cos_ref[...], sin_ref[...]
    for pair in range(wbf_ref.shape[1] // MXU_COLS):
        q = _dot(h, wbf_ref[:, pair * MXU_COLS:(pair + 1) * MXU_COLS])
        for hd in range(MXU_COLS // HEAD_DIM):
            qh = q[:, hd * HEAD_DIM:(hd + 1) * HEAD_DIM].T
            qh = qh * lax.rsqrt(jnp.mean(qh * qh, axis=0, keepdims=True) + EPS) * gain
            qh = _rope_t(qh, cos, sin) * (HEAD_DIM ** -0.5 * LOG2E)
            row = pair * MXU_COLS + hd * HEAD_DIM
            qt_ref[0, row:row + HEAD_DIM, :] = qh.astype(BF16)
            head = row // HEAD_DIM
            qn_ref[0, head:head + 1, :] = jnp.sqrt(jnp.sum(qh * qh, axis=0, keepdims=True))


def _kv_proj_kernel(x_ref, g_ref, w_ref, wtail_ref, gain_ref, c_ref, a_ref, b_ref, ci_ref, ai_ref,
                    bi_ref, cosi_ref, sini_ref, k_ref, vt_ref, qit_ref, ki_ref, wit_ref, wbf_ref):
    _cast_once(w_ref, wbf_ref, transpose=True)
    h = _rms(x_ref[...], g_ref[...]).astype(BF16)
    y = _dot(h, wbf_ref[...])
    gain = gain_ref[...]
    c, a, b = c_ref[...], a_ref[...], b_ref[...]
    half = HEAD_DIM // ROT_FRACTION // 2
    halfi = IDX_DIM // ROT_FRACTION // 2
    nkv = N_KV_HEADS * HEAD_DIM
    for hd in range(N_KV_HEADS):
        sl = slice(hd * HEAD_DIM, (hd + 1) * HEAD_DIM)
        k_ref[:, sl] = _rope(_rms(y[:, sl], gain), c, a, b, half).astype(BF16)
        vt_ref[0, sl, :] = y[:, nkv + hd * HEAD_DIM:nkv + (hd + 1) * HEAD_DIM].T.astype(BF16)
    nqi = IDX_HEADS * IDX_DIM
    cosi, sini = cosi_ref[...], sini_ref[...]
    for hd in range(IDX_HEADS):
        col = 2 * nkv + hd * IDX_DIM
        if hd % 2 == 0:
            pair_t = y[:, col:col + LANES].T
        qh = pair_t[(hd % 2) * IDX_DIM:(hd % 2 + 1) * IDX_DIM]
        qit_ref[0, hd * IDX_DIM:(hd + 1) * IDX_DIM, :] = _rope_t(qh, cosi, sini).astype(BF16)
    kw = _dot(h, wtail_ref[...])
    ki_ref[...] = _rope(kw, ci_ref[...], ai_ref[...], bi_ref[...], halfi)[:, :IDX_DIM].astype(BF16)
    wit_ref[0] = kw.T[IDX_DIM:IDX_DIM + IDX_HEADS] * (IDX_HEADS ** -0.5 * IDX_DIM ** -0.5)


def _key_to_f32(key):
    bits = jnp.where(key < 0, key ^ jnp.int32(0x7FFFFFFF), key)
    f = pltpu.bitcast(bits, F32)
    return jnp.where(f != f, jnp.inf, f)


def _attn_kernel(qt_ref, qn_ref, qit_ref, wit_ref, k_ref, vt_ref, ki_ref, kgain_ref, o_ref, sc_ref,
                 bias_ref, *, tk, tkf, n_sel):
    t0 = pl.program_id(1) * Q_BLOCK
    nk = (t0 + Q_BLOCK + tk - 1) // tk
    q_pos = t0 + lax.broadcasted_iota(jnp.int32, (1, Q_BLOCK), 1)
    key_end = (lax.shift_right_logical(q_pos, CHUNK.bit_length() - 1) + 1) * CHUNK
    key_row = lax.broadcasted_iota(jnp.int32, (tk, Q_BLOCK), 0)

    qit = qit_ref[0]
    wit = wit_ref[0]
    npair = IDX_HEADS // 2
    rhs = [jnp.concatenate([qit[(2 * p) * IDX_DIM:(2 * p + 1) * IDX_DIM],
                            qit[(2 * p + 1) * IDX_DIM:(2 * p + 2) * IDX_DIM]], axis=1)
           for p in range(npair)]

    def score_tile(kt, carry):
        off = pl.multiple_of(kt * tk, tk)
        ki_t = ki_ref[0, pl.ds(off, tk), :]
        acc = jnp.zeros((tk, Q_BLOCK), F32)
        for p in range(npair):
            d = jnp.maximum(_dot(ki_t, rhs[p]), 0.0)
            acc = acc + d[:, :Q_BLOCK] * wit[2 * p:2 * p + 1, :]
            acc = acc + d[:, Q_BLOCK:] * wit[2 * p + 1:2 * p + 2, :]
        sc_ref[pl.ds(off, tk), :] = jnp.where(key_row < key_end - off, acc, -jnp.inf)
        return carry

    lax.fori_loop(0, nk, score_tile, 0)

    nkf = (t0 + Q_BLOCK + tkf - 1) // tkf

    def pad_tile(kt, carry):
        off = pl.multiple_of(kt * tk, tk)
        bias_ref[pl.ds(off, tk), :] = jnp.full((tk, Q_BLOCK), MASK_BIAS, BF16)
        return carry

    lax.fori_loop(nk, nkf * (tkf // tk), pad_tile, 0)

    def count(pred):
        def body(kt, c):
            off = pl.multiple_of(kt * tk, tk)
            m = jnp.where(pred(sc_ref[pl.ds(off, tk), :], key_row + off), 1.0, 0.0)
            return c + jnp.sum(m.reshape(tk // COUNT_ROWS, COUNT_ROWS, Q_BLOCK), axis=0)

        c = lax.fori_loop(0, nk, body, jnp.zeros((COUNT_ROWS, Q_BLOCK), F32))
        return jnp.sum(c, axis=0, keepdims=True)

    def bit_body(bi, carry):
        key, cnt = carry
        cand = key + lax.shift_left(jnp.int32(1), 31 - bi)
        cand_f = _key_to_f32(cand)
        c = count(lambda x, pos: x >= cand_f)
        ok = c >= n_sel
        return jnp.where(ok, cand, key), jnp.where(ok, c, cnt)

    nbits = jnp.where(t0 + Q_BLOCK <= n_sel, 0, 32)
    key, cnt = lax.fori_loop(0, nbits, bit_body, (jnp.full((1, Q_BLOCK), INT_MIN, jnp.int32),
                                                  jnp.zeros((1, Q_BLOCK), F32)))
    thr = jnp.where(key == INT_MIN, jnp.finfo(F32).min, _key_to_f32(key))

    @pl.when(jnp.max(cnt) > n_sel)
    def _():
        need = n_sel - count(lambda x, pos: x > thr)
        pos_bits = (sc_ref.shape[0] - 1).bit_length() + 1

        def body(bi, end):
            cand = end + lax.shift_left(jnp.int32(1), pos_bits - 1 - bi)
            c = count(lambda x, pos: jnp.logical_and(x == thr, pos < cand))
            return jnp.where(c <= need, cand, end)

        pos_end = lax.fori_loop(0, pos_bits, body, jnp.zeros((1, Q_BLOCK), jnp.int32))

        def retire(kt, carry):
            off = pl.multiple_of(kt * tk, tk)
            x = sc_ref[pl.ds(off, tk), :]
            late_tie = jnp.logical_and(x == thr, key_row + off >= pos_end)
            sc_ref[pl.ds(off, tk), :] = jnp.where(late_tie, -jnp.inf, x)
            return carry

        lax.fori_loop(0, nk, retire, 0)

    def bias_tile(kt, carry):
        off = pl.multiple_of(kt * tk, tk)
        sel = sc_ref[pl.ds(off, tk), :] >= thr
        bias_ref[pl.ds(off, tk), :] = jnp.where(sel, 0.0, MASK_BIAS).astype(BF16)
        return carry

    lax.fori_loop(0, nk, bias_tile, 0)

    qt = qt_ref[0]
    rep = qt.shape[0] // HEAD_DIM // N_KV_HEADS
    cols = rep * Q_BLOCK
    eye = (lax.broadcasted_iota(jnp.int32, (Q_BLOCK, Q_BLOCK), 0)
           == lax.broadcasted_iota(jnp.int32, (Q_BLOCK, Q_BLOCK), 1))
    eye = jnp.where(eye, 1.0, 0.0).astype(BF16)
    ones = jnp.ones((ONES_ROWS, tkf), BF16)
    gsls = [slice(g * HEAD_DIM, (g + 1) * HEAD_DIM) for g in range(N_KV_HEADS)]
    qaugs = []
    for g in range(N_KV_HEADS):
        qg = jnp.concatenate(
            [qt[(g * rep + r) * HEAD_DIM:(g * rep + r + 1) * HEAD_DIM] for r in range(rep)], axis=1)
        qaugs.append(jnp.concatenate([qg, jnp.concatenate([eye] * rep, axis=1)], axis=0))

    def flash(shift):
        def body(kt, accs):
            off = pl.multiple_of(kt * tkf, tkf)
            bias_t = bias_ref[pl.ds(off, tkf), :]
            out = []
            for g in range(N_KV_HEADS):
                kaug = jnp.concatenate([k_ref[0, pl.ds(off, tkf), gsls[g]], bias_t], axis=1)
                vaug = jnp.concatenate([vt_ref[0, gsls[g], pl.ds(off, tkf)], ones], axis=0)
                p = jnp.exp2(_dot(kaug, qaugs[g]) - shift[g]).astype(BF16)
                out.append(accs[g] + _dot(vaug, p))
            return tuple(out)

        zero = jnp.zeros((HEAD_DIM + ONES_ROWS, cols), F32)
        return lax.fori_loop(0, nkf, body, (zero,) * N_KV_HEADS)

    def column_max():
        def body(kt, ms):
            off = pl.multiple_of(kt * tkf, tkf)
            bias_t = bias_ref[pl.ds(off, tkf), :]
            out = []
            for g in range(N_KV_HEADS):
                kaug = jnp.concatenate([k_ref[0, pl.ds(off, tkf), gsls[g]], bias_t], axis=1)
                out.append(jnp.maximum(ms[g], jnp.max(_dot(kaug, qaugs[g]), axis=0, keepdims=True)))
            return tuple(out)

        return lax.fori_loop(0, nkf, body, (jnp.full((1, cols), MASK_BIAS, F32),) * N_KV_HEADS)

    def write(accs):
        for g in range(N_KV_HEADS):
            o = accs[g][:HEAD_DIM] / accs[g][HEAD_DIM:HEAD_DIM + 1]
            for r in range(rep):
                hsl = slice((g * rep + r) * HEAD_DIM, (g * rep + r + 1) * HEAD_DIM)
                o_ref[0, :, hsl] = o[:, r * Q_BLOCK:(r + 1) * Q_BLOCK].T.astype(BF16)

    kmax = (HEAD_DIM ** 0.5) * jnp.max(jnp.abs(kgain_ref[...]), axis=1, keepdims=True)
    qn = qn_ref[0]
    bound = [jnp.concatenate([qn[g * rep + r:g * rep + r + 1] for r in range(rep)], axis=1) * kmax
             for g in range(N_KV_HEADS)]
    accs = flash(bound)
    den = jnp.concatenate([acc[HEAD_DIM:HEAD_DIM + 1] for acc in accs], axis=1)
    safe = jnp.logical_and(jnp.min(den) >= DEN_MIN, jnp.max(den) <= 1.0 / DEN_MIN)

    @pl.when(safe)
    def _():
        write(accs)

    @pl.when(jnp.logical_not(safe))
    def _():
        write(flash(column_max()))


def _sparse_attention(x, g, w_in, q_gain, k_gain, w_o, layer):
    b, s, d = x.shape
    n = b * s
    x2 = x.reshape(n, d)
    nq = d
    nkv = N_KV_HEADS * HEAD_DIM
    nqi = IDX_HEADS * IDX_DIM
    nmid = 2 * nkv + nqi
    assert nmid == nq
    pad = LANES - IDX_DIM - IDX_HEADS
    w_tail = jnp.pad(w_in[layer, :, nq + nmid:], ((0, 0), (0, pad))).astype(BF16)
    w_in_t = jnp.swapaxes(w_in, 1, 2)
    wblock = lambda blk: pl.BlockSpec((None, nq, d), lambda i: (layer, blk, 0),
                                      pipeline_mode=pl.Buffered(1))
    c, a, bt = _rope_tables(s, HEAD_DIM, HEAD_DIM // ROT_FRACTION)
    ci, ai, bi = _rope_tables(s, IDX_DIM, IDX_DIM // ROT_FRACTION)
    cos_t, sin_t = _rope_tables_t(s, HEAD_DIM // ROT_FRACTION)
    cosi_t, sini_t = _rope_tables_t(s, IDX_DIM // ROT_FRACTION)
    q_gain_b = jnp.broadcast_to(q_gain[:, None], (HEAD_DIM, LANES))

    tm = min(512, s)
    nt = s // tm
    row = lambda i: (i, 0)
    pos = lambda i: (i % nt, 0)
    tcol = lambda i: (i // nt, 0, i % nt)
    tab = pl.BlockSpec((tm, LANES), pos)
    tab_t = lambda half: pl.BlockSpec((half, tm), lambda i: (0, i % nt))
    half = HEAD_DIM // ROT_FRACTION // 2
    halfi = IDX_DIM // ROT_FRACTION // 2
    qt, qn = pl.pallas_call(
        _q_proj_kernel,
        grid=(n // tm,),
        in_specs=[pl.BlockSpec((tm, d), row), _resident((1, d)), wblock(0),
                  _resident((HEAD_DIM, LANES)), tab_t(half), tab_t(half)],
        out_specs=[pl.BlockSpec((1, nq, tm), tcol), pl.BlockSpec((1, nq // HEAD_DIM, tm), tcol)],
        out_shape=[jax.ShapeDtypeStruct((b, nq, s), BF16),
                   jax.ShapeDtypeStruct((b, nq // HEAD_DIM, s), F32)],
        scratch_shapes=[pltpu.VMEM((d, nq), BF16)],
        compiler_params=_params("arbitrary"),
        name="attn_q_proj",
    )(x2, g.reshape(1, d), w_in_t, q_gain_b, cos_t, sin_t)

    k, vt, qit, ki, wit = pl.pallas_call(
        _kv_proj_kernel,
        grid=(n // tm,),
        in_specs=[pl.BlockSpec((tm, d), row), _resident((1, d)), wblock(1), _resident((d, LANES)),
                  _resident((1, HEAD_DIM)), tab, tab, tab, tab, tab, tab, tab_t(halfi), tab_t(halfi)],
        out_specs=[pl.BlockSpec((tm, nkv), row), pl.BlockSpec((1, nkv, tm), tcol),
                   pl.BlockSpec((1, nqi, tm), tcol), pl.BlockSpec((tm, IDX_DIM), row),
                   pl.BlockSpec((1, IDX_HEADS, tm), tcol)],
        out_shape=[jax.ShapeDtypeStruct((n, nkv), BF16), jax.ShapeDtypeStruct((b, nkv, s), BF16),
                   jax.ShapeDtypeStruct((b, nqi, s), BF16), jax.ShapeDtypeStruct((n, IDX_DIM), BF16),
                   jax.ShapeDtypeStruct((b, IDX_HEADS, s), F32)],
        scratch_shapes=[pltpu.VMEM((d, nmid), BF16)],
        compiler_params=_params("arbitrary"),
        name="attn_kv_proj",
    )(x2, g.reshape(1, d), w_in_t, w_tail, k_gain.reshape(1, HEAD_DIM), c, a, bt, ci, ai, bi, cosi_t,
      sini_t)

    n_sel = min(INDEX_TOPK, s // 4)
    tk = min(512, s)
    qcol = lambda bi_, qb: (bi_, 0, qb)
    full = lambda bi_, qb: (bi_, 0, 0)
    o = pl.pallas_call(
        functools.partial(_attn_kernel, tk=tk, tkf=min(1024, s), n_sel=n_sel),
        grid=(b, s // Q_BLOCK),
        in_specs=[pl.BlockSpec((1, nq, Q_BLOCK), qcol), pl.BlockSpec((1, nq // HEAD_DIM, Q_BLOCK), qcol),
                  pl.BlockSpec((1, nqi, Q_BLOCK), qcol),
                  pl.BlockSpec((1, IDX_HEADS, Q_BLOCK), qcol), pl.BlockSpec((1, s, nkv), full),
                  pl.BlockSpec((1, nkv, s), full), pl.BlockSpec((1, s, IDX_DIM), full),
                  pl.BlockSpec((1, HEAD_DIM), lambda bi_, qb: (0, 0))],
        out_specs=pl.BlockSpec((1, Q_BLOCK, nq), lambda bi_, qb: (bi_, qb, 0)),
        out_shape=jax.ShapeDtypeStruct((b, s, nq), BF16),
        scratch_shapes=[pltpu.VMEM((s, Q_BLOCK), F32), pltpu.VMEM((s, Q_BLOCK), BF16)],
        compiler_params=_params("parallel", "arbitrary"),
        name="sparse_attn",
    )(qt, qn, qit, wit, k.reshape(b, s, nkv), vt, ki.reshape(b, s, IDX_DIM),
      k_gain.reshape(1, HEAD_DIM))
    return _proj_res(x2, o.reshape(n, nq), w_o, layer).reshape(b, s, d)


def _sgu_kernel(x_ref, g_ref, w_ref, b_ref, vg_ref, ws_ref, bs_ref, o_ref, *, tm, sub):
    ii = lax.broadcasted_iota(jnp.int32, (SGU_BLOCK, SGU_BLOCK), 0) // CHUNK
    jj = lax.broadcasted_iota(jnp.int32, (SGU_BLOCK, SGU_BLOCK), 1) // CHUNK
    causal = jj <= ii
    ws = [jnp.where(causal, ws_ref[gi], 0.0).astype(BF16) for gi in range(SGU_GROUPS)]
    bs = bs_ref[...]
    width = w_ref.shape[1] // 2
    gd = width // SGU_GROUPS
    for st in range(tm // sub):
        r0 = st * sub
        h = _rms(x_ref[r0:r0 + sub, :], g_ref[...]).astype(BF16)
        z = _dot(h, w_ref[...]) + b_ref[...]
        z = 0.5 * z * (1.0 + lax.erf(z * (2.0 ** -0.5)))
        u = z[:, :width]
        v = _rms(z[:, width:], vg_ref[...]).astype(BF16)
        for gi in range(SGU_GROUPS):
            cs = slice(gi * gd, (gi + 1) * gd)
            for nb in range(sub // SGU_BLOCK):
                rs = slice(nb * SGU_BLOCK, (nb + 1) * SGU_BLOCK)
                mixed = _dot(ws[gi], v[rs, cs]) + bs[:, gi:gi + 1]
                o_ref[r0 + nb * SGU_BLOCK:r0 + (nb + 1) * SGU_BLOCK, cs] = (u[rs, cs] * mixed).astype(BF16)


def _spatial_gating(x, g, w_in, b_in, v_gain, w_s, b_s, w_o, layer):
    b, s, d = x.shape
    n = b * s
    x2 = x.reshape(n, d)
    width = w_in.shape[1] // 2
    tm = min(512, s)
    row = lambda i: (i, 0)
    gated = pl.pallas_call(
        functools.partial(_sgu_kernel, tm=tm, sub=min(256, tm)),
        grid=(n // tm,),
        in_specs=[pl.BlockSpec((tm, d), row), _resident((1, d)), _resident((d, 2 * width)),
                  _resident((1, 2 * width)), _resident((1, width)),
                  _resident((SGU_GROUPS, SGU_BLOCK, SGU_BLOCK)), _resident((SGU_BLOCK, SGU_GROUPS))],
        out_specs=pl.BlockSpec((tm, width), row),
        out_shape=jax.ShapeDtypeStruct((n, width), BF16),
        compiler_params=_params("parallel"),
        name="sgu_gate",
    )(x2, g.reshape(1, d), w_in.astype(BF16), b_in.reshape(1, 2 * width), v_gain.reshape(1, width),
      w_s, b_s.T)
    return _proj_res(x2, gated, w_o, layer).reshape(b, s, d)


def kernel(x, norm_mix, norm_ffn, pool_w, pool_scale, attn_w_in, attn_q_gain, attn_k_gain, attn_w_o,
           sgu_w_in, sgu_b_in, sgu_v_gain, sgu_w_s, sgu_b_s, sgu_w_o, ffn_w_up, ffn_w_down):
    b, s, d = x.shape
    depth = norm_mix.shape[0]
    for i in range(depth):
        kind, j = i % 3, i // 3
        if kind == 0:
            x = _pool_mixer(x, norm_mix[i], pool_w[j].astype(BF16), pool_scale[j])
        elif kind == 1:
            x = _sparse_attention(x, norm_mix[i], attn_w_in, attn_q_gain[j], attn_k_gain[j],
                                  attn_w_o, j)
        else:
            x = _spatial_gating(x, norm_mix[i], sgu_w_in[j], sgu_b_in[j], sgu_v_gain[j], sgu_w_s[j],
                                sgu_b_s[j], sgu_w_o, j)
        x = _ffn(x.reshape(b * s, d), norm_ffn[i], ffn_w_up, ffn_w_down, i).reshape(b, s, d)
    return x
```

```python
import functools

import jax
import jax.numpy as jnp
from jax import lax
from jax.experimental import pallas as pl
from jax.experimental.pallas import tpu as pltpu

EPS = 1e-6
CHUNK = 64
POOL_WINDOWS = (2, 4, 8, 16)
POOL_HALO = 16
HEAD_DIM = 128
N_KV_HEADS = 4
IDX_HEADS = 16
IDX_DIM = 64
INDEX_TOPK = 256
Q_BLOCK = 128
ROPE_THETA = 500000.0
ROT_FRACTION = 4
SGU_BLOCK = 128
SGU_GROUPS = 8
LANES = 128
MXU_COLS = 256
assert CHUNK & (CHUNK - 1) == 0
INT_MIN = -(2 ** 31)
COUNT_ROWS = 64
ONES_ROWS = 16
DEN_MIN = 2.0 ** -60
LOG2E = 1.4426950408889634
MASK_BIAS = -1e30
VMEM_LIMIT_BYTES = 60 * 1024 * 1024

POOL_ROWS = 1024
FFN_ROWS = 1024
FFN_COLS = 512
PROJ_ROWS = 512
SGU_ROWS = 512
SGU_SUB_ROWS = 256
SCORE_KEYS = 512
FLASH_KEYS = 1024

F32 = jnp.float32
BF16 = jnp.bfloat16


def _params(*sem):
    return pltpu.CompilerParams(dimension_semantics=sem, vmem_limit_bytes=VMEM_LIMIT_BYTES)


def _resident(shape):
    nd = len(shape)
    return pl.BlockSpec(shape, lambda *_: (0,) * nd, pipeline_mode=pl.Buffered(1))


def _cast_once(w_ref, wbf_ref, transpose=False):
    @pl.when(pl.program_id(0) == 0)
    def _():
        if transpose:
            for r in range(0, w_ref.shape[0], LANES):
                wbf_ref[:, r:r + LANES] = w_ref[r:r + LANES, :].T.astype(BF16)
        else:
            wbf_ref[...] = w_ref[...].astype(BF16)


def _rms(xf, g):
    ms = jnp.mean(xf * xf, axis=-1, keepdims=True)
    return xf * lax.rsqrt(ms + EPS) * g


def _dot(a, b):
    return jnp.dot(a, b, preferred_element_type=F32)


def _pool_kernel(x_ref, halo_ref, g_ref, w_ref, scale_ref, o_ref, *, ts):
    i = pl.program_id(1)
    x = x_ref[0]
    g = g_ref[...]
    h = _rms(x, g)
    hh = _rms(halo_ref[0], g)
    hh = jnp.where(i > 0, hh, 0.0)
    hf = jnp.concatenate([hh, h], axis=0)
    t1 = (i * ts + lax.broadcasted_iota(jnp.int32, (ts, 1), 0) + 1).astype(F32)
    cg = x.shape[1] // len(POOL_WINDOWS)
    for gi, w in enumerate(POOL_WINDOWS):
        sl = slice(gi * cg, (gi + 1) * cg)
        s = hf[:, sl]
        k = 1
        while k < w:
            s = s + pltpu.roll(s, k, 0)
            k *= 2
        mean = s[POOL_HALO:] / jnp.minimum(t1, float(w))
        p = (mean - h[:, sl]).astype(BF16)
        y = _dot(p, w_ref[gi]) * scale_ref[:, sl]
        o_ref[0, :, sl] = x[:, sl] + y


def _pool_mixer(x, g, w_bf, scale):
    b, s, d = x.shape
    ts = min(POOL_ROWS, s)
    hb = ts // POOL_HALO
    ng = len(POOL_WINDOWS)
    return pl.pallas_call(
        functools.partial(_pool_kernel, ts=ts),
        grid=(b, s // ts),
        in_specs=[
            pl.BlockSpec((1, ts, d), lambda bi, i: (bi, i, 0)),
            pl.BlockSpec((1, POOL_HALO, d), lambda bi, i: (bi, jnp.maximum(i * hb - 1, 0), 0)),
            _resident((1, d)),
            _resident((ng, d // ng, d // ng)),
            _resident((1, d)),
        ],
        out_specs=pl.BlockSpec((1, ts, d), lambda bi, i: (bi, i, 0)),
        out_shape=jax.ShapeDtypeStruct((b, s, d), F32),
        compiler_params=_params("parallel", "parallel"),
        name="pool_mixer",
    )(x, x, g.reshape(1, d), w_bf, scale.reshape(1, d))


def _ffn_kernel(x_ref, g_ref, wu_ref, wd_ref, o_ref, h_ref):
    j = pl.program_id(1)

    @pl.when(j == 0)
    def _():
        x = x_ref[...]
        h_ref[...] = _rms(x, g_ref[...]).astype(BF16)
        o_ref[...] = x

    u = _dot(h_ref[...], wu_ref[...].astype(BF16))
    a = jnp.square(jnp.maximum(u, 0.0)).astype(BF16)
    o_ref[...] += _dot(a, wd_ref[...].astype(BF16))


def _ffn(x2, g, w_up, w_down, layer):
    n, d = x2.shape
    f = w_up.shape[2]
    tm = min(FFN_ROWS, n)
    tf = FFN_COLS
    return pl.pallas_call(
        _ffn_kernel,
        grid=(n // tm, f // tf),
        in_specs=[
            pl.BlockSpec((tm, d), lambda i, j: (i, 0)),
            _resident((1, d)),
            pl.BlockSpec((None, d, tf), lambda i, j: (layer, 0, j)),
            pl.BlockSpec((None, tf, d), lambda i, j: (layer, j, 0)),
        ],
        out_specs=pl.BlockSpec((tm, d), lambda i, j: (i, 0)),
        out_shape=jax.ShapeDtypeStruct((n, d), F32),
        scratch_shapes=[pltpu.VMEM((tm, d), BF16)],
        compiler_params=_params("parallel", "arbitrary"),
        name="ffn",
    )(x2, g.reshape(1, d), w_up, w_down)


def _proj_res_kernel(x_ref, a_ref, w_ref, o_ref, wbf_ref):
    _cast_once(w_ref, wbf_ref)
    o_ref[...] = x_ref[...] + _dot(a_ref[...], wbf_ref[...])


def _proj_res(x2, a_bf, w, layer):
    n, d = x2.shape
    kdim = a_bf.shape[1]
    tm = min(PROJ_ROWS, n)
    return pl.pallas_call(
        _proj_res_kernel,
        grid=(n // tm,),
        in_specs=[
            pl.BlockSpec((tm, d), lambda i: (i, 0)),
            pl.BlockSpec((tm, kdim), lambda i: (i, 0)),
            pl.BlockSpec((None, kdim, d), lambda i: (layer, 0, 0), pipeline_mode=pl.Buffered(1)),
        ],
        out_specs=pl.BlockSpec((tm, d), lambda i: (i, 0)),
        out_shape=jax.ShapeDtypeStruct((n, d), F32),
        scratch_shapes=[pltpu.VMEM((kdim, d), BF16)],
        compiler_params=_params("arbitrary"),
        name="proj_res",
    )(x2, a_bf, w)


def _rope_tables(s, width, rot):
    half = rot // 2
    inv = ROPE_THETA ** (-jnp.arange(half, dtype=F32) / half)
    ang = jnp.arange(s, dtype=F32)[:, None] * inv[None, :]
    cos, sin = jnp.cos(ang), jnp.sin(ang)
    pad = jnp.zeros((s, width - rot), F32)
    zero = jnp.zeros((s, half), F32)
    c = jnp.concatenate([cos, cos, pad + 1.0], axis=1)
    a = jnp.concatenate([-sin, zero, pad], axis=1)
    b = jnp.concatenate([zero, sin, pad], axis=1)
    rep = LANES // width
    return tuple(jnp.tile(t, (1, rep)) for t in (c, a, b))


def _rope_tables_t(s, rot):
    half = rot // 2
    inv = ROPE_THETA ** (-jnp.arange(half, dtype=F32) / half)
    ang = inv[:, None] * jnp.arange(s, dtype=F32)[None, :]
    return jnp.cos(ang), jnp.sin(ang)


def _rope(x, c, a, b, half):
    return x * c + pltpu.roll(x, LANES - half, 1) * a + pltpu.roll(x, half, 1) * b


def _rope_t(xt, cos, sin):
    half = cos.shape[0]
    x1, x2 = xt[:half], xt[half:2 * half]
    return jnp.concatenate([x1 * cos - x2 * sin, x2 * cos + x1 * sin, xt[2 * half:]], axis=0)


def _q_proj_kernel(x_ref, g_ref, w_ref, gain_ref, cos_ref, sin_ref, qt_ref, qn_ref, wbf_ref):
    _cast_once(w_ref, wbf_ref, transpose=True)
    h = _rms(x_ref[...], g_ref[...]).astype(BF16)
    tm = h.shape[0]
    gain = jnp.concatenate([gain_ref[...]] * (tm // LANES), axis=1)
    cos, sin = cos_ref[...], sin_ref[...]
    for pair in range(wbf_ref.shape[1] // MXU_COLS):
        q = _dot(h, wbf_ref[:, pair * MXU_COLS:(pair + 1) * MXU_COLS])
        for hd in range(MXU_COLS // HEAD_DIM):
            qh = q[:, hd * HEAD_DIM:(hd + 1) * HEAD_DIM].T
            qh = qh * lax.rsqrt(jnp.mean(qh * qh, axis=0, keepdims=True) + EPS) * gain
            qh = _rope_t(qh, cos, sin) * (HEAD_DIM ** -0.5 * LOG2E)
            row = pair * MXU_COLS + hd * HEAD_DIM
            qt_ref[0, row:row + HEAD_DIM, :] = qh.astype(BF16)
            head = row // HEAD_DIM
            qn_ref[0, head:head + 1, :] = jnp.sqrt(jnp.sum(qh * qh, axis=0, keepdims=True))


def _kv_proj_kernel(x_ref, g_ref, w_ref, wtail_ref, gain_ref, c_ref, a_ref, b_ref, ci_ref, ai_ref,
                    bi_ref, cosi_ref, sini_ref, k_ref, vt_ref, qit_ref, ki_ref, wit_ref, wbf_ref):
    _cast_once(w_ref, wbf_ref, transpose=True)
    h = _rms(x_ref[...], g_ref[...]).astype(BF16)
    y = _dot(h, wbf_ref[...])
    gain = gain_ref[...]
    c, a, b = c_ref[...], a_ref[...], b_ref[...]
    half = HEAD_DIM // ROT_FRACTION // 2
    halfi = IDX_DIM // ROT_FRACTION // 2
    nkv = N_KV_HEADS * HEAD_DIM
    for hd in range(N_KV_HEADS):
        sl = slice(hd * HEAD_DIM, (hd + 1) * HEAD_DIM)
        k_ref[:, sl] = _rope(_rms(y[:, sl], gain), c, a, b, half).astype(BF16)
        vt_ref[0, sl, :] = y[:, nkv + hd * HEAD_DIM:nkv + (hd + 1) * HEAD_DIM].T.astype(BF16)
    nqi = IDX_HEADS * IDX_DIM
    cosi, sini = cosi_ref[...], sini_ref[...]
    for hd in range(IDX_HEADS):
        col = 2 * nkv + hd * IDX_DIM
        if hd % 2 == 0:
            pair_t = y[:, col:col + LANES].T
        qh = pair_t[(hd % 2) * IDX_DIM:(hd % 2 + 1) * IDX_DIM]
        qit_ref[0, hd * IDX_DIM:(hd + 1) * IDX_DIM, :] = _rope_t(qh, cosi, sini).astype(BF16)
    kw = _dot(h, wtail_ref[...])
    ki_ref[...] = _rope(kw, ci_ref[...], ai_ref[...], bi_ref[...], halfi)[:, :IDX_DIM].astype(BF16)
    wit_ref[0] = kw.T[IDX_DIM:IDX_DIM + IDX_HEADS] * (IDX_HEADS ** -0.5 * IDX_DIM ** -0.5)


def _key_to_f32(key):
    bits = jnp.where(key < 0, key ^ jnp.int32(0x7FFFFFFF), key)
    f = pltpu.bitcast(bits, F32)
    return jnp.where(f != f, jnp.inf, f)


def _attn_kernel(qt_ref, qn_ref, qit_ref, wit_ref, k_ref, vt_ref, ki_ref, kgain_ref, o_ref, sc_ref,
                 bias_ref, *, tk, tkf, n_sel):
    t0 = pl.program_id(1) * Q_BLOCK
    nk = (t0 + Q_BLOCK + tk - 1) // tk
    q_pos = t0 + lax.broadcasted_iota(jnp.int32, (1, Q_BLOCK), 1)
    key_end = (lax.shift_right_logical(q_pos, CHUNK.bit_length() - 1) + 1) * CHUNK
    key_row = lax.broadcasted_iota(jnp.int32, (tk, Q_BLOCK), 0)

    qit = qit_ref[0]
    wit = wit_ref[0]
    npair = IDX_HEADS // 2
    rhs = [jnp.concatenate([qit[(2 * p) * IDX_DIM:(2 * p + 1) * IDX_DIM],
                            qit[(2 * p + 1) * IDX_DIM:(2 * p + 2) * IDX_DIM]], axis=1)
           for p in range(npair)]

    def score_tile(kt, carry):
        off = pl.multiple_of(kt * tk, tk)
        ki_t = ki_ref[0, pl.ds(off, tk), :]
        acc = jnp.zeros((tk, Q_BLOCK), F32)
        for p in range(npair):
            d = jnp.maximum(_dot(ki_t, rhs[p]), 0.0)
            acc = acc + d[:, :Q_BLOCK] * wit[2 * p:2 * p + 1, :]
            acc = acc + d[:, Q_BLOCK:] * wit[2 * p + 1:2 * p + 2, :]
        sc_ref[pl.ds(off, tk), :] = jnp.where(key_row < key_end - off, acc, -jnp.inf)
        return carry

    lax.fori_loop(0, nk, score_tile, 0)

    nkf = (t0 + Q_BLOCK + tkf - 1) // tkf

    def pad_tile(kt, carry):
        off = pl.multiple_of(kt * tk, tk)
        bias_ref[pl.ds(off, tk), :] = jnp.full((tk, Q_BLOCK), MASK_BIAS, BF16)
        return carry

    lax.fori_loop(nk, nkf * (tkf // tk), pad_tile, 0)

    def count(pred):
        def body(kt, c):
            off = pl.multiple_of(kt * tk, tk)
            m = jnp.where(pred(sc_ref[pl.ds(off, tk), :], key_row + off), 1.0, 0.0)
            return c + jnp.sum(m.reshape(tk // COUNT_ROWS, COUNT_ROWS, Q_BLOCK), axis=0)

        c = lax.fori_loop(0, nk, body, jnp.zeros((COUNT_ROWS, Q_BLOCK), F32))
        return jnp.sum(c, axis=0, keepdims=True)

    def bit_body(bi, carry):
        key, cnt = carry
        cand = key + lax.shift_left(jnp.int32(1), 31 - bi)
        cand_f = _key_to_f32(cand)
        c = count(lambda x, pos: x >= cand_f)
        ok = c >= n_sel
        return jnp.where(ok, cand, key), jnp.where(ok, c, cnt)

    nbits = jnp.where(t0 + Q_BLOCK <= n_sel, 0, 32)
    key, cnt = lax.fori_loop(0, nbits, bit_body, (jnp.full((1, Q_BLOCK), INT_MIN, jnp.int32),
                                                  jnp.zeros((1, Q_BLOCK), F32)))
    thr = jnp.where(key == INT_MIN, jnp.finfo(F32).min, _key_to_f32(key))

    @pl.when(jnp.max(cnt) > n_sel)
    def _():
        need = n_sel - count(lambda x, pos: x > thr)
        pos_bits = (sc_ref.shape[0] - 1).bit_length() + 1

        def body(bi, end):
            cand = end + lax.shift_left(jnp.int32(1), pos_bits - 1 - bi)
            c = count(lambda x, pos: jnp.logical_and(x == thr, pos < cand))
            return jnp.where(c <= need, cand, end)

        pos_end = lax.fori_loop(0, pos_bits, body, jnp.zeros((1, Q_BLOCK), jnp.int32))

        def retire(kt, carry):
            off = pl.multiple_of(kt * tk, tk)
            x = sc_ref[pl.ds(off, tk), :]
            late_tie = jnp.logical_and(x == thr, key_row + off >= pos_end)
            sc_ref[pl.ds(off, tk), :] = jnp.where(late_tie, -jnp.inf, x)
            return carry

        lax.fori_loop(0, nk, retire, 0)

    def bias_tile(kt, carry):
        off = pl.multiple_of(kt * tk, tk)
        sel = sc_ref[pl.ds(off, tk), :] >= thr
        bias_ref[pl.ds(off, tk), :] = jnp.where(sel, 0.0, MASK_BIAS).astype(BF16)
        return carry

    lax.fori_loop(0, nk, bias_tile, 0)

    qt = qt_ref[0]
    rep = qt.shape[0] // HEAD_DIM // N_KV_HEADS
    cols = rep * Q_BLOCK
    eye = (lax.broadcasted_iota(jnp.int32, (Q_BLOCK, Q_BLOCK), 0)
           == lax.broadcasted_iota(jnp.int32, (Q_BLOCK, Q_BLOCK), 1))
    eye = jnp.where(eye, 1.0, 0.0).astype(BF16)
    ones = jnp.ones((ONES_ROWS, tkf), BF16)
    gsls = [slice(g * HEAD_DIM, (g + 1) * HEAD_DIM) for g in range(N_KV_HEADS)]
    qaugs = []
    for g in range(N_KV_HEADS):
        qg = jnp.concatenate(
            [qt[(g * rep + r) * HEAD_DIM:(g * rep + r + 1) * HEAD_DIM] for r in range(rep)], axis=1)
        qaugs.append(jnp.concatenate([qg, jnp.concatenate([eye] * rep, axis=1)], axis=0))

    def flash(shift):
        def body(kt, accs):
            off = pl.multiple_of(kt * tkf, tkf)
            bias_t = bias_ref[pl.ds(off, tkf), :]
            out = []
            for g in range(N_KV_HEADS):
                kaug = jnp.concatenate([k_ref[0, pl.ds(off, tkf), gsls[g]], bias_t], axis=1)
                vaug = jnp.concatenate([vt_ref[0, gsls[g], pl.ds(off, tkf)], ones], axis=0)
                p = jnp.exp2(_dot(kaug, qaugs[g]) - shift[g]).astype(BF16)
                out.append(accs[g] + _dot(vaug, p))
            return tuple(out)

        zero = jnp.zeros((HEAD_DIM + ONES_ROWS, cols), F32)
        return lax.fori_loop(0, nkf, body, (zero,) * N_KV_HEADS)

    def column_max():
        def body(kt, ms):
            off = pl.multiple_of(kt * tkf, tkf)
            bias_t = bias_ref[pl.ds(off, tkf), :]
            out = []
            for g in range(N_KV_HEADS):
                kaug = jnp.concatenate([k_ref[0, pl.ds(off, tkf), gsls[g]], bias_t], axis=1)
                out.append(jnp.maximum(ms[g], jnp.max(_dot(kaug, qaugs[g]), axis=0, keepdims=True)))
            return tuple(out)

        return lax.fori_loop(0, nkf, body, (jnp.full((1, cols), MASK_BIAS, F32),) * N_KV_HEADS)

    def write(accs):
        for g in range(N_KV_HEADS):
            o = accs[g][:HEAD_DIM] / accs[g][HEAD_DIM:HEAD_DIM + 1]
            for r in range(rep):
                hsl = slice((g * rep + r) * HEAD_DIM, (g * rep + r + 1) * HEAD_DIM)
                o_ref[0, :, hsl] = o[:, r * Q_BLOCK:(r + 1) * Q_BLOCK].T.astype(BF16)

    kmax = (HEAD_DIM ** 0.5) * jnp.max(jnp.abs(kgain_ref[...]), axis=1, keepdims=True)
    qn = qn_ref[0]
    bound = [jnp.concatenate([qn[g * rep + r:g * rep + r + 1] for r in range(rep)], axis=1) * kmax
             for g in range(N_KV_HEADS)]
    accs = flash(bound)
    den = jnp.concatenate([acc[HEAD_DIM:HEAD_DIM + 1] for acc in accs], axis=1)
    safe = jnp.logical_and(jnp.min(den) >= DEN_MIN, jnp.max(den) <= 1.0 / DEN_MIN)

    @pl.when(safe)
    def _():
        write(accs)

    @pl.when(jnp.logical_not(safe))
    def _():
        write(flash(column_max()))


def _sparse_attention(x, g, w_in, q_gain, k_gain, w_o, layer):
    b, s, d = x.shape
    n = b * s
    x2 = x.reshape(n, d)
    nq = d
    nkv = N_KV_HEADS * HEAD_DIM
    nqi = IDX_HEADS * IDX_DIM
    nmid = 2 * nkv + nqi
    assert nmid == nq
    pad = LANES - IDX_DIM - IDX_HEADS
    w_tail = jnp.pad(w_in[layer, :, nq + nmid:], ((0, 0), (0, pad))).astype(BF16)
    w_in_t = jnp.swapaxes(w_in, 1, 2)
    wblock = lambda blk: pl.BlockSpec((None, nq, d), lambda i: (layer, blk, 0),
                                      pipeline_mode=pl.Buffered(1))
    c, a, bt = _rope_tables(s, HEAD_DIM, HEAD_DIM // ROT_FRACTION)
    ci, ai, bi = _rope_tables(s, IDX_DIM, IDX_DIM // ROT_FRACTION)
    cos_t, sin_t = _rope_tables_t(s, HEAD_DIM // ROT_FRACTION)
    cosi_t, sini_t = _rope_tables_t(s, IDX_DIM // ROT_FRACTION)
    q_gain_b = jnp.broadcast_to(q_gain[:, None], (HEAD_DIM, LANES))

    tm = min(PROJ_ROWS, s)
    nt = s // tm
    row = lambda i: (i, 0)
    pos = lambda i: (i % nt, 0)
    tcol = lambda i: (i // nt, 0, i % nt)
    tab = pl.BlockSpec((tm, LANES), pos)
    tab_t = lambda half: pl.BlockSpec((half, tm), lambda i: (0, i % nt))
    half = HEAD_DIM // ROT_FRACTION // 2
    halfi = IDX_DIM // ROT_FRACTION // 2
    qt, qn = pl.pallas_call(
        _q_proj_kernel,
        grid=(n // tm,),
        in_specs=[pl.BlockSpec((tm, d), row), _resident((1, d)), wblock(0),
                  _resident((HEAD_DIM, LANES)), tab_t(half), tab_t(half)],
        out_specs=[pl.BlockSpec((1, nq, tm), tcol), pl.BlockSpec((1, nq // HEAD_DIM, tm), tcol)],
        out_shape=[jax.ShapeDtypeStruct((b, nq, s), BF16),
                   jax.ShapeDtypeStruct((b, nq // HEAD_DIM, s), F32)],
        scratch_shapes=[pltpu.VMEM((d, nq), BF16)],
        compiler_params=_params("arbitrary"),
        name="attn_q_proj",
    )(x2, g.reshape(1, d), w_in_t, q_gain_b, cos_t, sin_t)

    k, vt, qit, ki, wit = pl.pallas_call(
        _kv_proj_kernel,
        grid=(n // tm,),
        in_specs=[pl.BlockSpec((tm, d), row), _resident((1, d)), wblock(1), _resident((d, LANES)),
                  _resident((1, HEAD_DIM)), tab, tab, tab, tab, tab, tab, tab_t(halfi), tab_t(halfi)],
        out_specs=[pl.BlockSpec((tm, nkv), row), pl.BlockSpec((1, nkv, tm), tcol),
                   pl.BlockSpec((1, nqi, tm), tcol), pl.BlockSpec((tm, IDX_DIM), row),
                   pl.BlockSpec((1, IDX_HEADS, tm), tcol)],
        out_shape=[jax.ShapeDtypeStruct((n, nkv), BF16), jax.ShapeDtypeStruct((b, nkv, s), BF16),
                   jax.ShapeDtypeStruct((b, nqi, s), BF16), jax.ShapeDtypeStruct((n, IDX_DIM), BF16),
                   jax.ShapeDtypeStruct((b, IDX_HEADS, s), F32)],
        scratch_shapes=[pltpu.VMEM((d, nmid), BF16)],
        compiler_params=_params("arbitrary"),
        name="attn_kv_proj",
    )(x2, g.reshape(1, d), w_in_t, w_tail, k_gain.reshape(1, HEAD_DIM), c, a, bt, ci, ai, bi, cosi_t,
      sini_t)

    n_sel = min(INDEX_TOPK, s // 4)
    tk = min(SCORE_KEYS, s)
    qcol = lambda bi_, qb: (bi_, 0, qb)
    full = lambda bi_, qb: (bi_, 0, 0)
    o = pl.pallas_call(
        functools.partial(_attn_kernel, tk=tk, tkf=min(FLASH_KEYS, s), n_sel=n_sel),
        grid=(b, s // Q_BLOCK),
        in_specs=[pl.BlockSpec((1, nq, Q_BLOCK), qcol), pl.BlockSpec((1, nq // HEAD_DIM, Q_BLOCK), qcol),
                  pl.BlockSpec((1, nqi, Q_BLOCK), qcol),
                  pl.BlockSpec((1, IDX_HEADS, Q_BLOCK), qcol), pl.BlockSpec((1, s, nkv), full),
                  pl.BlockSpec((1, nkv, s), full), pl.BlockSpec((1, s, IDX_DIM), full),
                  pl.BlockSpec((1, HEAD_DIM), lambda bi_, qb: (0, 0))],
        out_specs=pl.BlockSpec((1, Q_BLOCK, nq), lambda bi_, qb: (bi_, qb, 0)),
        out_shape=jax.ShapeDtypeStruct((b, s, nq), BF16),
        scratch_shapes=[pltpu.VMEM((s, Q_BLOCK), F32), pltpu.VMEM((s, Q_BLOCK), BF16)],
        compiler_params=_params("parallel", "arbitrary"),
        name="sparse_attn",
    )(qt, qn, qit, wit, k.reshape(b, s, nkv), vt, ki.reshape(b, s, IDX_DIM),
      k_gain.reshape(1, HEAD_DIM))
    return _proj_res(x2, o.reshape(n, nq), w_o, layer).reshape(b, s, d)


def _sgu_kernel(x_ref, g_ref, w_ref, b_ref, vg_ref, ws_ref, bs_ref, o_ref, *, tm, sub):
    ii = lax.broadcasted_iota(jnp.int32, (SGU_BLOCK, SGU_BLOCK), 0) // CHUNK
    jj = lax.broadcasted_iota(jnp.int32, (SGU_BLOCK, SGU_BLOCK), 1) // CHUNK
    causal = jj <= ii
    ws = [jnp.where(causal, ws_ref[gi], 0.0).astype(BF16) for gi in range(SGU_GROUPS)]
    bs = bs_ref[...]
    width = w_ref.shape[1] // 2
    gd = width // SGU_GROUPS
    for st in range(tm // sub):
        r0 = st * sub
        h = _rms(x_ref[r0:r0 + sub, :], g_ref[...]).astype(BF16)
        z = _dot(h, w_ref[...]) + b_ref[...]
        z = 0.5 * z * (1.0 + lax.erf(z * (2.0 ** -0.5)))
        u = z[:, :width]
        v = _rms(z[:, width:], vg_ref[...]).astype(BF16)
        for gi in range(SGU_GROUPS):
            cs = slice(gi * gd, (gi + 1) * gd)
            for nb in range(sub // SGU_BLOCK):
                rs = slice(nb * SGU_BLOCK, (nb + 1) * SGU_BLOCK)
                mixed = _dot(ws[gi], v[rs, cs]) + bs[:, gi:gi + 1]
                o_ref[r0 + nb * SGU_BLOCK:r0 + (nb + 1) * SGU_BLOCK, cs] = (u[rs, cs] * mixed).astype(BF16)


def _spatial_gating(x, g, w_in, b_in, v_gain, w_s, b_s, w_o, layer):
    b, s, d = x.shape
    n = b * s
    x2 = x.reshape(n, d)
    width = w_in.shape[1] // 2
    tm = min(SGU_ROWS, s)
    row = lambda i: (i, 0)
    gated = pl.pallas_call(
        functools.partial(_sgu_kernel, tm=tm, sub=min(SGU_SUB_ROWS, tm)),
        grid=(n // tm,),
        in_specs=[pl.BlockSpec((tm, d), row), _resident((1, d)), _resident((d, 2 * width)),
                  _resident((1, 2 * width)), _resident((1, width)),
                  _resident((SGU_GROUPS, SGU_BLOCK, SGU_BLOCK)), _resident((SGU_BLOCK, SGU_GROUPS))],
        out_specs=pl.BlockSpec((tm, width), row),
        out_shape=jax.ShapeDtypeStruct((n, width), BF16),
        compiler_params=_params("parallel"),
        name="sgu_gate",
    )(x2, g.reshape(1, d), w_in.astype(BF16), b_in.reshape(1, 2 * width), v_gain.reshape(1, width),
      w_s, b_s.T)
    return _proj_res(x2, gated, w_o, layer).reshape(b, s, d)


def kernel(x, norm_mix, norm_ffn, pool_w, pool_scale, attn_w_in, attn_q_gain, attn_k_gain, attn_w_o,
           sgu_w_in, sgu_b_in, sgu_v_gain, sgu_w_s, sgu_b_s, sgu_w_o, ffn_w_up, ffn_w_down):
    b, s, d = x.shape
    depth = norm_mix.shape[0]
    for i in range(depth):
        kind, j = i % 3, i // 3
        if kind == 0:
            x = _pool_mixer(x, norm_mix[i], pool_w[j].astype(BF16), pool_scale[j])
        elif kind == 1:
            x = _sparse_attention(x, norm_mix[i], attn_w_in, attn_q_gain[j], attn_k_gain[j],
                                  attn_w_o, j)
        else:
            x = _spatial_gating(x, norm_mix[i], sgu_w_in[j], sgu_b_in[j], sgu_v_gain[j], sgu_w_s[j],
                                sgu_b_s[j], sgu_w_o, j)
        x = _ffn(x.reshape(b * s, d), norm_ffn[i], ffn_w_up, ffn_w_down, i).reshape(b, s, d)
    return x
```

```python
import functools

import jax
import jax.numpy as jnp
from jax import lax
from jax.experimental import pallas as pl
from jax.experimental.pallas import tpu as pltpu

EPS = 1e-6
CHUNK = 64
POOL_WINDOWS = (2, 4, 8, 16)
POOL_HALO = 16
HEAD_DIM = 128
N_KV_HEADS = 4
IDX_HEADS = 16
IDX_DIM = 64
INDEX_TOPK = 256
Q_BLOCK = 128
ROPE_THETA = 500000.0
ROT_FRACTION = 4
SGU_BLOCK = 128
SGU_GROUPS = 8
LANES = 128
MXU_COLS = 256
assert CHUNK & (CHUNK - 1) == 0
INT_MIN = -(2 ** 31)
COUNT_ROWS = 64
ONES_ROWS = 16
DEN_MIN = 2.0 ** -60
LOG2E = 1.4426950408889634
MASK_BIAS = -1e30
VMEM_LIMIT_BYTES = 60 * 1024 * 1024

POOL_ROWS = 1024
FFN_ROWS = 1024
FFN_COLS = 512
PROJ_ROWS = 512
SGU_ROWS = 512
SGU_SUB_ROWS = 256
SCORE_KEYS = 512
FLASH_KEYS = 1024

F32 = jnp.float32
BF16 = jnp.bfloat16


def _params(*sem):
    return pltpu.CompilerParams(dimension_semantics=sem, vmem_limit_bytes=VMEM_LIMIT_BYTES)


def _resident(shape):
    nd = len(shape)
    return pl.BlockSpec(shape, lambda *_: (0,) * nd, pipeline_mode=pl.Buffered(1))


def _cast_once(w_ref, wbf_ref, transpose=False):
    @pl.when(pl.program_id(0) == 0)
    def _():
        if transpose:
            for r in range(0, w_ref.shape[0], LANES):
                wbf_ref[:, r:r + LANES] = w_ref[r:r + LANES, :].T.astype(BF16)
        else:
            wbf_ref[...] = w_ref[...].astype(BF16)


def _rms(xf, g):
    ms = jnp.mean(xf * xf, axis=-1, keepdims=True)
    return xf * lax.rsqrt(ms + EPS) * g


def _dot(a, b):
    return jnp.dot(a, b, preferred_element_type=F32)


def _pool_kernel(x_ref, halo_ref, g_ref, w_ref, scale_ref, o_ref, *, ts):
    i = pl.program_id(1)
    x = x_ref[0]
    g = g_ref[...]
    h = _rms(x, g)
    hh = _rms(halo_ref[0], g)
    hh = jnp.where(i > 0, hh, 0.0)
    hf = jnp.concatenate([hh, h], axis=0)
    t1 = (i * ts + lax.broadcasted_iota(jnp.int32, (ts, 1), 0) + 1).astype(F32)
    cg = x.shape[1] // len(POOL_WINDOWS)
    for gi, w in enumerate(POOL_WINDOWS):
        sl = slice(gi * cg, (gi + 1) * cg)
        s = hf[:, sl]
        k = 1
        while k < w:
            s = s + pltpu.roll(s, k, 0)
            k *= 2
        mean = s[POOL_HALO:] / jnp.minimum(t1, float(w))
        p = (mean - h[:, sl]).astype(BF16)
        y = _dot(p, w_ref[gi]) * scale_ref[:, sl]
        o_ref[0, :, sl] = x[:, sl] + y


def _pool_mixer(x, g, w_bf, scale):
    b, s, d = x.shape
    ts = min(POOL_ROWS, s)
    hb = ts // POOL_HALO
    ng = len(POOL_WINDOWS)
    return pl.pallas_call(
        functools.partial(_pool_kernel, ts=ts),
        grid=(b, s // ts),
        in_specs=[
            pl.BlockSpec((1, ts, d), lambda bi, i: (bi, i, 0)),
            pl.BlockSpec((1, POOL_HALO, d), lambda bi, i: (bi, jnp.maximum(i * hb - 1, 0), 0)),
            _resident((1, d)),
            _resident((ng, d // ng, d // ng)),
            _resident((1, d)),
        ],
        out_specs=pl.BlockSpec((1, ts, d), lambda bi, i: (bi, i, 0)),
        out_shape=jax.ShapeDtypeStruct((b, s, d), F32),
        compiler_params=_params("parallel", "parallel"),
        name="pool_mixer",
    )(x, x, g.reshape(1, d), w_bf, scale.reshape(1, d))


def _ffn_kernel(x_ref, g_ref, wu_ref, wd_ref, o_ref, h_ref):
    j = pl.program_id(1)

    @pl.when(j == 0)
    def _():
        x = x_ref[...]
        h_ref[...] = _rms(x, g_ref[...]).astype(BF16)
        o_ref[...] = x

    u = _dot(h_ref[...], wu_ref[...].astype(BF16))
    a = jnp.square(jnp.maximum(u, 0.0)).astype(BF16)
    o_ref[...] += _dot(a, wd_ref[...].astype(BF16))


def _ffn(x2, g, w_up, w_down, layer):
    n, d = x2.shape
    f = w_up.shape[2]
    tm = min(FFN_ROWS, n)
    tf = FFN_COLS
    return pl.pallas_call(
        _ffn_kernel,
        grid=(n // tm, f // tf),
        in_specs=[
            pl.BlockSpec((tm, d), lambda i, j: (i, 0)),
            _resident((1, d)),
            pl.BlockSpec((None, d, tf), lambda i, j: (layer, 0, j)),
            pl.BlockSpec((None, tf, d), lambda i, j: (layer, j, 0)),
        ],
        out_specs=pl.BlockSpec((tm, d), lambda i, j: (i, 0)),
        out_shape=jax.ShapeDtypeStruct((n, d), F32),
        scratch_shapes=[pltpu.VMEM((tm, d), BF16)],
        compiler_params=_params("parallel", "arbitrary"),
        name="ffn",
    )(x2, g.reshape(1, d), w_up, w_down)


def _proj_res_kernel(x_ref, a_ref, w_ref, o_ref, wbf_ref, *, a_transposed):
    _cast_once(w_ref, wbf_ref)
    if a_transposed:
        y = lax.dot_general(a_ref[0], wbf_ref[...], (((0,), (0,)), ((), ())),
                            preferred_element_type=F32)
    else:
        y = _dot(a_ref[...], wbf_ref[...])
    o_ref[...] = x_ref[...] + y


def _proj_res(x2, a_bf, w, layer):
    n, d = x2.shape
    a_transposed = a_bf.ndim == 3
    kdim = a_bf.shape[1]
    tm = min(PROJ_ROWS, a_bf.shape[2] if a_transposed else n)
    if a_transposed:
        nt = a_bf.shape[2] // tm
        a_spec = pl.BlockSpec((1, kdim, tm), lambda i: (i // nt, 0, i % nt))
    else:
        a_spec = pl.BlockSpec((tm, kdim), lambda i: (i, 0))
    return pl.pallas_call(
        functools.partial(_proj_res_kernel, a_transposed=a_transposed),
        grid=(n // tm,),
        in_specs=[
            pl.BlockSpec((tm, d), lambda i: (i, 0)),
            a_spec,
            pl.BlockSpec((None, kdim, d), lambda i: (layer, 0, 0), pipeline_mode=pl.Buffered(1)),
        ],
        out_specs=pl.BlockSpec((tm, d), lambda i: (i, 0)),
        out_shape=jax.ShapeDtypeStruct((n, d), F32),
        scratch_shapes=[pltpu.VMEM((kdim, d), BF16)],
        compiler_params=_params("arbitrary"),
        name="proj_res",
    )(x2, a_bf, w)


def _rope_tables(s, width, rot):
    half = rot // 2
    inv = ROPE_THETA ** (-jnp.arange(half, dtype=F32) / half)
    ang = jnp.arange(s, dtype=F32)[:, None] * inv[None, :]
    cos, sin = jnp.cos(ang), jnp.sin(ang)
    pad = jnp.zeros((s, width - rot), F32)
    zero = jnp.zeros((s, half), F32)
    c = jnp.concatenate([cos, cos, pad + 1.0], axis=1)
    a = jnp.concatenate([-sin, zero, pad], axis=1)
    b = jnp.concatenate([zero, sin, pad], axis=1)
    rep = LANES // width
    return tuple(jnp.tile(t, (1, rep)) for t in (c, a, b))


def _rope_tables_t(s, rot):
    half = rot // 2
    inv = ROPE_THETA ** (-jnp.arange(half, dtype=F32) / half)
    ang = inv[:, None] * jnp.arange(s, dtype=F32)[None, :]
    return jnp.cos(ang), jnp.sin(ang)


def _rope(x, c, a, b, half):
    return x * c + pltpu.roll(x, LANES - half, 1) * a + pltpu.roll(x, half, 1) * b


def _rope_t(xt, cos, sin):
    half = cos.shape[0]
    x1, x2 = xt[:half], xt[half:2 * half]
    return jnp.concatenate([x1 * cos - x2 * sin, x2 * cos + x1 * sin, xt[2 * half:]], axis=0)


def _q_proj_kernel(x_ref, g_ref, w_ref, gain_ref, cos_ref, sin_ref, qt_ref, qn_ref, wbf_ref):
    _cast_once(w_ref, wbf_ref, transpose=True)
    h = _rms(x_ref[...], g_ref[...]).astype(BF16)
    tm = h.shape[0]
    gain = jnp.concatenate([gain_ref[...]] * (tm // LANES), axis=1)
    cos, sin = cos_ref[...], sin_ref[...]
    for pair in range(wbf_ref.shape[1] // MXU_COLS):
        q = _dot(h, wbf_ref[:, pair * MXU_COLS:(pair + 1) * MXU_COLS])
        for hd in range(MXU_COLS // HEAD_DIM):
            qh = q[:, hd * HEAD_DIM:(hd + 1) * HEAD_DIM].T
            qh = qh * lax.rsqrt(jnp.mean(qh * qh, axis=0, keepdims=True) + EPS) * gain
            qh = _rope_t(qh, cos, sin) * (HEAD_DIM ** -0.5 * LOG2E)
            row = pair * MXU_COLS + hd * HEAD_DIM
            qt_ref[0, row:row + HEAD_DIM, :] = qh.astype(BF16)
            head = row // HEAD_DIM
            qn_ref[0, head:head + 1, :] = jnp.sqrt(jnp.sum(qh * qh, axis=0, keepdims=True))


def _kv_proj_kernel(x_ref, g_ref, w_ref, wtail_ref, gain_ref, c_ref, a_ref, b_ref, ci_ref, ai_ref,
                    bi_ref, cosi_ref, sini_ref, k_ref, vt_ref, qit_ref, ki_ref, wit_ref, wbf_ref):
    _cast_once(w_ref, wbf_ref, transpose=True)
    h = _rms(x_ref[...], g_ref[...]).astype(BF16)
    y = _dot(h, wbf_ref[...])
    gain = gain_ref[...]
    c, a, b = c_ref[...], a_ref[...], b_ref[...]
    half = HEAD_DIM // ROT_FRACTION // 2
    halfi = IDX_DIM // ROT_FRACTION // 2
    nkv = N_KV_HEADS * HEAD_DIM
    for hd in range(N_KV_HEADS):
        sl = slice(hd * HEAD_DIM, (hd + 1) * HEAD_DIM)
        k_ref[:, sl] = _rope(_rms(y[:, sl], gain), c, a, b, half).astype(BF16)
        vt_ref[0, sl, :] = y[:, nkv + hd * HEAD_DIM:nkv + (hd + 1) * HEAD_DIM].T.astype(BF16)
    nqi = IDX_HEADS * IDX_DIM
    cosi, sini = cosi_ref[...], sini_ref[...]
    for hd in range(IDX_HEADS):
        col = 2 * nkv + hd * IDX_DIM
        if hd % 2 == 0:
            pair_t = y[:, col:col + LANES].T
        qh = pair_t[(hd % 2) * IDX_DIM:(hd % 2 + 1) * IDX_DIM]
        qit_ref[0, hd * IDX_DIM:(hd + 1) * IDX_DIM, :] = _rope_t(qh, cosi, sini).astype(BF16)
    kw = _dot(h, wtail_ref[...])
    ki_ref[...] = _rope(kw, ci_ref[...], ai_ref[...], bi_ref[...], halfi)[:, :IDX_DIM].astype(BF16)
    wit_ref[0] = kw.T[IDX_DIM:IDX_DIM + IDX_HEADS] * (IDX_HEADS ** -0.5 * IDX_DIM ** -0.5)


def _key_to_f32(key):
    bits = jnp.where(key < 0, key ^ jnp.int32(0x7FFFFFFF), key)
    f = pltpu.bitcast(bits, F32)
    return jnp.where(f != f, jnp.inf, f)


def _attn_kernel(qt_ref, qn_ref, qit_ref, wit_ref, k_ref, vt_ref, ki_ref, kgain_ref, o_ref, sc_ref,
                 bias_ref, *, tk, tkf, n_sel):
    t0 = pl.program_id(1) * Q_BLOCK
    nk = (t0 + Q_BLOCK + tk - 1) // tk
    q_pos = t0 + lax.broadcasted_iota(jnp.int32, (1, Q_BLOCK), 1)
    key_end = (lax.shift_right_logical(q_pos, CHUNK.bit_length() - 1) + 1) * CHUNK
    key_row = lax.broadcasted_iota(jnp.int32, (tk, Q_BLOCK), 0)

    qit = qit_ref[0]
    wit = wit_ref[0]
    npair = IDX_HEADS // 2
    rhs = [jnp.concatenate([qit[(2 * p) * IDX_DIM:(2 * p + 1) * IDX_DIM],
                            qit[(2 * p + 1) * IDX_DIM:(2 * p + 2) * IDX_DIM]], axis=1)
           for p in range(npair)]

    def score_tile(kt, carry):
        off = pl.multiple_of(kt * tk, tk)
        ki_t = ki_ref[0, pl.ds(off, tk), :]
        acc = jnp.zeros((tk, Q_BLOCK), F32)
        for p in range(npair):
            d = jnp.maximum(_dot(ki_t, rhs[p]), 0.0)
            acc = acc + d[:, :Q_BLOCK] * wit[2 * p:2 * p + 1, :]
            acc = acc + d[:, Q_BLOCK:] * wit[2 * p + 1:2 * p + 2, :]
        sc_ref[pl.ds(off, tk), :] = jnp.where(key_row < key_end - off, acc, -jnp.inf)
        return carry

    lax.fori_loop(0, nk, score_tile, 0)

    def count(pred):
        def body(kt, c):
            off = pl.multiple_of(kt * tk, tk)
            m = jnp.where(pred(sc_ref[pl.ds(off, tk), :], key_row + off), 1.0, 0.0)
            return c + jnp.sum(m.reshape(tk // COUNT_ROWS, COUNT_ROWS, Q_BLOCK), axis=0)

        c = lax.fori_loop(0, nk, body, jnp.zeros((COUNT_ROWS, Q_BLOCK), F32))
        return jnp.sum(c, axis=0, keepdims=True)

    def bit_body(bi, carry):
        key, cnt = carry
        cand = key + lax.shift_left(jnp.int32(1), 31 - bi)
        cand_f = _key_to_f32(cand)
        c = count(lambda x, pos: x >= cand_f)
        ok = c >= n_sel
        return jnp.where(ok, cand, key), jnp.where(ok, c, cnt)

    nbits = jnp.where(t0 + Q_BLOCK <= n_sel, 0, 32)
    key, cnt = lax.fori_loop(0, nbits, bit_body, (jnp.full((1, Q_BLOCK), INT_MIN, jnp.int32),
                                                  jnp.zeros((1, Q_BLOCK), F32)))
    thr = jnp.where(key == INT_MIN, jnp.finfo(F32).min, _key_to_f32(key))

    @pl.when(jnp.max(cnt) > n_sel)
    def _():
        need = n_sel - count(lambda x, pos: x > thr)
        pos_bits = (sc_ref.shape[0] - 1).bit_length() + 1

        def body(bi, end):
            cand = end + lax.shift_left(jnp.int32(1), pos_bits - 1 - bi)
            c = count(lambda x, pos: jnp.logical_and(x == thr, pos < cand))
            return jnp.where(c <= need, cand, end)

        pos_end = lax.fori_loop(0, pos_bits, body, jnp.zeros((1, Q_BLOCK), jnp.int32))

        def retire(kt, carry):
            off = pl.multiple_of(kt * tk, tk)
            x = sc_ref[pl.ds(off, tk), :]
            late_tie = jnp.logical_and(x == thr, key_row + off >= pos_end)
            sc_ref[pl.ds(off, tk), :] = jnp.where(late_tie, -jnp.inf, x)
            return carry

        lax.fori_loop(0, nk, retire, 0)

    def bias_tile(kt, carry):
        off = pl.multiple_of(kt * tk, tk)
        sel = sc_ref[pl.ds(off, tk), :] >= thr
        bias_ref[pl.ds(off, tk), :] = jnp.where(sel, 0.0, MASK_BIAS).astype(BF16)
        return carry

    lax.fori_loop(0, nk, bias_tile, 0)

    qt = qt_ref[0]
    rep = qt.shape[0] // HEAD_DIM // N_KV_HEADS
    cols = rep * Q_BLOCK
    eye = (lax.broadcasted_iota(jnp.int32, (Q_BLOCK, Q_BLOCK), 0)
           == lax.broadcasted_iota(jnp.int32, (Q_BLOCK, Q_BLOCK), 1))
    eye = jnp.where(eye, 1.0, 0.0).astype(BF16)
    gsls = [slice(g * HEAD_DIM, (g + 1) * HEAD_DIM) for g in range(N_KV_HEADS)]
    qaugs = []
    for g in range(N_KV_HEADS):
        qg = jnp.concatenate(
            [qt[(g * rep + r) * HEAD_DIM:(g * rep + r + 1) * HEAD_DIM] for r in range(rep)], axis=1)
        qaugs.append(jnp.concatenate([qg, jnp.concatenate([eye] * rep, axis=1)], axis=0))

    def over_key_tiles(step, init):
        if tkf != 2 * tk:
            return lax.fori_loop(0, nk, lambda kt, c: step(pl.multiple_of(kt * tk, tk), tk, c), init)
        carry = lax.fori_loop(
            0, nk // 2, lambda kt, c: step(pl.multiple_of(kt * tkf, tkf), tkf, c), init)
        return lax.fori_loop(
            0, nk % 2, lambda _, c: step(pl.multiple_of((nk // 2) * tkf, tk), tk, c), carry)

    def flash(shift):
        def step(off, size, accs):
            bias_t = bias_ref[pl.ds(off, size), :]
            ones = jnp.ones((ONES_ROWS, size), BF16)
            out = []
            for g in range(N_KV_HEADS):
                kaug = jnp.concatenate([k_ref[0, pl.ds(off, size), gsls[g]], bias_t], axis=1)
                vaug = jnp.concatenate([vt_ref[0, gsls[g], pl.ds(off, size)], ones], axis=0)
                p = jnp.exp2(_dot(kaug, qaugs[g]) - shift[g]).astype(BF16)
                out.append(accs[g] + _dot(vaug, p))
            return tuple(out)

        zero = jnp.zeros((HEAD_DIM + ONES_ROWS, cols), F32)
        return over_key_tiles(step, (zero,) * N_KV_HEADS)

    def column_max():
        def step(off, size, ms):
            bias_t = bias_ref[pl.ds(off, size), :]
            out = []
            for g in range(N_KV_HEADS):
                kaug = jnp.concatenate([k_ref[0, pl.ds(off, size), gsls[g]], bias_t], axis=1)
                out.append(jnp.maximum(ms[g], jnp.max(_dot(kaug, qaugs[g]), axis=0, keepdims=True)))
            return tuple(out)

        return over_key_tiles(step, (jnp.full((1, cols), MASK_BIAS, F32),) * N_KV_HEADS)

    def write(accs):
        for g in range(N_KV_HEADS):
            o = accs[g][:HEAD_DIM] / accs[g][HEAD_DIM:HEAD_DIM + 1]
            for r in range(rep):
                hsl = slice((g * rep + r) * HEAD_DIM, (g * rep + r + 1) * HEAD_DIM)
                o_ref[0, hsl, :] = o[:, r * Q_BLOCK:(r + 1) * Q_BLOCK].astype(BF16)

    kmax = (HEAD_DIM ** 0.5) * jnp.max(jnp.abs(kgain_ref[...]), axis=1, keepdims=True)
    qn = qn_ref[0]
    bound = [jnp.concatenate([qn[g * rep + r:g * rep + r + 1] for r in range(rep)], axis=1) * kmax
             for g in range(N_KV_HEADS)]
    accs = flash(bound)
    den = jnp.concatenate([acc[HEAD_DIM:HEAD_DIM + 1] for acc in accs], axis=1)
    safe = jnp.logical_and(jnp.min(den) >= DEN_MIN, jnp.max(den) <= 1.0 / DEN_MIN)

    @pl.when(safe)
    def _():
        write(accs)

    @pl.when(jnp.logical_not(safe))
    def _():
        write(flash(column_max()))


def _sparse_attention(x, g, w_in, q_gain, k_gain, w_o, layer):
    b, s, d = x.shape
    n = b * s
    x2 = x.reshape(n, d)
    nq = d
    nkv = N_KV_HEADS * HEAD_DIM
    nqi = IDX_HEADS * IDX_DIM
    nmid = 2 * nkv + nqi
    assert nmid == nq
    pad = LANES - IDX_DIM - IDX_HEADS
    w_tail = jnp.pad(w_in[layer, :, nq + nmid:], ((0, 0), (0, pad))).astype(BF16)
    w_in_t = jnp.swapaxes(w_in, 1, 2)
    wblock = lambda blk: pl.BlockSpec((None, nq, d), lambda i: (layer, blk, 0),
                                      pipeline_mode=pl.Buffered(1))
    c, a, bt = _rope_tables(s, HEAD_DIM, HEAD_DIM // ROT_FRACTION)
    ci, ai, bi = _rope_tables(s, IDX_DIM, IDX_DIM // ROT_FRACTION)
    cos_t, sin_t = _rope_tables_t(s, HEAD_DIM // ROT_FRACTION)
    cosi_t, sini_t = _rope_tables_t(s, IDX_DIM // ROT_FRACTION)
    q_gain_b = jnp.broadcast_to(q_gain[:, None], (HEAD_DIM, LANES))

    tm = min(PROJ_ROWS, s)
    nt = s // tm
    row = lambda i: (i, 0)
    pos = lambda i: (i % nt, 0)
    tcol = lambda i: (i // nt, 0, i % nt)
    tab = pl.BlockSpec((tm, LANES), pos)
    tab_t = lambda half: pl.BlockSpec((half, tm), lambda i: (0, i % nt))
    half = HEAD_DIM // ROT_FRACTION // 2
    halfi = IDX_DIM // ROT_FRACTION // 2
    qt, qn = pl.pallas_call(
        _q_proj_kernel,
        grid=(n // tm,),
        in_specs=[pl.BlockSpec((tm, d), row), _resident((1, d)), wblock(0),
                  _resident((HEAD_DIM, LANES)), tab_t(half), tab_t(half)],
        out_specs=[pl.BlockSpec((1, nq, tm), tcol), pl.BlockSpec((1, nq // HEAD_DIM, tm), tcol)],
        out_shape=[jax.ShapeDtypeStruct((b, nq, s), BF16),
                   jax.ShapeDtypeStruct((b, nq // HEAD_DIM, s), F32)],
        scratch_shapes=[pltpu.VMEM((d, nq), BF16)],
        compiler_params=_params("arbitrary"),
        name="attn_q_proj",
    )(x2, g.reshape(1, d), w_in_t, q_gain_b, cos_t, sin_t)

    k, vt, qit, ki, wit = pl.pallas_call(
        _kv_proj_kernel,
        grid=(n // tm,),
        in_specs=[pl.BlockSpec((tm, d), row), _resident((1, d)), wblock(1), _resident((d, LANES)),
                  _resident((1, HEAD_DIM)), tab, tab, tab, tab, tab, tab, tab_t(halfi), tab_t(halfi)],
        out_specs=[pl.BlockSpec((tm, nkv), row), pl.BlockSpec((1, nkv, tm), tcol),
                   pl.BlockSpec((1, nqi, tm), tcol), pl.BlockSpec((tm, IDX_DIM), row),
                   pl.BlockSpec((1, IDX_HEADS, tm), tcol)],
        out_shape=[jax.ShapeDtypeStruct((n, nkv), BF16), jax.ShapeDtypeStruct((b, nkv, s), BF16),
                   jax.ShapeDtypeStruct((b, nqi, s), BF16), jax.ShapeDtypeStruct((n, IDX_DIM), BF16),
                   jax.ShapeDtypeStruct((b, IDX_HEADS, s), F32)],
        scratch_shapes=[pltpu.VMEM((d, nmid), BF16)],
        compiler_params=_params("arbitrary"),
        name="attn_kv_proj",
    )(x2, g.reshape(1, d), w_in_t, w_tail, k_gain.reshape(1, HEAD_DIM), c, a, bt, ci, ai, bi, cosi_t,
      sini_t)

    n_sel = min(INDEX_TOPK, s // 4)
    tk = min(SCORE_KEYS, s)
    qcol = lambda bi_, qb: (bi_, 0, qb)
    full = lambda bi_, qb: (bi_, 0, 0)
    o = pl.pallas_call(
        functools.partial(_attn_kernel, tk=tk, tkf=min(FLASH_KEYS, s), n_sel=n_sel),
        grid=(b, s // Q_BLOCK),
        in_specs=[pl.BlockSpec((1, nq, Q_BLOCK), qcol), pl.BlockSpec((1, nq // HEAD_DIM, Q_BLOCK), qcol),
                  pl.BlockSpec((1, nqi, Q_BLOCK), qcol),
                  pl.BlockSpec((1, IDX_HEADS, Q_BLOCK), qcol), pl.BlockSpec((1, s, nkv), full),
                  pl.BlockSpec((1, nkv, s), full), pl.BlockSpec((1, s, IDX_DIM), full),
                  pl.BlockSpec((1, HEAD_DIM), lambda bi_, qb: (0, 0))],
        out_specs=pl.BlockSpec((1, nq, Q_BLOCK), qcol),
        out_shape=jax.ShapeDtypeStruct((b, nq, s), BF16),
        scratch_shapes=[pltpu.VMEM((s, Q_BLOCK), F32), pltpu.VMEM((s, Q_BLOCK), BF16)],
        compiler_params=_params("parallel", "arbitrary"),
        name="sparse_attn",
    )(qt, qn, qit, wit, k.reshape(b, s, nkv), vt, ki.reshape(b, s, IDX_DIM),
      k_gain.reshape(1, HEAD_DIM))
    return _proj_res(x2, o, w_o, layer).reshape(b, s, d)


def _sgu_kernel(x_ref, g_ref, w_ref, b_ref, vg_ref, ws_ref, bs_ref, o_ref, *, tm, sub):
    ii = lax.broadcasted_iota(jnp.int32, (SGU_BLOCK, SGU_BLOCK), 0) // CHUNK
    jj = lax.broadcasted_iota(jnp.int32, (SGU_BLOCK, SGU_BLOCK), 1) // CHUNK
    causal = jj <= ii
    ws = [jnp.where(causal, ws_ref[gi], 0.0).astype(BF16) for gi in range(SGU_GROUPS)]
    bs = bs_ref[...]
    width = w_ref.shape[1] // 2
    gd = width // SGU_GROUPS
    for st in range(tm // sub):
        r0 = st * sub
        h = _rms(x_ref[r0:r0 + sub, :], g_ref[...]).astype(BF16)
        z = _dot(h, w_ref[...]) + b_ref[...]
        z = 0.5 * z * (1.0 + lax.erf(z * (2.0 ** -0.5)))
        u = z[:, :width]
        v = _rms(z[:, width:], vg_ref[...]).astype(BF16)
        for gi in range(SGU_GROUPS):
            cs = slice(gi * gd, (gi + 1) * gd)
            for nb in range(sub // SGU_BLOCK):
                rs = slice(nb * SGU_BLOCK, (nb + 1) * SGU_BLOCK)
                mixed = _dot(ws[gi], v[rs, cs]) + bs[:, gi:gi + 1]
                o_ref[r0 + nb * SGU_BLOCK:r0 + (nb + 1) * SGU_BLOCK, cs] = (u[rs, cs] * mixed).astype(BF16)


def _spatial_gating(x, g, w_in, b_in, v_gain, w_s, b_s, w_o, layer):
    b, s, d = x.shape
    n = b * s
    x2 = x.reshape(n, d)
    width = w_in.shape[1] // 2
    tm = min(SGU_ROWS, s)
    row = lambda i: (i, 0)
    gated = pl.pallas_call(
        functools.partial(_sgu_kernel, tm=tm, sub=min(SGU_SUB_ROWS, tm)),
        grid=(n // tm,),
        in_specs=[pl.BlockSpec((tm, d), row), _resident((1, d)), _resident((d, 2 * width)),
                  _resident((1, 2 * width)), _resident((1, width)),
                  _resident((SGU_GROUPS, SGU_BLOCK, SGU_BLOCK)), _resident((SGU_BLOCK, SGU_GROUPS))],
        out_specs=pl.BlockSpec((tm, width), row),
        out_shape=jax.ShapeDtypeStruct((n, width), BF16),
        compiler_params=_params("parallel"),
        name="sgu_gate",
    )(x2, g.reshape(1, d), w_in.astype(BF16), b_in.reshape(1, 2 * width), v_gain.reshape(1, width),
      w_s, b_s.T)
    return _proj_res(x2, gated, w_o, layer).reshape(b, s, d)


def kernel(x, norm_mix, norm_ffn, pool_w, pool_scale, attn_w_in, attn_q_gain, attn_k_gain, attn_w_o,
           sgu_w_in, sgu_b_in, sgu_v_gain, sgu_w_s, sgu_b_s, sgu_w_o, ffn_w_up, ffn_w_down):
    b, s, d = x.shape
    depth = norm_mix.shape[0]
    for i in range(depth):
        kind, j = i % 3, i // 3
        if kind == 0:
            x = _pool_mixer(x, norm_mix[i], pool_w[j].astype(BF16), pool_scale[j])
        elif kind == 1:
            x = _sparse_attention(x, norm_mix[i], attn_w_in, attn_q_gain[j], attn_k_gain[j],
                                  attn_w_o, j)
        else:
            x = _spatial_gating(x, norm_mix[i], sgu_w_in[j], sgu_b_in[j], sgu_v_gain[j], sgu_w_s[j],
                                sgu_b_s[j], sgu_w_o, j)
        x = _ffn(x.reshape(b * s, d), norm_ffn[i], ffn_w_up, ffn_w_down, i).reshape(b, s, d)
    return x
```

```python
import functools

import jax
import jax.numpy as jnp
from jax import lax
from jax.experimental import pallas as pl
from jax.experimental.pallas import tpu as pltpu

EPS = 1e-6
CHUNK = 64
POOL_WINDOWS = (2, 4, 8, 16)
POOL_HALO = 16
HEAD_DIM = 128
N_KV_HEADS = 4
IDX_HEADS = 16
IDX_DIM = 64
INDEX_TOPK = 256
Q_BLOCK = 128
ROPE_THETA = 500000.0
ROT_FRACTION = 4
SGU_BLOCK = 128
SGU_GROUPS = 8
LANES = 128
MXU_COLS = 256
assert CHUNK & (CHUNK - 1) == 0
INT_MIN = -(2 ** 31)
COUNT_ROWS = 64
ONES_ROWS = 16
DEN_MIN = 2.0 ** -60
LOG2E = 1.4426950408889634
MASK_BIAS = -1e30
VMEM_LIMIT_BYTES = 60 * 1024 * 1024

POOL_ROWS = 1024
FFN_ROWS = 1024
FFN_COLS = 512
PROJ_ROWS = 512
SGU_ROWS = 512
SGU_SUB_ROWS = 256
SCORE_KEYS = 512
FLASH_KEYS = 1024

F32 = jnp.float32
BF16 = jnp.bfloat16


def _params(*sem):
    return pltpu.CompilerParams(dimension_semantics=sem, vmem_limit_bytes=VMEM_LIMIT_BYTES)


def _resident(shape):
    nd = len(shape)
    return pl.BlockSpec(shape, lambda *_: (0,) * nd, pipeline_mode=pl.Buffered(1))


def _cast_once(w_ref, wbf_ref, transpose=False):
    @pl.when(pl.program_id(0) == 0)
    def _():
        if transpose:
            for r in range(0, w_ref.shape[0], LANES):
                wbf_ref[:, r:r + LANES] = w_ref[r:r + LANES, :].T.astype(BF16)
        else:
            wbf_ref[...] = w_ref[...].astype(BF16)


def _rms(xf, g):
    ms = jnp.mean(xf * xf, axis=-1, keepdims=True)
    return xf * lax.rsqrt(ms + EPS) * g


def _dot(a, b):
    return jnp.dot(a, b, preferred_element_type=F32)


def _pool_kernel(x_ref, halo_ref, g_ref, w_ref, scale_ref, o_ref, *, ts):
    i = pl.program_id(1)
    x = x_ref[0]
    g = g_ref[...]
    h = _rms(x, g)
    hh = _rms(halo_ref[0], g)
    hh = jnp.where(i > 0, hh, 0.0)
    hf = jnp.concatenate([hh, h], axis=0)
    t1 = (i * ts + lax.broadcasted_iota(jnp.int32, (ts, 1), 0) + 1).astype(F32)
    cg = x.shape[1] // len(POOL_WINDOWS)
    for gi, w in enumerate(POOL_WINDOWS):
        sl = slice(gi * cg, (gi + 1) * cg)
        s = hf[:, sl]
        k = 1
        while k < w:
            s = s + pltpu.roll(s, k, 0)
            k *= 2
        mean = s[POOL_HALO:] / jnp.minimum(t1, float(w))
        p = (mean - h[:, sl]).astype(BF16)
        y = _dot(p, w_ref[gi]) * scale_ref[:, sl]
        o_ref[0, :, sl] = x[:, sl] + y


def _pool_mixer(x, g, w_bf, scale):
    b, s, d = x.shape
    ts = min(POOL_ROWS, s)
    hb = ts // POOL_HALO
    ng = len(POOL_WINDOWS)
    return pl.pallas_call(
        functools.partial(_pool_kernel, ts=ts),
        grid=(b, s // ts),
        in_specs=[
            pl.BlockSpec((1, ts, d), lambda bi, i: (bi, i, 0)),
            pl.BlockSpec((1, POOL_HALO, d), lambda bi, i: (bi, jnp.maximum(i * hb - 1, 0), 0)),
            _resident((1, d)),
            _resident((ng, d // ng, d // ng)),
            _resident((1, d)),
        ],
        out_specs=pl.BlockSpec((1, ts, d), lambda bi, i: (bi, i, 0)),
        out_shape=jax.ShapeDtypeStruct((b, s, d), F32),
        compiler_params=_params("parallel", "parallel"),
        name="pool_mixer",
    )(x, x, g.reshape(1, d), w_bf, scale.reshape(1, d))


def _ffn_kernel(x_ref, g_ref, wu_ref, wd_ref, o_ref, h_ref):
    j = pl.program_id(1)

    @pl.when(j == 0)
    def _():
        x = x_ref[...]
        h_ref[...] = _rms(x, g_ref[...]).astype(BF16)
        o_ref[...] = x

    u = _dot(h_ref[...], wu_ref[...].astype(BF16))
    a = jnp.square(jnp.maximum(u, 0.0)).astype(BF16)
    o_ref[...] += _dot(a, wd_ref[...].astype(BF16))


def _ffn(x2, g, w_up, w_down, layer):
    n, d = x2.shape
    f = w_up.shape[2]
    tm = min(FFN_ROWS, n)
    tf = FFN_COLS
    return pl.pallas_call(
        _ffn_kernel,
        grid=(n // tm, f // tf),
        in_specs=[
            pl.BlockSpec((tm, d), lambda i, j: (i, 0)),
            _resident((1, d)),
            pl.BlockSpec((None, d, tf), lambda i, j: (layer, 0, j)),
            pl.BlockSpec((None, tf, d), lambda i, j: (layer, j, 0)),
        ],
        out_specs=pl.BlockSpec((tm, d), lambda i, j: (i, 0)),
        out_shape=jax.ShapeDtypeStruct((n, d), F32),
        scratch_shapes=[pltpu.VMEM((tm, d), BF16)],
        compiler_params=_params("parallel", "arbitrary"),
        name="ffn",
    )(x2, g.reshape(1, d), w_up, w_down)


def _proj_res_kernel(x_ref, a_ref, w_ref, o_ref, wbf_ref, *, a_transposed):
    _cast_once(w_ref, wbf_ref)
    if a_transposed:
        y = lax.dot_general(a_ref[0], wbf_ref[...], (((0,), (0,)), ((), ())),
                            preferred_element_type=F32)
    else:
        y = _dot(a_ref[...], wbf_ref[...])
    o_ref[...] = x_ref[...] + y


def _proj_res(x2, a_bf, w, layer):
    n, d = x2.shape
    a_transposed = a_bf.ndim == 3
    kdim = a_bf.shape[1]
    tm = min(PROJ_ROWS, a_bf.shape[2] if a_transposed else n)
    if a_transposed:
        nt = a_bf.shape[2] // tm
        a_spec = pl.BlockSpec((1, kdim, tm), lambda i: (i // nt, 0, i % nt))
    else:
        a_spec = pl.BlockSpec((tm, kdim), lambda i: (i, 0))
    return pl.pallas_call(
        functools.partial(_proj_res_kernel, a_transposed=a_transposed),
        grid=(n // tm,),
        in_specs=[
            pl.BlockSpec((tm, d), lambda i: (i, 0)),
            a_spec,
            pl.BlockSpec((None, kdim, d), lambda i: (layer, 0, 0), pipeline_mode=pl.Buffered(1)),
        ],
        out_specs=pl.BlockSpec((tm, d), lambda i: (i, 0)),
        out_shape=jax.ShapeDtypeStruct((n, d), F32),
        scratch_shapes=[pltpu.VMEM((kdim, d), BF16)],
        compiler_params=_params("arbitrary"),
        name="proj_res",
    )(x2, a_bf, w)


def _rope_tables(s, width, rot):
    half = rot // 2
    inv = ROPE_THETA ** (-jnp.arange(half, dtype=F32) / half)
    ang = jnp.arange(s, dtype=F32)[:, None] * inv[None, :]
    cos, sin = jnp.cos(ang), jnp.sin(ang)
    pad = jnp.zeros((s, width - rot), F32)
    zero = jnp.zeros((s, half), F32)
    c = jnp.concatenate([cos, cos, pad + 1.0], axis=1)
    a = jnp.concatenate([-sin, zero, pad], axis=1)
    b = jnp.concatenate([zero, sin, pad], axis=1)
    rep = LANES // width
    return tuple(jnp.tile(t, (1, rep)) for t in (c, a, b))


def _rope_tables_t(s, rot):
    half = rot // 2
    inv = ROPE_THETA ** (-jnp.arange(half, dtype=F32) / half)
    ang = inv[:, None] * jnp.arange(s, dtype=F32)[None, :]
    return jnp.cos(ang), jnp.sin(ang)


def _rope(x, c, a, b, half):
    return x * c + pltpu.roll(x, LANES - half, 1) * a + pltpu.roll(x, half, 1) * b


def _rope_t(xt, cos, sin):
    half = cos.shape[0]
    x1, x2 = xt[:half], xt[half:2 * half]
    return jnp.concatenate([x1 * cos - x2 * sin, x2 * cos + x1 * sin, xt[2 * half:]], axis=0)


def _q_proj_kernel(x_ref, g_ref, w_ref, gain_ref, cos_ref, sin_ref, qt_ref, qn_ref, wbf_ref):
    _cast_once(w_ref, wbf_ref, transpose=True)
    h = _rms(x_ref[...], g_ref[...]).astype(BF16)
    tm = h.shape[0]
    gain = jnp.concatenate([gain_ref[...]] * (tm // LANES), axis=1)
    cos, sin = cos_ref[...], sin_ref[...]
    for pair in range(wbf_ref.shape[1] // MXU_COLS):
        q = _dot(h, wbf_ref[:, pair * MXU_COLS:(pair + 1) * MXU_COLS])
        for hd in range(MXU_COLS // HEAD_DIM):
            qh = q[:, hd * HEAD_DIM:(hd + 1) * HEAD_DIM].T
            qh = qh * lax.rsqrt(jnp.mean(qh * qh, axis=0, keepdims=True) + EPS) * gain
            qh = _rope_t(qh, cos, sin) * (HEAD_DIM ** -0.5 * LOG2E)
            row = pair * MXU_COLS + hd * HEAD_DIM
            qt_ref[0, row:row + HEAD_DIM, :] = qh.astype(BF16)
            head = row // HEAD_DIM
            qn_ref[0, head:head + 1, :] = jnp.sqrt(jnp.sum(qh * qh, axis=0, keepdims=True))


def _kv_proj_kernel(x_ref, g_ref, w_ref, wtail_ref, gain_ref, c_ref, a_ref, b_ref, ci_ref, ai_ref,
                    bi_ref, cosi_ref, sini_ref, k_ref, vt_ref, qit_ref, ki_ref, wit_ref, wbf_ref):
    _cast_once(w_ref, wbf_ref, transpose=True)
    h = _rms(x_ref[...], g_ref[...]).astype(BF16)
    y = _dot(h, wbf_ref[...])
    gain = gain_ref[...]
    c, a, b = c_ref[...], a_ref[...], b_ref[...]
    half = HEAD_DIM // ROT_FRACTION // 2
    halfi = IDX_DIM // ROT_FRACTION // 2
    nkv = N_KV_HEADS * HEAD_DIM
    for hd in range(N_KV_HEADS):
        sl = slice(hd * HEAD_DIM, (hd + 1) * HEAD_DIM)
        k_ref[:, sl] = _rope(_rms(y[:, sl], gain), c, a, b, half).astype(BF16)
        vt_ref[0, sl, :] = y[:, nkv + hd * HEAD_DIM:nkv + (hd + 1) * HEAD_DIM].T.astype(BF16)
    nqi = IDX_HEADS * IDX_DIM
    cosi, sini = cosi_ref[...], sini_ref[...]
    for hd in range(IDX_HEADS):
        col = 2 * nkv + hd * IDX_DIM
        if hd % 2 == 0:
            pair_t = y[:, col:col + LANES].T
        qh = pair_t[(hd % 2) * IDX_DIM:(hd % 2 + 1) * IDX_DIM]
        qit_ref[0, hd * IDX_DIM:(hd + 1) * IDX_DIM, :] = _rope_t(qh, cosi, sini).astype(BF16)
    kw = _dot(h, wtail_ref[...])
    ki_ref[...] = _rope(kw, ci_ref[...], ai_ref[...], bi_ref[...], halfi)[:, :IDX_DIM].astype(BF16)
    wit_ref[0] = kw.T[IDX_DIM:IDX_DIM + IDX_HEADS] * (IDX_HEADS ** -0.5 * IDX_DIM ** -0.5)


def _key_to_f32(key):
    bits = jnp.where(key < 0, key ^ jnp.int32(0x7FFFFFFF), key)
    f = pltpu.bitcast(bits, F32)
    return jnp.where(f != f, jnp.inf, f)


def _attn_kernel(qt_ref, qn_ref, qit_ref, wit_ref, k_ref, vt_ref, ki_ref, kgain_ref, o_ref, sc_ref,
                 bias_ref, *, tk, tkf, n_sel):
    t0 = pl.program_id(1) * Q_BLOCK
    nk = (t0 + Q_BLOCK + tk - 1) // tk
    q_pos = t0 + lax.broadcasted_iota(jnp.int32, (1, Q_BLOCK), 1)
    key_end = (lax.shift_right_logical(q_pos, CHUNK.bit_length() - 1) + 1) * CHUNK
    key_row = lax.broadcasted_iota(jnp.int32, (tk, Q_BLOCK), 0)

    qit = qit_ref[0]
    wit = wit_ref[0]
    npair = IDX_HEADS // 2
    rhs = [jnp.concatenate([qit[(2 * p) * IDX_DIM:(2 * p + 1) * IDX_DIM],
                            qit[(2 * p + 1) * IDX_DIM:(2 * p + 2) * IDX_DIM]], axis=1)
           for p in range(npair)]

    def over_key_tiles(step, init):
        if tkf != 2 * tk:
            return lax.fori_loop(0, nk, lambda kt, c: step(pl.multiple_of(kt * tk, tk), tk, c), init)
        carry = lax.fori_loop(
            0, nk // 2, lambda kt, c: step(pl.multiple_of(kt * tkf, tkf), tkf, c), init)
        return lax.fori_loop(
            0, nk % 2, lambda _, c: step(pl.multiple_of((nk // 2) * tkf, tk), tk, c), carry)

    def score_tile(off, size, carry):
        ki_t = ki_ref[0, pl.ds(off, size), :]
        acc = jnp.zeros((size, Q_BLOCK), F32)
        for p in range(npair):
            d = jnp.maximum(_dot(ki_t, rhs[p]), 0.0)
            acc = acc + d[:, :Q_BLOCK] * wit[2 * p:2 * p + 1, :]
            acc = acc + d[:, Q_BLOCK:] * wit[2 * p + 1:2 * p + 2, :]
        row = lax.broadcasted_iota(jnp.int32, (size, Q_BLOCK), 0)
        sc_ref[pl.ds(off, size), :] = jnp.where(row < key_end - off, acc, -jnp.inf)
        return carry

    over_key_tiles(score_tile, 0)

    def count(pred):
        def body(kt, c):
            off = pl.multiple_of(kt * tk, tk)
            m = jnp.where(pred(sc_ref[pl.ds(off, tk), :], key_row + off), 1.0, 0.0)
            return c + jnp.sum(m.reshape(tk // COUNT_ROWS, COUNT_ROWS, Q_BLOCK), axis=0)

        c = lax.fori_loop(0, nk, body, jnp.zeros((COUNT_ROWS, Q_BLOCK), F32))
        return jnp.sum(c, axis=0, keepdims=True)

    def bit_body(bi, carry):
        key, cnt = carry
        cand = key + lax.shift_left(jnp.int32(1), 31 - bi)
        cand_f = _key_to_f32(cand)
        c = count(lambda x, pos: x >= cand_f)
        ok = c >= n_sel
        return jnp.where(ok, cand, key), jnp.where(ok, c, cnt)

    nbits = jnp.where(t0 + Q_BLOCK <= n_sel, 0, 32)
    key, cnt = lax.fori_loop(0, nbits, bit_body, (jnp.full((1, Q_BLOCK), INT_MIN, jnp.int32),
                                                  jnp.zeros((1, Q_BLOCK), F32)))
    thr = jnp.where(key == INT_MIN, jnp.finfo(F32).min, _key_to_f32(key))

    @pl.when(jnp.max(cnt) > n_sel)
    def _():
        need = n_sel - count(lambda x, pos: x > thr)
        pos_bits = (sc_ref.shape[0] - 1).bit_length() + 1

        def body(bi, end):
            cand = end + lax.shift_left(jnp.int32(1), pos_bits - 1 - bi)
            c = count(lambda x, pos: jnp.logical_and(x == thr, pos < cand))
            return jnp.where(c <= need, cand, end)

        pos_end = lax.fori_loop(0, pos_bits, body, jnp.zeros((1, Q_BLOCK), jnp.int32))

        def retire(kt, carry):
            off = pl.multiple_of(kt * tk, tk)
            x = sc_ref[pl.ds(off, tk), :]
            late_tie = jnp.logical_and(x == thr, key_row + off >= pos_end)
            sc_ref[pl.ds(off, tk), :] = jnp.where(late_tie, -jnp.inf, x)
            return carry

        lax.fori_loop(0, nk, retire, 0)

    def bias_tile(kt, carry):
        off = pl.multiple_of(kt * tk, tk)
        sel = sc_ref[pl.ds(off, tk), :] >= thr
        bias_ref[pl.ds(off, tk), :] = jnp.where(sel, 0.0, MASK_BIAS).astype(BF16)
        return carry

    lax.fori_loop(0, nk, bias_tile, 0)

    qt = qt_ref[0]
    rep = qt.shape[0] // HEAD_DIM // N_KV_HEADS
    cols = rep * Q_BLOCK
    eye = (lax.broadcasted_iota(jnp.int32, (Q_BLOCK, Q_BLOCK), 0)
           == lax.broadcasted_iota(jnp.int32, (Q_BLOCK, Q_BLOCK), 1))
    eye = jnp.where(eye, 1.0, 0.0).astype(BF16)
    gsls = [slice(g * HEAD_DIM, (g + 1) * HEAD_DIM) for g in range(N_KV_HEADS)]
    qaugs = []
    for g in range(N_KV_HEADS):
        qg = jnp.concatenate(
            [qt[(g * rep + r) * HEAD_DIM:(g * rep + r + 1) * HEAD_DIM] for r in range(rep)], axis=1)
        qaugs.append(jnp.concatenate([qg, jnp.concatenate([eye] * rep, axis=1)], axis=0))

    def flash(shift):
        def step(off, size, accs):
            bias_t = bias_ref[pl.ds(off, size), :]
            ones = jnp.ones((ONES_ROWS, size), BF16)
            out = []
            for g in range(N_KV_HEADS):
                kaug = jnp.concatenate([k_ref[0, pl.ds(off, size), gsls[g]], bias_t], axis=1)
                vaug = jnp.concatenate([vt_ref[0, gsls[g], pl.ds(off, size)], ones], axis=0)
                p = jnp.exp2(_dot(kaug, qaugs[g]) - shift[g]).astype(BF16)
                out.append(accs[g] + _dot(vaug, p))
            return tuple(out)

        zero = jnp.zeros((HEAD_DIM + ONES_ROWS, cols), F32)
        return over_key_tiles(step, (zero,) * N_KV_HEADS)

    def column_max():
        def step(off, size, ms):
            bias_t = bias_ref[pl.ds(off, size), :]
            out = []
            for g in range(N_KV_HEADS):
                kaug = jnp.concatenate([k_ref[0, pl.ds(off, size), gsls[g]], bias_t], axis=1)
                out.append(jnp.maximum(ms[g], jnp.max(_dot(kaug, qaugs[g]), axis=0, keepdims=True)))
            return tuple(out)

        return over_key_tiles(step, (jnp.full((1, cols), MASK_BIAS, F32),) * N_KV_HEADS)

    def write(accs):
        for g in range(N_KV_HEADS):
            o = accs[g][:HEAD_DIM] / accs[g][HEAD_DIM:HEAD_DIM + 1]
            for r in range(rep):
                hsl = slice((g * rep + r) * HEAD_DIM, (g * rep + r + 1) * HEAD_DIM)
                o_ref[0, hsl, :] = o[:, r * Q_BLOCK:(r + 1) * Q_BLOCK].astype(BF16)

    kmax = (HEAD_DIM ** 0.5) * jnp.max(jnp.abs(kgain_ref[...]), axis=1, keepdims=True)
    qn = qn_ref[0]
    bound = [jnp.concatenate([qn[g * rep + r:g * rep + r + 1] for r in range(rep)], axis=1) * kmax
             for g in range(N_KV_HEADS)]
    accs = flash(bound)
    den = jnp.concatenate([acc[HEAD_DIM:HEAD_DIM + 1] for acc in accs], axis=1)
    safe = jnp.logical_and(jnp.min(den) >= DEN_MIN, jnp.max(den) <= 1.0 / DEN_MIN)

    @pl.when(safe)
    def _():
        write(accs)

    @pl.when(jnp.logical_not(safe))
    def _():
        write(flash(column_max()))


def _sparse_attention(x, g, w_in, q_gain, k_gain, w_o, layer):
    b, s, d = x.shape
    n = b * s
    x2 = x.reshape(n, d)
    nq = d
    nkv = N_KV_HEADS * HEAD_DIM
    nqi = IDX_HEADS * IDX_DIM
    nmid = 2 * nkv + nqi
    assert nmid == nq
    pad = LANES - IDX_DIM - IDX_HEADS
    w_tail = jnp.pad(w_in[layer, :, nq + nmid:], ((0, 0), (0, pad))).astype(BF16)
    w_in_t = jnp.swapaxes(w_in, 1, 2)
    wblock = lambda blk: pl.BlockSpec((None, nq, d), lambda i: (layer, blk, 0),
                                      pipeline_mode=pl.Buffered(1))
    c, a, bt = _rope_tables(s, HEAD_DIM, HEAD_DIM // ROT_FRACTION)
    ci, ai, bi = _rope_tables(s, IDX_DIM, IDX_DIM // ROT_FRACTION)
    cos_t, sin_t = _rope_tables_t(s, HEAD_DIM // ROT_FRACTION)
    cosi_t, sini_t = _rope_tables_t(s, IDX_DIM // ROT_FRACTION)
    q_gain_b = jnp.broadcast_to(q_gain[:, None], (HEAD_DIM, LANES))

    tm = min(PROJ_ROWS, s)
    nt = s // tm
    row = lambda i: (i, 0)
    pos = lambda i: (i % nt, 0)
    tcol = lambda i: (i // nt, 0, i % nt)
    tab = pl.BlockSpec((tm, LANES), pos)
    tab_t = lambda half: pl.BlockSpec((half, tm), lambda i: (0, i % nt))
    half = HEAD_DIM // ROT_FRACTION // 2
    halfi = IDX_DIM // ROT_FRACTION // 2
    qt, qn = pl.pallas_call(
        _q_proj_kernel,
        grid=(n // tm,),
        in_specs=[pl.BlockSpec((tm, d), row), _resident((1, d)), wblock(0),
                  _resident((HEAD_DIM, LANES)), tab_t(half), tab_t(half)],
        out_specs=[pl.BlockSpec((1, nq, tm), tcol), pl.BlockSpec((1, nq // HEAD_DIM, tm), tcol)],
        out_shape=[jax.ShapeDtypeStruct((b, nq, s), BF16),
                   jax.ShapeDtypeStruct((b, nq // HEAD_DIM, s), F32)],
        scratch_shapes=[pltpu.VMEM((d, nq), BF16)],
        compiler_params=_params("arbitrary"),
        name="attn_q_proj",
    )(x2, g.reshape(1, d), w_in_t, q_gain_b, cos_t, sin_t)

    k, vt, qit, ki, wit = pl.pallas_call(
        _kv_proj_kernel,
        grid=(n // tm,),
        in_specs=[pl.BlockSpec((tm, d), row), _resident((1, d)), wblock(1), _resident((d, LANES)),
                  _resident((1, HEAD_DIM)), tab, tab, tab, tab, tab, tab, tab_t(halfi), tab_t(halfi)],
        out_specs=[pl.BlockSpec((tm, nkv), row), pl.BlockSpec((1, nkv, tm), tcol),
                   pl.BlockSpec((1, nqi, tm), tcol), pl.BlockSpec((tm, IDX_DIM), row),
                   pl.BlockSpec((1, IDX_HEADS, tm), tcol)],
        out_shape=[jax.ShapeDtypeStruct((n, nkv), BF16), jax.ShapeDtypeStruct((b, nkv, s), BF16),
                   jax.ShapeDtypeStruct((b, nqi, s), BF16), jax.ShapeDtypeStruct((n, IDX_DIM), BF16),
                   jax.ShapeDtypeStruct((b, IDX_HEADS, s), F32)],
        scratch_shapes=[pltpu.VMEM((d, nmid), BF16)],
        compiler_params=_params("arbitrary"),
        name="attn_kv_proj",
    )(x2, g.reshape(1, d), w_in_t, w_tail, k_gain.reshape(1, HEAD_DIM), c, a, bt, ci, ai, bi, cosi_t,
      sini_t)

    n_sel = min(INDEX_TOPK, s // 4)
    tk = min(SCORE_KEYS, s)
    qcol = lambda bi_, qb: (bi_, 0, qb)
    full = lambda bi_, qb: (bi_, 0, 0)
    o = pl.pallas_call(
        functools.partial(_attn_kernel, tk=tk, tkf=min(FLASH_KEYS, s), n_sel=n_sel),
        grid=(b, s // Q_BLOCK),
        in_specs=[pl.BlockSpec((1, nq, Q_BLOCK), qcol), pl.BlockSpec((1, nq // HEAD_DIM, Q_BLOCK), qcol),
                  pl.BlockSpec((1, nqi, Q_BLOCK), qcol),
                  pl.BlockSpec((1, IDX_HEADS, Q_BLOCK), qcol), pl.BlockSpec((1, s, nkv), full),
                  pl.BlockSpec((1, nkv, s), full), pl.BlockSpec((1, s, IDX_DIM), full),
                  pl.BlockSpec((1, HEAD_DIM), lambda bi_, qb: (0, 0))],
        out_specs=pl.BlockSpec((1, nq, Q_BLOCK), qcol),
        out_shape=jax.ShapeDtypeStruct((b, nq, s), BF16),
        scratch_shapes=[pltpu.VMEM((s, Q_BLOCK), F32), pltpu.VMEM((s, Q_BLOCK), BF16)],
        compiler_params=_params("parallel", "arbitrary"),
        name="sparse_attn",
    )(qt, qn, qit, wit, k.reshape(b, s, nkv), vt, ki.reshape(b, s, IDX_DIM),
      k_gain.reshape(1, HEAD_DIM))
    return _proj_res(x2, o, w_o, layer).reshape(b, s, d)


def _sgu_kernel(x_ref, g_ref, w_ref, b_ref, vg_ref, ws_ref, bs_ref, o_ref, *, tm, sub):
    ii = lax.broadcasted_iota(jnp.int32, (SGU_BLOCK, SGU_BLOCK), 0) // CHUNK
    jj = lax.broadcasted_iota(jnp.int32, (SGU_BLOCK, SGU_BLOCK), 1) // CHUNK
    causal = jj <= ii
    ws = [jnp.where(causal, ws_ref[gi], 0.0).astype(BF16) for gi in range(SGU_GROUPS)]
    bs = bs_ref[...]
    width = w_ref.shape[1] // 2
    gd = width // SGU_GROUPS
    for st in range(tm // sub):
        r0 = st * sub
        h = _rms(x_ref[r0:r0 + sub, :], g_ref[...]).astype(BF16)
        z = _dot(h, w_ref[...]) + b_ref[...]
        z = 0.5 * z * (1.0 + lax.erf(z * (2.0 ** -0.5)))
        u = z[:, :width]
        v = _rms(z[:, width:], vg_ref[...]).astype(BF16)
        for gi in range(SGU_GROUPS):
            cs = slice(gi * gd, (gi + 1) * gd)
            for nb in range(sub // SGU_BLOCK):
                rs = slice(nb * SGU_BLOCK, (nb + 1) * SGU_BLOCK)
                mixed = _dot(ws[gi], v[rs, cs]) + bs[:, gi:gi + 1]
                o_ref[r0 + nb * SGU_BLOCK:r0 + (nb + 1) * SGU_BLOCK, cs] = (u[rs, cs] * mixed).astype(BF16)


def _spatial_gating(x, g, w_in, b_in, v_gain, w_s, b_s, w_o, layer):
    b, s, d = x.shape
    n = b * s
    x2 = x.reshape(n, d)
    width = w_in.shape[1] // 2
    tm = min(SGU_ROWS, s)
    row = lambda i: (i, 0)
    gated = pl.pallas_call(
        functools.partial(_sgu_kernel, tm=tm, sub=min(SGU_SUB_ROWS, tm)),
        grid=(n // tm,),
        in_specs=[pl.BlockSpec((tm, d), row), _resident((1, d)), _resident((d, 2 * width)),
                  _resident((1, 2 * width)), _resident((1, width)),
                  _resident((SGU_GROUPS, SGU_BLOCK, SGU_BLOCK)), _resident((SGU_BLOCK, SGU_GROUPS))],
        out_specs=pl.BlockSpec((tm, width), row),
        out_shape=jax.ShapeDtypeStruct((n, width), BF16),
        compiler_params=_params("parallel"),
        name="sgu_gate",
    )(x2, g.reshape(1, d), w_in.astype(BF16), b_in.reshape(1, 2 * width), v_gain.reshape(1, width),
      w_s, b_s.T)
    return _proj_res(x2, gated, w_o, layer).reshape(b, s, d)


def kernel(x, norm_mix, norm_ffn, pool_w, pool_scale, attn_w_in, attn_q_gain, attn_k_gain, attn_w_o,
           sgu_w_in, sgu_b_in, sgu_v_gain, sgu_w_s, sgu_b_s, sgu_w_o, ffn_w_up, ffn_w_down):
    b, s, d = x.shape
    depth = norm_mix.shape[0]
    for i in range(depth):
        kind, j = i % 3, i // 3
        if kind == 0:
            x = _pool_mixer(x, norm_mix[i], pool_w[j].astype(BF16), pool_scale[j])
        elif kind == 1:
            x = _sparse_attention(x, norm_mix[i], attn_w_in, attn_q_gain[j], attn_k_gain[j],
                                  attn_w_o, j)
        else:
            x = _spatial_gating(x, norm_mix[i], sgu_w_in[j], sgu_b_in[j], sgu_v_gain[j], sgu_w_s[j],
                                sgu_b_s[j], sgu_w_o, j)
        x = _ffn(x.reshape(b * s, d), norm_ffn[i], ffn_w_up, ffn_w_down, i).reshape(b, s, d)
    return x
```

```python
import functools

import jax
import jax.numpy as jnp
from jax import lax
from jax.experimental import pallas as pl
from jax.experimental.pallas import tpu as pltpu

EPS = 1e-6
CHUNK = 64
POOL_WINDOWS = (2, 4, 8, 16)
POOL_HALO = 16
HEAD_DIM = 128
N_KV_HEADS = 4
IDX_HEADS = 16
IDX_DIM = 64
INDEX_TOPK = 256
Q_BLOCK = 128
ROPE_THETA = 500000.0
ROT_FRACTION = 4
SGU_BLOCK = 128
SGU_GROUPS = 8
LANES = 128
MXU_COLS = 256
assert CHUNK & (CHUNK - 1) == 0
INT_MIN = -(2 ** 31)
COUNT_ROWS = 64
ONES_ROWS = 16
DEN_MIN = 2.0 ** -60
LOG2E = 1.4426950408889634
MASK_BIAS = -1e30
VMEM_LIMIT_BYTES = 60 * 1024 * 1024

POOL_ROWS = 1024
FFN_ROWS = 1024
FFN_COLS = 512
PROJ_ROWS = 512
SGU_ROWS = 512
SGU_SUB_ROWS = 256
SCORE_KEYS = 512
FLASH_KEYS = 1024

F32 = jnp.float32
BF16 = jnp.bfloat16


def _params(*sem):
    return pltpu.CompilerParams(dimension_semantics=sem, vmem_limit_bytes=VMEM_LIMIT_BYTES)


def _resident(shape):
    nd = len(shape)
    return pl.BlockSpec(shape, lambda *_: (0,) * nd, pipeline_mode=pl.Buffered(1))


def _cast_once(w_ref, wbf_ref, transpose=False):
    @pl.when(pl.program_id(0) == 0)
    def _():
        if transpose:
            for r in range(0, w_ref.shape[0], LANES):
                wbf_ref[:, r:r + LANES] = w_ref[r:r + LANES, :].T.astype(BF16)
        else:
            wbf_ref[...] = w_ref[...].astype(BF16)


def _rms(xf, g):
    ms = jnp.mean(xf * xf, axis=-1, keepdims=True)
    return xf * lax.rsqrt(ms + EPS) * g


def _dot(a, b):
    return jnp.dot(a, b, preferred_element_type=F32)


def _pool_kernel(x_ref, halo_ref, g_ref, w_ref, scale_ref, o_ref, *, ts):
    i = pl.program_id(1)
    x = x_ref[0]
    g = g_ref[...]
    h = _rms(x, g)
    hh = _rms(halo_ref[0], g)
    hh = jnp.where(i > 0, hh, 0.0)
    hf = jnp.concatenate([hh, h], axis=0)
    t1 = (i * ts + lax.broadcasted_iota(jnp.int32, (ts, 1), 0) + 1).astype(F32)
    cg = x.shape[1] // len(POOL_WINDOWS)
    for gi, w in enumerate(POOL_WINDOWS):
        sl = slice(gi * cg, (gi + 1) * cg)
        s = hf[:, sl]
        k = 1
        while k < w:
            s = s + pltpu.roll(s, k, 0)
            k *= 2
        mean = s[POOL_HALO:] / jnp.minimum(t1, float(w))
        p = (mean - h[:, sl]).astype(BF16)
        y = _dot(p, w_ref[gi]) * scale_ref[:, sl]
        o_ref[0, :, sl] = x[:, sl] + y


def _pool_mixer(x, g, w_bf, scale):
    b, s, d = x.shape
    ts = min(POOL_ROWS, s)
    hb = ts // POOL_HALO
    ng = len(POOL_WINDOWS)
    return pl.pallas_call(
        functools.partial(_pool_kernel, ts=ts),
        grid=(b, s // ts),
        in_specs=[
            pl.BlockSpec((1, ts, d), lambda bi, i: (bi, i, 0)),
            pl.BlockSpec((1, POOL_HALO, d), lambda bi, i: (bi, jnp.maximum(i * hb - 1, 0), 0)),
            _resident((1, d)),
            _resident((ng, d // ng, d // ng)),
            _resident((1, d)),
        ],
        out_specs=pl.BlockSpec((1, ts, d), lambda bi, i: (bi, i, 0)),
        out_shape=jax.ShapeDtypeStruct((b, s, d), F32),
        compiler_params=_params("parallel", "parallel"),
        name="pool_mixer",
    )(x, x, g.reshape(1, d), w_bf, scale.reshape(1, d))


def _ffn_kernel(x_ref, g_ref, wu_ref, wd_ref, o_ref, h_ref):
    j = pl.program_id(1)

    @pl.when(j == 0)
    def _():
        x = x_ref[...]
        h_ref[...] = _rms(x, g_ref[...]).astype(BF16)
        o_ref[...] = x

    u = _dot(h_ref[...], wu_ref[...].astype(BF16))
    a = jnp.square(jnp.maximum(u, 0.0)).astype(BF16)
    o_ref[...] += _dot(a, wd_ref[...].astype(BF16))


def _ffn(x2, g, w_up, w_down, layer):
    n, d = x2.shape
    f = w_up.shape[2]
    tm = min(FFN_ROWS, n)
    tf = FFN_COLS
    return pl.pallas_call(
        _ffn_kernel,
        grid=(n // tm, f // tf),
        in_specs=[
            pl.BlockSpec((tm, d), lambda i, j: (i, 0)),
            _resident((1, d)),
            pl.BlockSpec((None, d, tf), lambda i, j: (layer, 0, j)),
            pl.BlockSpec((None, tf, d), lambda i, j: (layer, j, 0)),
        ],
        out_specs=pl.BlockSpec((tm, d), lambda i, j: (i, 0)),
        out_shape=jax.ShapeDtypeStruct((n, d), F32),
        scratch_shapes=[pltpu.VMEM((tm, d), BF16)],
        compiler_params=_params("parallel", "arbitrary"),
        name="ffn",
    )(x2, g.reshape(1, d), w_up, w_down)


def _proj_res_kernel(x_ref, a_ref, w_ref, o_ref, wbf_ref, *, a_transposed):
    _cast_once(w_ref, wbf_ref)
    if a_transposed:
        y = lax.dot_general(a_ref[0], wbf_ref[...], (((0,), (0,)), ((), ())),
                            preferred_element_type=F32)
    else:
        y = _dot(a_ref[...], wbf_ref[...])
    o_ref[...] = x_ref[...] + y


def _proj_res(x2, a_bf, w, layer):
    n, d = x2.shape
    a_transposed = a_bf.ndim == 3
    kdim = a_bf.shape[1]
    tm = min(PROJ_ROWS, a_bf.shape[2] if a_transposed else n)
    if a_transposed:
        nt = a_bf.shape[2] // tm
        a_spec = pl.BlockSpec((1, kdim, tm), lambda i: (i // nt, 0, i % nt))
    else:
        a_spec = pl.BlockSpec((tm, kdim), lambda i: (i, 0))
    return pl.pallas_call(
        functools.partial(_proj_res_kernel, a_transposed=a_transposed),
        grid=(n // tm,),
        in_specs=[
            pl.BlockSpec((tm, d), lambda i: (i, 0)),
            a_spec,
            pl.BlockSpec((None, kdim, d), lambda i: (layer, 0, 0), pipeline_mode=pl.Buffered(1)),
        ],
        out_specs=pl.BlockSpec((tm, d), lambda i: (i, 0)),
        out_shape=jax.ShapeDtypeStruct((n, d), F32),
        scratch_shapes=[pltpu.VMEM((kdim, d), BF16)],
        compiler_params=_params("arbitrary"),
        name="proj_res",
    )(x2, a_bf, w)


def _rope_tables(s, width, rot):
    half = rot // 2
    inv = ROPE_THETA ** (-jnp.arange(half, dtype=F32) / half)
    ang = jnp.arange(s, dtype=F32)[:, None] * inv[None, :]
    cos, sin = jnp.cos(ang), jnp.sin(ang)
    pad = jnp.zeros((s, width - rot), F32)
    zero = jnp.zeros((s, half), F32)
    c = jnp.concatenate([cos, cos, pad + 1.0], axis=1)
    a = jnp.concatenate([-sin, zero, pad], axis=1)
    b = jnp.concatenate([zero, sin, pad], axis=1)
    rep = LANES // width
    return tuple(jnp.tile(t, (1, rep)) for t in (c, a, b))


def _rope_tables_t(s, rot):
    half = rot // 2
    inv = ROPE_THETA ** (-jnp.arange(half, dtype=F32) / half)
    ang = inv[:, None] * jnp.arange(s, dtype=F32)[None, :]
    return jnp.cos(ang), jnp.sin(ang)


def _rope(x, c, a, b, half):
    return x * c + pltpu.roll(x, LANES - half, 1) * a + pltpu.roll(x, half, 1) * b


def _rope_t(xt, cos, sin):
    half = cos.shape[0]
    x1, x2 = xt[:half], xt[half:2 * half]
    return jnp.concatenate([x1 * cos - x2 * sin, x2 * cos + x1 * sin, xt[2 * half:]], axis=0)


def _q_proj_kernel(x_ref, g_ref, w_ref, gain_ref, cos_ref, sin_ref, qt_ref, qn_ref, wbf_ref):
    _cast_once(w_ref, wbf_ref, transpose=True)
    h = _rms(x_ref[...], g_ref[...]).astype(BF16)
    tm = h.shape[0]
    gain = jnp.concatenate([gain_ref[...]] * (tm // LANES), axis=1)
    cos, sin = cos_ref[...], sin_ref[...]
    for pair in range(wbf_ref.shape[1] // MXU_COLS):
        q = _dot(h, wbf_ref[:, pair * MXU_COLS:(pair + 1) * MXU_COLS])
        for hd in range(MXU_COLS // HEAD_DIM):
            qh = q[:, hd * HEAD_DIM:(hd + 1) * HEAD_DIM].T
            qh = qh * lax.rsqrt(jnp.mean(qh * qh, axis=0, keepdims=True) + EPS) * gain
            qh = _rope_t(qh, cos, sin) * (HEAD_DIM ** -0.5 * LOG2E)
            row = pair * MXU_COLS + hd * HEAD_DIM
            qt_ref[0, row:row + HEAD_DIM, :] = qh.astype(BF16)
            head = row // HEAD_DIM
            qn_ref[0, head:head + 1, :] = jnp.sqrt(jnp.sum(qh * qh, axis=0, keepdims=True))


def _kv_proj_kernel(x_ref, g_ref, w_ref, wtail_ref, gain_ref, c_ref, a_ref, b_ref, ci_ref, ai_ref,
                    bi_ref, cosi_ref, sini_ref, k_ref, vt_ref, qit_ref, ki_ref, wit_ref, wbf_ref):
    _cast_once(w_ref, wbf_ref, transpose=True)
    h = _rms(x_ref[...], g_ref[...]).astype(BF16)
    y = _dot(h, wbf_ref[...])
    gain = gain_ref[...]
    c, a, b = c_ref[...], a_ref[...], b_ref[...]
    half = HEAD_DIM // ROT_FRACTION // 2
    halfi = IDX_DIM // ROT_FRACTION // 2
    nkv = N_KV_HEADS * HEAD_DIM
    for hd in range(N_KV_HEADS):
        sl = slice(hd * HEAD_DIM, (hd + 1) * HEAD_DIM)
        k_ref[:, sl] = _rope(_rms(y[:, sl], gain), c, a, b, half).astype(BF16)
        vt_ref[0, sl, :] = y[:, nkv + hd * HEAD_DIM:nkv + (hd + 1) * HEAD_DIM].T.astype(BF16)
    nqi = IDX_HEADS * IDX_DIM
    cosi, sini = cosi_ref[...], sini_ref[...]
    for hd in range(IDX_HEADS):
        col = 2 * nkv + hd * IDX_DIM
        if hd % 2 == 0:
            pair_t = y[:, col:col + LANES].T
        qh = pair_t[(hd % 2) * IDX_DIM:(hd % 2 + 1) * IDX_DIM]
        qit_ref[0, hd * IDX_DIM:(hd + 1) * IDX_DIM, :] = _rope_t(qh, cosi, sini).astype(BF16)
    kw = _dot(h, wtail_ref[...])
    ki_ref[...] = _rope(kw, ci_ref[...], ai_ref[...], bi_ref[...], halfi)[:, :IDX_DIM].astype(BF16)
    wit_ref[0] = kw.T[IDX_DIM:IDX_DIM + IDX_HEADS] * (IDX_HEADS ** -0.5 * IDX_DIM ** -0.5)


def _key_to_f32(key):
    bits = jnp.where(key < 0, key ^ jnp.int32(0x7FFFFFFF), key)
    f = pltpu.bitcast(bits, F32)
    return jnp.where(f != f, jnp.inf, f)


def _attn_kernel(qt_ref, qn_ref, qit_ref, wit_ref, k_ref, vt_ref, ki_ref, kgain_ref, o_ref, sc_ref,
                 bias_ref, *, tk, tkf, n_sel):
    t0 = pl.program_id(1) * Q_BLOCK
    nk = (t0 + Q_BLOCK + tk - 1) // tk
    q_pos = t0 + lax.broadcasted_iota(jnp.int32, (1, Q_BLOCK), 1)
    key_end = (lax.shift_right_logical(q_pos, CHUNK.bit_length() - 1) + 1) * CHUNK
    key_row = lax.broadcasted_iota(jnp.int32, (tk, Q_BLOCK), 0)

    qit = qit_ref[0]
    wit = wit_ref[0]
    npair = IDX_HEADS // 2
    rhs = [jnp.concatenate([qit[(2 * p) * IDX_DIM:(2 * p + 1) * IDX_DIM],
                            qit[(2 * p + 1) * IDX_DIM:(2 * p + 2) * IDX_DIM]], axis=1)
           for p in range(npair)]

    def over_key_tiles(step, init):
        if tkf != 2 * tk:
            return lax.fori_loop(0, nk, lambda kt, c: step(pl.multiple_of(kt * tk, tk), tk, c), init)
        carry = lax.fori_loop(
            0, nk // 2, lambda kt, c: step(pl.multiple_of(kt * tkf, tkf), tkf, c), init)
        return lax.fori_loop(
            0, nk % 2, lambda _, c: step(pl.multiple_of((nk // 2) * tkf, tk), tk, c), carry)

    def score_tile(off, size, carry):
        ki_t = ki_ref[0, pl.ds(off, size), :]
        acc = jnp.zeros((size, Q_BLOCK), F32)
        for p in range(npair):
            d = jnp.maximum(_dot(ki_t, rhs[p]), 0.0)
            acc = acc + d[:, :Q_BLOCK] * wit[2 * p:2 * p + 1, :]
            acc = acc + d[:, Q_BLOCK:] * wit[2 * p + 1:2 * p + 2, :]
        row = lax.broadcasted_iota(jnp.int32, (size, Q_BLOCK), 0)
        sc_ref[pl.ds(off, size), :] = jnp.where(row < key_end - off, acc, -jnp.inf)
        return carry

    over_key_tiles(score_tile, 0)

    count_row = lax.broadcasted_iota(jnp.int32, (COUNT_ROWS, Q_BLOCK), 0)

    def count(pred):
        def step(off, size, c):
            for r in range(0, size, COUNT_ROWS):
                x = sc_ref[pl.ds(off + r, COUNT_ROWS), :]
                c = c + jnp.where(pred(x, count_row + (off + r)), 1.0, 0.0)
            return c

        c = over_key_tiles(step, jnp.zeros((COUNT_ROWS, Q_BLOCK), F32))
        return jnp.sum(c, axis=0, keepdims=True)

    def bit_body(bi, carry):
        key, cnt = carry
        cand = key + lax.shift_left(jnp.int32(1), 31 - bi)
        cand_f = _key_to_f32(cand)
        c = count(lambda x, pos: x >= cand_f)
        ok = c >= n_sel
        return jnp.where(ok, cand, key), jnp.where(ok, c, cnt)

    nbits = jnp.where(t0 + Q_BLOCK <= n_sel, 0, 32)
    key, cnt = lax.fori_loop(0, nbits, bit_body, (jnp.full((1, Q_BLOCK), INT_MIN, jnp.int32),
                                                  jnp.zeros((1, Q_BLOCK), F32)))
    thr = jnp.where(key == INT_MIN, jnp.finfo(F32).min, _key_to_f32(key))

    @pl.when(jnp.max(cnt) > n_sel)
    def _():
        need = n_sel - count(lambda x, pos: x > thr)
        pos_bits = (sc_ref.shape[0] - 1).bit_length() + 1

        def body(bi, end):
            cand = end + lax.shift_left(jnp.int32(1), pos_bits - 1 - bi)
            c = count(lambda x, pos: jnp.logical_and(x == thr, pos < cand))
            return jnp.where(c <= need, cand, end)

        pos_end = lax.fori_loop(0, pos_bits, body, jnp.zeros((1, Q_BLOCK), jnp.int32))

        def retire(kt, carry):
            off = pl.multiple_of(kt * tk, tk)
            x = sc_ref[pl.ds(off, tk), :]
            late_tie = jnp.logical_and(x == thr, key_row + off >= pos_end)
            sc_ref[pl.ds(off, tk), :] = jnp.where(late_tie, -jnp.inf, x)
            return carry

        lax.fori_loop(0, nk, retire, 0)

    def bias_tile(kt, carry):
        off = pl.multiple_of(kt * tk, tk)
        sel = sc_ref[pl.ds(off, tk), :] >= thr
        bias_ref[pl.ds(off, tk), :] = jnp.where(sel, 0.0, MASK_BIAS).astype(BF16)
        return carry

    lax.fori_loop(0, nk, bias_tile, 0)

    qt = qt_ref[0]
    rep = qt.shape[0] // HEAD_DIM // N_KV_HEADS
    cols = rep * Q_BLOCK
    eye = (lax.broadcasted_iota(jnp.int32, (Q_BLOCK, Q_BLOCK), 0)
           == lax.broadcasted_iota(jnp.int32, (Q_BLOCK, Q_BLOCK), 1))
    eye = jnp.where(eye, 1.0, 0.0).astype(BF16)
    gsls = [slice(g * HEAD_DIM, (g + 1) * HEAD_DIM) for g in range(N_KV_HEADS)]
    qaugs = []
    for g in range(N_KV_HEADS):
        qg = jnp.concatenate(
            [qt[(g * rep + r) * HEAD_DIM:(g * rep + r + 1) * HEAD_DIM] for r in range(rep)], axis=1)
        qaugs.append(jnp.concatenate([qg, jnp.concatenate([eye] * rep, axis=1)], axis=0))

    def flash(shift):
        def step(off, size, accs):
            bias_t = bias_ref[pl.ds(off, size), :]
            ones = jnp.ones((ONES_ROWS, size), BF16)
            out = []
            for g in range(N_KV_HEADS):
                kaug = jnp.concatenate([k_ref[0, pl.ds(off, size), gsls[g]], bias_t], axis=1)
                vaug = jnp.concatenate([vt_ref[0, gsls[g], pl.ds(off, size)], ones], axis=0)
                p = jnp.exp2(_dot(kaug, qaugs[g]) - shift[g]).astype(BF16)
                out.append(accs[g] + _dot(vaug, p))
            return tuple(out)

        zero = jnp.zeros((HEAD_DIM + ONES_ROWS, cols), F32)
        return over_key_tiles(step, (zero,) * N_KV_HEADS)

    def column_max():
        def step(off, size, ms):
            bias_t = bias_ref[pl.ds(off, size), :]
            out = []
            for g in range(N_KV_HEADS):
                kaug = jnp.concatenate([k_ref[0, pl.ds(off, size), gsls[g]], bias_t], axis=1)
                out.append(jnp.maximum(ms[g], jnp.max(_dot(kaug, qaugs[g]), axis=0, keepdims=True)))
            return tuple(out)

        return over_key_tiles(step, (jnp.full((1, cols), MASK_BIAS, F32),) * N_KV_HEADS)

    def write(accs):
        for g in range(N_KV_HEADS):
            o = accs[g][:HEAD_DIM] / accs[g][HEAD_DIM:HEAD_DIM + 1]
            for r in range(rep):
                hsl = slice((g * rep + r) * HEAD_DIM, (g * rep + r + 1) * HEAD_DIM)
                o_ref[0, hsl, :] = o[:, r * Q_BLOCK:(r + 1) * Q_BLOCK].astype(BF16)

    kmax = (HEAD_DIM ** 0.5) * jnp.max(jnp.abs(kgain_ref[...]), axis=1, keepdims=True)
    qn = qn_ref[0]
    bound = [jnp.concatenate([qn[g * rep + r:g * rep + r + 1] for r in range(rep)], axis=1) * kmax
             for g in range(N_KV_HEADS)]
    accs = flash(bound)
    den = jnp.concatenate([acc[HEAD_DIM:HEAD_DIM + 1] for acc in accs], axis=1)
    safe = jnp.logical_and(jnp.min(den) >= DEN_MIN, jnp.max(den) <= 1.0 / DEN_MIN)

    @pl.when(safe)
    def _():
        write(accs)

    @pl.when(jnp.logical_not(safe))
    def _():
        write(flash(column_max()))


def _sparse_attention(x, g, w_in, q_gain, k_gain, w_o, layer):
    b, s, d = x.shape
    n = b * s
    x2 = x.reshape(n, d)
    nq = d
    nkv = N_KV_HEADS * HEAD_DIM
    nqi = IDX_HEADS * IDX_DIM
    nmid = 2 * nkv + nqi
    assert nmid == nq
    pad = LANES - IDX_DIM - IDX_HEADS
    w_tail = jnp.pad(w_in[layer, :, nq + nmid:], ((0, 0), (0, pad))).astype(BF16)
    w_in_t = jnp.swapaxes(w_in, 1, 2)
    wblock = lambda blk: pl.BlockSpec((None, nq, d), lambda i: (layer, blk, 0),
                                      pipeline_mode=pl.Buffered(1))
    c, a, bt = _rope_tables(s, HEAD_DIM, HEAD_DIM // ROT_FRACTION)
    ci, ai, bi = _rope_tables(s, IDX_DIM, IDX_DIM // ROT_FRACTION)
    cos_t, sin_t = _rope_tables_t(s, HEAD_DIM // ROT_FRACTION)
    cosi_t, sini_t = _rope_tables_t(s, IDX_DIM // ROT_FRACTION)
    q_gain_b = jnp.broadcast_to(q_gain[:, None], (HEAD_DIM, LANES))

    tm = min(PROJ_ROWS, s)
    nt = s // tm
    row = lambda i: (i, 0)
    pos = lambda i: (i % nt, 0)
    tcol = lambda i: (i // nt, 0, i % nt)
    tab = pl.BlockSpec((tm, LANES), pos)
    tab_t = lambda half: pl.BlockSpec((half, tm), lambda i: (0, i % nt))
    half = HEAD_DIM // ROT_FRACTION // 2
    halfi = IDX_DIM // ROT_FRACTION // 2
    qt, qn = pl.pallas_call(
        _q_proj_kernel,
        grid=(n // tm,),
        in_specs=[pl.BlockSpec((tm, d), row), _resident((1, d)), wblock(0),
                  _resident((HEAD_DIM, LANES)), tab_t(half), tab_t(half)],
        out_specs=[pl.BlockSpec((1, nq, tm), tcol), pl.BlockSpec((1, nq // HEAD_DIM, tm), tcol)],
        out_shape=[jax.ShapeDtypeStruct((b, nq, s), BF16),
                   jax.ShapeDtypeStruct((b, nq // HEAD_DIM, s), F32)],
        scratch_shapes=[pltpu.VMEM((d, nq), BF16)],
        compiler_params=_params("arbitrary"),
        name="attn_q_proj",
    )(x2, g.reshape(1, d), w_in_t, q_gain_b, cos_t, sin_t)

    k, vt, qit, ki, wit = pl.pallas_call(
        _kv_proj_kernel,
        grid=(n // tm,),
        in_specs=[pl.BlockSpec((tm, d), row), _resident((1, d)), wblock(1), _resident((d, LANES)),
                  _resident((1, HEAD_DIM)), tab, tab, tab, tab, tab, tab, tab_t(halfi), tab_t(halfi)],
        out_specs=[pl.BlockSpec((tm, nkv), row), pl.BlockSpec((1, nkv, tm), tcol),
                   pl.BlockSpec((1, nqi, tm), tcol), pl.BlockSpec((tm, IDX_DIM), row),
                   pl.BlockSpec((1, IDX_HEADS, tm), tcol)],
        out_shape=[jax.ShapeDtypeStruct((n, nkv), BF16), jax.ShapeDtypeStruct((b, nkv, s), BF16),
                   jax.ShapeDtypeStruct((b, nqi, s), BF16), jax.ShapeDtypeStruct((n, IDX_DIM), BF16),
                   jax.ShapeDtypeStruct((b, IDX_HEADS, s), F32)],
        scratch_shapes=[pltpu.VMEM((d, nmid), BF16)],
        compiler_params=_params("arbitrary"),
        name="attn_kv_proj",
    )(x2, g.reshape(1, d), w_in_t, w_tail, k_gain.reshape(1, HEAD_DIM), c, a, bt, ci, ai, bi, cosi_t,
      sini_t)

    n_sel = min(INDEX_TOPK, s // 4)
    tk = min(SCORE_KEYS, s)
    qcol = lambda bi_, qb: (bi_, 0, qb)
    full = lambda bi_, qb: (bi_, 0, 0)
    o = pl.pallas_call(
        functools.partial(_attn_kernel, tk=tk, tkf=min(FLASH_KEYS, s), n_sel=n_sel),
        grid=(b, s // Q_BLOCK),
        in_specs=[pl.BlockSpec((1, nq, Q_BLOCK), qcol), pl.BlockSpec((1, nq // HEAD_DIM, Q_BLOCK), qcol),
                  pl.BlockSpec((1, nqi, Q_BLOCK), qcol),
                  pl.BlockSpec((1, IDX_HEADS, Q_BLOCK), qcol), pl.BlockSpec((1, s, nkv), full),
                  pl.BlockSpec((1, nkv, s), full), pl.BlockSpec((1, s, IDX_DIM), full),
                  pl.BlockSpec((1, HEAD_DIM), lambda bi_, qb: (0, 0))],
        out_specs=pl.BlockSpec((1, nq, Q_BLOCK), qcol),
        out_shape=jax.ShapeDtypeStruct((b, nq, s), BF16),
        scratch_shapes=[pltpu.VMEM((s, Q_BLOCK), F32), pltpu.VMEM((s, Q_BLOCK), BF16)],
        compiler_params=_params("parallel", "arbitrary"),
        name="sparse_attn",
    )(qt, qn, qit, wit, k.reshape(b, s, nkv), vt, ki.reshape(b, s, IDX_DIM),
      k_gain.reshape(1, HEAD_DIM))
    return _proj_res(x2, o, w_o, layer).reshape(b, s, d)


def _sgu_kernel(x_ref, g_ref, w_ref, b_ref, vg_ref, ws_ref, bs_ref, o_ref, *, tm, sub):
    ii = lax.broadcasted_iota(jnp.int32, (SGU_BLOCK, SGU_BLOCK), 0) // CHUNK
    jj = lax.broadcasted_iota(jnp.int32, (SGU_BLOCK, SGU_BLOCK), 1) // CHUNK
    causal = jj <= ii
    ws = [jnp.where(causal, ws_ref[gi], 0.0).astype(BF16) for gi in range(SGU_GROUPS)]
    bs = bs_ref[...]
    width = w_ref.shape[1] // 2
    gd = width // SGU_GROUPS
    for st in range(tm // sub):
        r0 = st * sub
        h = _rms(x_ref[r0:r0 + sub, :], g_ref[...]).astype(BF16)
        z = _dot(h, w_ref[...]) + b_ref[...]
        z = 0.5 * z * (1.0 + lax.erf(z * (2.0 ** -0.5)))
        u = z[:, :width]
        v = _rms(z[:, width:], vg_ref[...]).astype(BF16)
        for gi in range(SGU_GROUPS):
            cs = slice(gi * gd, (gi + 1) * gd)
            for nb in range(sub // SGU_BLOCK):
                rs = slice(nb * SGU_BLOCK, (nb + 1) * SGU_BLOCK)
                mixed = _dot(ws[gi], v[rs, cs]) + bs[:, gi:gi + 1]
                o_ref[r0 + nb * SGU_BLOCK:r0 + (nb + 1) * SGU_BLOCK, cs] = (u[rs, cs] * mixed).astype(BF16)


def _spatial_gating(x, g, w_in, b_in, v_gain, w_s, b_s, w_o, layer):
    b, s, d = x.shape
    n = b * s
    x2 = x.reshape(n, d)
    width = w_in.shape[1] // 2
    tm = min(SGU_ROWS, s)
    row = lambda i: (i, 0)
    gated = pl.pallas_call(
        functools.partial(_sgu_kernel, tm=tm, sub=min(SGU_SUB_ROWS, tm)),
        grid=(n // tm,),
        in_specs=[pl.BlockSpec((tm, d), row), _resident((1, d)), _resident((d, 2 * width)),
                  _resident((1, 2 * width)), _resident((1, width)),
                  _resident((SGU_GROUPS, SGU_BLOCK, SGU_BLOCK)), _resident((SGU_BLOCK, SGU_GROUPS))],
        out_specs=pl.BlockSpec((tm, width), row),
        out_shape=jax.ShapeDtypeStruct((n, width), BF16),
        compiler_params=_params("parallel"),
        name="sgu_gate",
    )(x2, g.reshape(1, d), w_in.astype(BF16), b_in.reshape(1, 2 * width), v_gain.reshape(1, width),
      w_s, b_s.T)
    return _proj_res(x2, gated, w_o, layer).reshape(b, s, d)


def kernel(x, norm_mix, norm_ffn, pool_w, pool_scale, attn_w_in, attn_q_gain, attn_k_gain, attn_w_o,
           sgu_w_in, sgu_b_in, sgu_v_gain, sgu_w_s, sgu_b_s, sgu_w_o, ffn_w_up, ffn_w_down):
    b, s, d = x.shape
    depth = norm_mix.shape[0]
    for i in range(depth):
        kind, j = i % 3, i // 3
        if kind == 0:
            x = _pool_mixer(x, norm_mix[i], pool_w[j].astype(BF16), pool_scale[j])
        elif kind == 1:
            x = _sparse_attention(x, norm_mix[i], attn_w_in, attn_q_gain[j], attn_k_gain[j],
                                  attn_w_o, j)
        else:
            x = _spatial_gating(x, norm_mix[i], sgu_w_in[j], sgu_b_in[j], sgu_v_gain[j], sgu_w_s[j],
                                sgu_b_s[j], sgu_w_o, j)
        x = _ffn(x.reshape(b * s, d), norm_ffn[i], ffn_w_up, ffn_w_down, i).reshape(b, s, d)
    return x
```

```python
import functools

import jax
import jax.numpy as jnp
from jax import lax
from jax.experimental import pallas as pl
from jax.experimental.pallas import tpu as pltpu

EPS = 1e-6
CHUNK = 64
POOL_WINDOWS = (2, 4, 8, 16)
POOL_HALO = 16
HEAD_DIM = 128
N_KV_HEADS = 4
IDX_HEADS = 16
IDX_DIM = 64
INDEX_TOPK = 256
Q_BLOCK = 128
ROPE_THETA = 500000.0
ROT_FRACTION = 4
SGU_BLOCK = 128
SGU_GROUPS = 8
LANES = 128
MXU_COLS = 256
assert CHUNK & (CHUNK - 1) == 0
INT_MIN = -(2 ** 31)
COUNT_ROWS = 64
ONES_ROWS = 16
DEN_MIN = 2.0 ** -60
LOG2E = 1.4426950408889634
MASK_BIAS = -1e30
VMEM_LIMIT_BYTES = 60 * 1024 * 1024

POOL_ROWS = 1024
FFN_ROWS = 1024
FFN_COLS = 512
PROJ_ROWS = 512
SGU_ROWS = 512
SGU_SUB_ROWS = 256
SCORE_KEYS = 512
FLASH_KEYS = 1024

F32 = jnp.float32
BF16 = jnp.bfloat16


def _params(*sem):
    return pltpu.CompilerParams(dimension_semantics=sem, vmem_limit_bytes=VMEM_LIMIT_BYTES)


def _resident(shape):
    nd = len(shape)
    return pl.BlockSpec(shape, lambda *_: (0,) * nd, pipeline_mode=pl.Buffered(1))


def _cast_once(w_ref, wbf_ref, transpose=False):
    @pl.when(pl.program_id(0) == 0)
    def _():
        if transpose:
            for r in range(0, w_ref.shape[0], LANES):
                wbf_ref[:, r:r + LANES] = w_ref[r:r + LANES, :].T.astype(BF16)
        else:
            wbf_ref[...] = w_ref[...].astype(BF16)


def _rms(xf, g):
    ms = jnp.mean(xf * xf, axis=-1, keepdims=True)
    return xf * lax.rsqrt(ms + EPS) * g


def _dot(a, b):
    return jnp.dot(a, b, preferred_element_type=F32)


def _pool_kernel(x_ref, halo_ref, g_ref, w_ref, scale_ref, o_ref, *, ts):
    i = pl.program_id(1)
    x = x_ref[0]
    g = g_ref[...]
    h = _rms(x, g)
    hh = _rms(halo_ref[0], g)
    hh = jnp.where(i > 0, hh, 0.0)
    hf = jnp.concatenate([hh, h], axis=0)
    t1 = (i * ts + lax.broadcasted_iota(jnp.int32, (ts, 1), 0) + 1).astype(F32)
    cg = x.shape[1] // len(POOL_WINDOWS)
    for gi, w in enumerate(POOL_WINDOWS):
        sl = slice(gi * cg, (gi + 1) * cg)
        s = hf[:, sl]
        k = 1
        while k < w:
            s = s + pltpu.roll(s, k, 0)
            k *= 2
        mean = s[POOL_HALO:] / jnp.minimum(t1, float(w))
        p = (mean - h[:, sl]).astype(BF16)
        y = _dot(p, w_ref[gi]) * scale_ref[:, sl]
        o_ref[0, :, sl] = x[:, sl] + y


def _pool_mixer(x, g, w_bf, scale):
    b, s, d = x.shape
    ts = min(POOL_ROWS, s)
    hb = ts // POOL_HALO
    ng = len(POOL_WINDOWS)
    return pl.pallas_call(
        functools.partial(_pool_kernel, ts=ts),
        grid=(b, s // ts),
        in_specs=[
            pl.BlockSpec((1, ts, d), lambda bi, i: (bi, i, 0)),
            pl.BlockSpec((1, POOL_HALO, d), lambda bi, i: (bi, jnp.maximum(i * hb - 1, 0), 0)),
            _resident((1, d)),
            _resident((ng, d // ng, d // ng)),
            _resident((1, d)),
        ],
        out_specs=pl.BlockSpec((1, ts, d), lambda bi, i: (bi, i, 0)),
        out_shape=jax.ShapeDtypeStruct((b, s, d), F32),
        compiler_params=_params("parallel", "parallel"),
        name="pool_mixer",
    )(x, x, g.reshape(1, d), w_bf, scale.reshape(1, d))


def _ffn_kernel(x_ref, g_ref, wu_ref, wd_ref, o_ref, h_ref):
    j = pl.program_id(1)

    @pl.when(j == 0)
    def _():
        x = x_ref[...]
        h_ref[...] = _rms(x, g_ref[...]).astype(BF16)
        o_ref[...] = x

    u = _dot(h_ref[...], wu_ref[...].astype(BF16))
    a = jnp.square(jnp.maximum(u, 0.0)).astype(BF16)
    o_ref[...] += _dot(a, wd_ref[...].astype(BF16))


def _ffn(x2, g, w_up, w_down, layer):
    n, d = x2.shape
    f = w_up.shape[2]
    tm = min(FFN_ROWS, n)
    tf = FFN_COLS
    return pl.pallas_call(
        _ffn_kernel,
        grid=(n // tm, f // tf),
        in_specs=[
            pl.BlockSpec((tm, d), lambda i, j: (i, 0)),
            _resident((1, d)),
            pl.BlockSpec((None, d, tf), lambda i, j: (layer, 0, j)),
            pl.BlockSpec((None, tf, d), lambda i, j: (layer, j, 0)),
        ],
        out_specs=pl.BlockSpec((tm, d), lambda i, j: (i, 0)),
        out_shape=jax.ShapeDtypeStruct((n, d), F32),
        scratch_shapes=[pltpu.VMEM((tm, d), BF16)],
        compiler_params=_params("parallel", "arbitrary"),
        name="ffn",
    )(x2, g.reshape(1, d), w_up, w_down)


def _proj_res_kernel(x_ref, a_ref, w_ref, o_ref, wbf_ref, *, a_transposed):
    _cast_once(w_ref, wbf_ref)
    if a_transposed:
        y = lax.dot_general(a_ref[0], wbf_ref[...], (((0,), (0,)), ((), ())),
                            preferred_element_type=F32)
    else:
        y = _dot(a_ref[...], wbf_ref[...])
    o_ref[...] = x_ref[...] + y


def _proj_res(x2, a_bf, w, layer):
    n, d = x2.shape
    a_transposed = a_bf.ndim == 3
    kdim = a_bf.shape[1]
    tm = min(PROJ_ROWS, a_bf.shape[2] if a_transposed else n)
    if a_transposed:
        nt = a_bf.shape[2] // tm
        a_spec = pl.BlockSpec((1, kdim, tm), lambda i: (i // nt, 0, i % nt))
    else:
        a_spec = pl.BlockSpec((tm, kdim), lambda i: (i, 0))
    return pl.pallas_call(
        functools.partial(_proj_res_kernel, a_transposed=a_transposed),
        grid=(n // tm,),
        in_specs=[
            pl.BlockSpec((tm, d), lambda i: (i, 0)),
            a_spec,
            pl.BlockSpec((None, kdim, d), lambda i: (layer, 0, 0), pipeline_mode=pl.Buffered(1)),
        ],
        out_specs=pl.BlockSpec((tm, d), lambda i: (i, 0)),
        out_shape=jax.ShapeDtypeStruct((n, d), F32),
        scratch_shapes=[pltpu.VMEM((kdim, d), BF16)],
        compiler_params=_params("arbitrary"),
        name="proj_res",
    )(x2, a_bf, w)


def _rope_tables(s, width, rot):
    half = rot // 2
    inv = ROPE_THETA ** (-jnp.arange(half, dtype=F32) / half)
    ang = jnp.arange(s, dtype=F32)[:, None] * inv[None, :]
    cos, sin = jnp.cos(ang), jnp.sin(ang)
    pad = jnp.zeros((s, width - rot), F32)
    zero = jnp.zeros((s, half), F32)
    c = jnp.concatenate([cos, cos, pad + 1.0], axis=1)
    a = jnp.concatenate([-sin, zero, pad], axis=1)
    b = jnp.concatenate([zero, sin, pad], axis=1)
    rep = LANES // width
    return tuple(jnp.tile(t, (1, rep)) for t in (c, a, b))


def _rope_tables_t(s, rot):
    half = rot // 2
    inv = ROPE_THETA ** (-jnp.arange(half, dtype=F32) / half)
    ang = inv[:, None] * jnp.arange(s, dtype=F32)[None, :]
    return jnp.cos(ang), jnp.sin(ang)


def _rope(x, c, a, b, half):
    return x * c + pltpu.roll(x, LANES - half, 1) * a + pltpu.roll(x, half, 1) * b


def _rope_t(xt, cos, sin):
    half = cos.shape[0]
    x1, x2 = xt[:half], xt[half:2 * half]
    return jnp.concatenate([x1 * cos - x2 * sin, x2 * cos + x1 * sin, xt[2 * half:]], axis=0)


def _q_proj_kernel(x_ref, g_ref, w_ref, gain_ref, cos_ref, sin_ref, qt_ref, qn_ref, wbf_ref):
    _cast_once(w_ref, wbf_ref, transpose=True)
    h = _rms(x_ref[...], g_ref[...]).astype(BF16)
    tm = h.shape[0]
    gain = jnp.concatenate([gain_ref[...]] * (tm // LANES), axis=1)
    cos, sin = cos_ref[...], sin_ref[...]
    for pair in range(wbf_ref.shape[1] // MXU_COLS):
        q = _dot(h, wbf_ref[:, pair * MXU_COLS:(pair + 1) * MXU_COLS])
        for hd in range(MXU_COLS // HEAD_DIM):
            qh = q[:, hd * HEAD_DIM:(hd + 1) * HEAD_DIM].T
            qh = qh * lax.rsqrt(jnp.mean(qh * qh, axis=0, keepdims=True) + EPS) * gain
            qh = _rope_t(qh, cos, sin) * (HEAD_DIM ** -0.5 * LOG2E)
            row = pair * MXU_COLS + hd * HEAD_DIM
            qt_ref[0, row:row + HEAD_DIM, :] = qh.astype(BF16)
            head = row // HEAD_DIM
            qn_ref[0, head:head + 1, :] = jnp.sqrt(jnp.sum(qh * qh, axis=0, keepdims=True))


def _kv_proj_kernel(x_ref, g_ref, w_ref, wtail_ref, gain_ref, c_ref, a_ref, b_ref, ci_ref, ai_ref,
                    bi_ref, cosi_ref, sini_ref, k_ref, vt_ref, qit_ref, ki_ref, wit_ref, wbf_ref):
    _cast_once(w_ref, wbf_ref, transpose=True)
    h = _rms(x_ref[...], g_ref[...]).astype(BF16)
    y = _dot(h, wbf_ref[...])
    gain = gain_ref[...]
    c, a, b = c_ref[...], a_ref[...], b_ref[...]
    half = HEAD_DIM // ROT_FRACTION // 2
    halfi = IDX_DIM // ROT_FRACTION // 2
    nkv = N_KV_HEADS * HEAD_DIM
    for hd in range(N_KV_HEADS):
        sl = slice(hd * HEAD_DIM, (hd + 1) * HEAD_DIM)
        k_ref[:, sl] = _rope(_rms(y[:, sl], gain), c, a, b, half).astype(BF16)
        vt_ref[0, sl, :] = y[:, nkv + hd * HEAD_DIM:nkv + (hd + 1) * HEAD_DIM].T.astype(BF16)
    nqi = IDX_HEADS * IDX_DIM
    cosi, sini = cosi_ref[...], sini_ref[...]
    for hd in range(IDX_HEADS):
        col = 2 * nkv + hd * IDX_DIM
        if hd % 2 == 0:
            pair_t = y[:, col:col + LANES].T
        qh = pair_t[(hd % 2) * IDX_DIM:(hd % 2 + 1) * IDX_DIM]
        qit_ref[0, hd * IDX_DIM:(hd + 1) * IDX_DIM, :] = _rope_t(qh, cosi, sini).astype(BF16)
    kw = _dot(h, wtail_ref[...])
    ki_ref[...] = _rope(kw, ci_ref[...], ai_ref[...], bi_ref[...], halfi)[:, :IDX_DIM].astype(BF16)
    wit_ref[0] = kw.T[IDX_DIM:IDX_DIM + IDX_HEADS] * (IDX_HEADS ** -0.5 * IDX_DIM ** -0.5)


def _key_to_f32(key):
    bits = jnp.where(key < 0, key ^ jnp.int32(0x7FFFFFFF), key)
    f = pltpu.bitcast(bits, F32)
    return jnp.where(f != f, jnp.inf, f)


def _attn_kernel(qt_ref, qn_ref, qit_ref, wit_ref, k_ref, vt_ref, ki_ref, kgain_ref, o_ref, sc_ref,
                 bias_ref, acc_ref, *, tk, tkf, n_sel):
    t0 = pl.program_id(1) * Q_BLOCK
    nk = (t0 + Q_BLOCK + tk - 1) // tk
    q_pos = t0 + lax.broadcasted_iota(jnp.int32, (1, Q_BLOCK), 1)
    key_end = (lax.shift_right_logical(q_pos, CHUNK.bit_length() - 1) + 1) * CHUNK
    key_row = lax.broadcasted_iota(jnp.int32, (tk, Q_BLOCK), 0)

    qit = qit_ref[0]
    wit = wit_ref[0]
    npair = IDX_HEADS // 2
    rhs = [jnp.concatenate([qit[(2 * p) * IDX_DIM:(2 * p + 1) * IDX_DIM],
                            qit[(2 * p + 1) * IDX_DIM:(2 * p + 2) * IDX_DIM]], axis=1)
           for p in range(npair)]

    def over_key_tiles(step, init):
        if tkf != 2 * tk:
            return lax.fori_loop(0, nk, lambda kt, c: step(pl.multiple_of(kt * tk, tk), tk, c), init)
        carry = lax.fori_loop(
            0, nk // 2, lambda kt, c: step(pl.multiple_of(kt * tkf, tkf), tkf, c), init)
        return lax.fori_loop(
            0, nk % 2, lambda _, c: step(pl.multiple_of((nk // 2) * tkf, tk), tk, c), carry)

    def score_tile(off, size, carry):
        ki_t = ki_ref[0, pl.ds(off, size), :]
        acc = jnp.zeros((size, Q_BLOCK), F32)
        for p in range(npair):
            d = jnp.maximum(_dot(ki_t, rhs[p]), 0.0)
            acc = acc + d[:, :Q_BLOCK] * wit[2 * p:2 * p + 1, :]
            acc = acc + d[:, Q_BLOCK:] * wit[2 * p + 1:2 * p + 2, :]
        row = lax.broadcasted_iota(jnp.int32, (size, Q_BLOCK), 0)
        sc_ref[pl.ds(off, size), :] = jnp.where(row < key_end - off, acc, -jnp.inf)
        return carry

    over_key_tiles(score_tile, 0)

    count_row = lax.broadcasted_iota(jnp.int32, (COUNT_ROWS, Q_BLOCK), 0)

    def count(pred):
        def step(off, size, c):
            for r in range(0, size, COUNT_ROWS):
                x = sc_ref[pl.ds(off + r, COUNT_ROWS), :]
                c = c + jnp.where(pred(x, count_row + (off + r)), 1.0, 0.0)
            return c

        c = over_key_tiles(step, jnp.zeros((COUNT_ROWS, Q_BLOCK), F32))
        return jnp.sum(c, axis=0, keepdims=True)

    def bit_body(bi, carry):
        key, cnt = carry
        cand = key + lax.shift_left(jnp.int32(1), 31 - bi)
        cand_f = _key_to_f32(cand)
        c = count(lambda x, pos: x >= cand_f)
        ok = c >= n_sel
        return jnp.where(ok, cand, key), jnp.where(ok, c, cnt)

    nbits = jnp.where(t0 + Q_BLOCK <= n_sel, 0, 32)
    key, cnt = lax.fori_loop(0, nbits, bit_body, (jnp.full((1, Q_BLOCK), INT_MIN, jnp.int32),
                                                  jnp.zeros((1, Q_BLOCK), F32)))
    thr = jnp.where(key == INT_MIN, jnp.finfo(F32).min, _key_to_f32(key))

    @pl.when(jnp.max(cnt) > n_sel)
    def _():
        need = n_sel - count(lambda x, pos: x > thr)
        pos_bits = (sc_ref.shape[0] - 1).bit_length() + 1

        def body(bi, end):
            cand = end + lax.shift_left(jnp.int32(1), pos_bits - 1 - bi)
            c = count(lambda x, pos: jnp.logical_and(x == thr, pos < cand))
            return jnp.where(c <= need, cand, end)

        pos_end = lax.fori_loop(0, pos_bits, body, jnp.zeros((1, Q_BLOCK), jnp.int32))

        def retire(kt, carry):
            off = pl.multiple_of(kt * tk, tk)
            x = sc_ref[pl.ds(off, tk), :]
            late_tie = jnp.logical_and(x == thr, key_row + off >= pos_end)
            sc_ref[pl.ds(off, tk), :] = jnp.where(late_tie, -jnp.inf, x)
            return carry

        lax.fori_loop(0, nk, retire, 0)

    def bias_tile(kt, carry):
        off = pl.multiple_of(kt * tk, tk)
        sel = sc_ref[pl.ds(off, tk), :] >= thr
        bias_ref[pl.ds(off, tk), :] = jnp.where(sel, 0.0, MASK_BIAS).astype(BF16)
        return carry

    lax.fori_loop(0, nk, bias_tile, 0)

    qt = qt_ref[0]
    rep = qt.shape[0] // HEAD_DIM // N_KV_HEADS
    cols = rep * Q_BLOCK
    eye = (lax.broadcasted_iota(jnp.int32, (Q_BLOCK, Q_BLOCK), 0)
           == lax.broadcasted_iota(jnp.int32, (Q_BLOCK, Q_BLOCK), 1))
    eye = jnp.where(eye, 1.0, 0.0).astype(BF16)
    gsls = [slice(g * HEAD_DIM, (g + 1) * HEAD_DIM) for g in range(N_KV_HEADS)]
    qaugs = []
    for g in range(N_KV_HEADS):
        qg = jnp.concatenate(
            [qt[(g * rep + r) * HEAD_DIM:(g * rep + r + 1) * HEAD_DIM] for r in range(rep)], axis=1)
        qaugs.append(jnp.concatenate([qg, jnp.concatenate([eye] * rep, axis=1)], axis=0))

    def flash(shift):
        acc_ref[...] = jnp.zeros(acc_ref.shape, F32)

        def step(off, size, carry):
            bias_t = bias_ref[pl.ds(off, size), :]
            ones = jnp.ones((ONES_ROWS, size), BF16)
            for g in range(N_KV_HEADS):
                kaug = jnp.concatenate([k_ref[0, pl.ds(off, size), gsls[g]], bias_t], axis=1)
                vaug = jnp.concatenate([vt_ref[0, gsls[g], pl.ds(off, size)], ones], axis=0)
                p = jnp.exp2(_dot(kaug, qaugs[g]) - shift[g]).astype(BF16)
                acc_ref[g] += _dot(vaug, p)
            return carry

        over_key_tiles(step, 0)

    def column_max():
        def step(off, size, ms):
            bias_t = bias_ref[pl.ds(off, size), :]
            out = []
            for g in range(N_KV_HEADS):
                kaug = jnp.concatenate([k_ref[0, pl.ds(off, size), gsls[g]], bias_t], axis=1)
                out.append(jnp.maximum(ms[g], jnp.max(_dot(kaug, qaugs[g]), axis=0, keepdims=True)))
            return tuple(out)

        return over_key_tiles(step, (jnp.full((1, cols), MASK_BIAS, F32),) * N_KV_HEADS)


    kmax = (HEAD_DIM ** 0.5) * jnp.max(jnp.abs(kgain_ref[...]), axis=1, keepdims=True)
    qn = qn_ref[0]
    bound = [jnp.concatenate([qn[g * rep + r:g * rep + r + 1] for r in range(rep)], axis=1) * kmax
             for g in range(N_KV_HEADS)]
    flash(bound)
    den = acc_ref[:, HEAD_DIM:HEAD_DIM + 1, :]
    safe = jnp.logical_and(jnp.min(den) >= DEN_MIN, jnp.max(den) <= 1.0 / DEN_MIN)

    @pl.when(jnp.logical_not(safe))
    def _():
        flash(column_max())

    for g in range(N_KV_HEADS):
        acc = acc_ref[g]
        o = acc[:HEAD_DIM] / acc[HEAD_DIM:HEAD_DIM + 1]
        for r in range(rep):
            hsl = slice((g * rep + r) * HEAD_DIM, (g * rep + r + 1) * HEAD_DIM)
            o_ref[0, hsl, :] = o[:, r * Q_BLOCK:(r + 1) * Q_BLOCK].astype(BF16)


def _sparse_attention(x, g, w_in, q_gain, k_gain, w_o, layer):
    b, s, d = x.shape
    n = b * s
    x2 = x.reshape(n, d)
    nq = d
    nkv = N_KV_HEADS * HEAD_DIM
    nqi = IDX_HEADS * IDX_DIM
    nmid = 2 * nkv + nqi
    assert nmid == nq
    pad = LANES - IDX_DIM - IDX_HEADS
    w_tail = jnp.pad(w_in[layer, :, nq + nmid:], ((0, 0), (0, pad))).astype(BF16)
    w_in_t = jnp.swapaxes(w_in, 1, 2)
    wblock = lambda blk: pl.BlockSpec((None, nq, d), lambda i: (layer, blk, 0),
                                      pipeline_mode=pl.Buffered(1))
    c, a, bt = _rope_tables(s, HEAD_DIM, HEAD_DIM // ROT_FRACTION)
    ci, ai, bi = _rope_tables(s, IDX_DIM, IDX_DIM // ROT_FRACTION)
    cos_t, sin_t = _rope_tables_t(s, HEAD_DIM // ROT_FRACTION)
    cosi_t, sini_t = _rope_tables_t(s, IDX_DIM // ROT_FRACTION)
    q_gain_b = jnp.broadcast_to(q_gain[:, None], (HEAD_DIM, LANES))

    tm = min(PROJ_ROWS, s)
    nt = s // tm
    row = lambda i: (i, 0)
    pos = lambda i: (i % nt, 0)
    tcol = lambda i: (i // nt, 0, i % nt)
    tab = pl.BlockSpec((tm, LANES), pos)
    tab_t = lambda half: pl.BlockSpec((half, tm), lambda i: (0, i % nt))
    half = HEAD_DIM // ROT_FRACTION // 2
    halfi = IDX_DIM // ROT_FRACTION // 2
    qt, qn = pl.pallas_call(
        _q_proj_kernel,
        grid=(n // tm,),
        in_specs=[pl.BlockSpec((tm, d), row), _resident((1, d)), wblock(0),
                  _resident((HEAD_DIM, LANES)), tab_t(half), tab_t(half)],
        out_specs=[pl.BlockSpec((1, nq, tm), tcol), pl.BlockSpec((1, nq // HEAD_DIM, tm), tcol)],
        out_shape=[jax.ShapeDtypeStruct((b, nq, s), BF16),
                   jax.ShapeDtypeStruct((b, nq // HEAD_DIM, s), F32)],
        scratch_shapes=[pltpu.VMEM((d, nq), BF16)],
        compiler_params=_params("arbitrary"),
        name="attn_q_proj",
    )(x2, g.reshape(1, d), w_in_t, q_gain_b, cos_t, sin_t)

    k, vt, qit, ki, wit = pl.pallas_call(
        _kv_proj_kernel,
        grid=(n // tm,),
        in_specs=[pl.BlockSpec((tm, d), row), _resident((1, d)), wblock(1), _resident((d, LANES)),
                  _resident((1, HEAD_DIM)), tab, tab, tab, tab, tab, tab, tab_t(halfi), tab_t(halfi)],
        out_specs=[pl.BlockSpec((tm, nkv), row), pl.BlockSpec((1, nkv, tm), tcol),
                   pl.BlockSpec((1, nqi, tm), tcol), pl.BlockSpec((tm, IDX_DIM), row),
                   pl.BlockSpec((1, IDX_HEADS, tm), tcol)],
        out_shape=[jax.ShapeDtypeStruct((n, nkv), BF16), jax.ShapeDtypeStruct((b, nkv, s), BF16),
                   jax.ShapeDtypeStruct((b, nqi, s), BF16), jax.ShapeDtypeStruct((n, IDX_DIM), BF16),
                   jax.ShapeDtypeStruct((b, IDX_HEADS, s), F32)],
        scratch_shapes=[pltpu.VMEM((d, nmid), BF16)],
        compiler_params=_params("arbitrary"),
        name="attn_kv_proj",
    )(x2, g.reshape(1, d), w_in_t, w_tail, k_gain.reshape(1, HEAD_DIM), c, a, bt, ci, ai, bi, cosi_t,
      sini_t)

    n_sel = min(INDEX_TOPK, s // 4)
    tk = min(SCORE_KEYS, s)
    qcol = lambda bi_, qb: (bi_, 0, qb)
    full = lambda bi_, qb: (bi_, 0, 0)
    o = pl.pallas_call(
        functools.partial(_attn_kernel, tk=tk, tkf=min(FLASH_KEYS, s), n_sel=n_sel),
        grid=(b, s // Q_BLOCK),
        in_specs=[pl.BlockSpec((1, nq, Q_BLOCK), qcol), pl.BlockSpec((1, nq // HEAD_DIM, Q_BLOCK), qcol),
                  pl.BlockSpec((1, nqi, Q_BLOCK), qcol),
                  pl.BlockSpec((1, IDX_HEADS, Q_BLOCK), qcol), pl.BlockSpec((1, s, nkv), full),
                  pl.BlockSpec((1, nkv, s), full), pl.BlockSpec((1, s, IDX_DIM), full),
                  pl.BlockSpec((1, HEAD_DIM), lambda bi_, qb: (0, 0))],
        out_specs=pl.BlockSpec((1, nq, Q_BLOCK), qcol),
        out_shape=jax.ShapeDtypeStruct((b, nq, s), BF16),
        scratch_shapes=[pltpu.VMEM((s, Q_BLOCK), F32), pltpu.VMEM((s, Q_BLOCK), BF16),
                        pltpu.VMEM((N_KV_HEADS, HEAD_DIM + ONES_ROWS,
                                    nq // HEAD_DIM // N_KV_HEADS * Q_BLOCK), F32)],
        compiler_params=_params("parallel", "arbitrary"),
        name="sparse_attn",
    )(qt, qn, qit, wit, k.reshape(b, s, nkv), vt, ki.reshape(b, s, IDX_DIM),
      k_gain.reshape(1, HEAD_DIM))
    return _proj_res(x2, o, w_o, layer).reshape(b, s, d)


def _sgu_kernel(x_ref, g_ref, w_ref, b_ref, vg_ref, ws_ref, bs_ref, o_ref, *, tm, sub):
    ii = lax.broadcasted_iota(jnp.int32, (SGU_BLOCK, SGU_BLOCK), 0) // CHUNK
    jj = lax.broadcasted_iota(jnp.int32, (SGU_BLOCK, SGU_BLOCK), 1) // CHUNK
    causal = jj <= ii
    ws = [jnp.where(causal, ws_ref[gi], 0.0).astype(BF16) for gi in range(SGU_GROUPS)]
    bs = bs_ref[...]
    width = w_ref.shape[1] // 2
    gd = width // SGU_GROUPS
    for st in range(tm // sub):
        r0 = st * sub
        h = _rms(x_ref[r0:r0 + sub, :], g_ref[...]).astype(BF16)
        z = _dot(h, w_ref[...]) + b_ref[...]
        z = 0.5 * z * (1.0 + lax.erf(z * (2.0 ** -0.5)))
        u = z[:, :width]
        v = _rms(z[:, width:], vg_ref[...]).astype(BF16)
        for gi in range(SGU_GROUPS):
            cs = slice(gi * gd, (gi + 1) * gd)
            for nb in range(sub // SGU_BLOCK):
                rs = slice(nb * SGU_BLOCK, (nb + 1) * SGU_BLOCK)
                mixed = _dot(ws[gi], v[rs, cs]) + bs[:, gi:gi + 1]
                o_ref[r0 + nb * SGU_BLOCK:r0 + (nb + 1) * SGU_BLOCK, cs] = (u[rs, cs] * mixed).astype(BF16)


def _spatial_gating(x, g, w_in, b_in, v_gain, w_s, b_s, w_o, layer):
    b, s, d = x.shape
    n = b * s
    x2 = x.reshape(n, d)
    width = w_in.shape[1] // 2
    tm = min(SGU_ROWS, s)
    row = lambda i: (i, 0)
    gated = pl.pallas_call(
        functools.partial(_sgu_kernel, tm=tm, sub=min(SGU_SUB_ROWS, tm)),
        grid=(n // tm,),
        in_specs=[pl.BlockSpec((tm, d), row), _resident((1, d)), _resident((d, 2 * width)),
                  _resident((1, 2 * width)), _resident((1, width)),
                  _resident((SGU_GROUPS, SGU_BLOCK, SGU_BLOCK)), _resident((SGU_BLOCK, SGU_GROUPS))],
        out_specs=pl.BlockSpec((tm, width), row),
        out_shape=jax.ShapeDtypeStruct((n, width), BF16),
        compiler_params=_params("parallel"),
        name="sgu_gate",
    )(x2, g.reshape(1, d), w_in.astype(BF16), b_in.reshape(1, 2 * width), v_gain.reshape(1, width),
      w_s, b_s.T)
    return _proj_res(x2, gated, w_o, layer).reshape(b, s, d)


def kernel(x, norm_mix, norm_ffn, pool_w, pool_scale, attn_w_in, attn_q_gain, attn_k_gain, attn_w_o,
           sgu_w_in, sgu_b_in, sgu_v_gain, sgu_w_s, sgu_b_s, sgu_w_o, ffn_w_up, ffn_w_down):
    b, s, d = x.shape
    depth = norm_mix.shape[0]
    for i in range(depth):
        kind, j = i % 3, i // 3
        if kind == 0:
            x = _pool_mixer(x, norm_mix[i], pool_w[j].astype(BF16), pool_scale[j])
        elif kind == 1:
            x = _sparse_attention(x, norm_mix[i], attn_w_in, attn_q_gain[j], attn_k_gain[j],
                                  attn_w_o, j)
        else:
            x = _spatial_gating(x, norm_mix[i], sgu_w_in[j], sgu_b_in[j], sgu_v_gain[j], sgu_w_s[j],
                                sgu_b_s[j], sgu_w_o, j)
        x = _ffn(x.reshape(b * s, d), norm_ffn[i], ffn_w_up, ffn_w_down, i).reshape(b, s, d)
    return x
```
